```python
import math
import jax
import jax.numpy as jnp
from jax import lax
import numpy as np

D_MODEL = 1024
BATCH = 4
SEQ = 4096
DEPTH = 4

F32 = jnp.float32
GRID_W = 64
CTX_LEN = 256
N_MIXERS = 3
N_LAYERS_RWKV = (DEPTH + 2) // 3
N_LAYERS_NA = (DEPTH + 1) // 3
N_LAYERS_S5 = DEPTH // 3
D_INNER = D_MODEL
NORM_EPS = 1e-6

RWKV_HEAD_DIM = 64
RWKV_HEADS = D_INNER // RWKV_HEAD_DIM
RWKV_DECAY_LORA = 64
RWKV_ICLR_LORA = 64
RWKV_GN_EPS = 64e-5
RWKV_N_MIX = 6

NA_HEAD_DIM = 64
NA_HEADS = D_INNER // NA_HEAD_DIM
NA_KH_MAX = 8
NA_KW = 16

S5_GROUP = 16
S5_GROUPS = D_INNER // S5_GROUP
S5_STATE = 64
S5_DT_MIN = 1e-3
S5_DT_MAX = 1e-1

kernel_name = "hybrid_rwkv7_natten_s5_prefix_block"


def _rms_norm(x, g):
    xf = x.astype(F32)
    y = xf * lax.rsqrt(jnp.mean(xf * xf, axis=-1, keepdims=True) + NORM_EPS)
    return (y * g.astype(F32)).astype(x.dtype)


def _head_rms(x, g):
    xf = x.astype(F32)
    y = xf * lax.rsqrt(jnp.mean(xf * xf, axis=-1, keepdims=True) + NORM_EPS)
    return (y * g.astype(F32)).astype(x.dtype)


def _head_l2norm(x, n_heads):
    xf = x.astype(F32).reshape(x.shape[:-1] + (n_heads, -1))
    n = jnp.sqrt(jnp.sum(xf * xf, axis=-1, keepdims=True))
    return (xf / jnp.maximum(n, 1e-12)).reshape(x.shape)


def _centred_shift(h):
    prev = jnp.pad(h[:, :-1], ((0, 0), (1, 0), (0, 0)))
    nxt = jnp.pad(h[:, 1:], ((0, 0), (0, 1), (0, 0)))
    return 0.5 * (prev + nxt)


def _seg_flip(a, n_ctx, axis):
    a_c, a_l = jnp.split(a, [n_ctx], axis=axis)
    return jnp.concatenate([jnp.flip(a_c, axis), jnp.flip(a_l, axis)], axis=axis)


def _rwkv7_step(state, inp):
    r, w, k, v, kk, a = inp
    sa = jnp.einsum("dbhvk,dbhk->dbhv", state, kk)
    state = (state * w[..., None, :] - sa[..., None] * (kk * a)[..., None, :]
             + v[..., None] * k[..., None, :])
    return state, jnp.einsum("dbhvk,dbhk->dbhv", state, r)


def _complex_affine_combine(e1, e2):
    a1r, a1i, b1r, b1i = e1
    a2r, a2i, b2r, b2i = e2
    return (a2r * a1r - a2i * a1i,
            a2r * a1i + a2i * a1r,
            a2r * b1r - a2i * b1i + b2r,
            a2r * b1i + a2i * b1r + b2i)


def _rwkv7_mixer(hc, hx, mu, w_rkvg, w0, w1, w2, a0, a1, a2, k_k, k_a, r_k,
                 ln_w, ln_b, w_out, need_ctx):
    dt = hx.dtype
    n_ctx = hc.shape[1]
    h = jnp.concatenate([hc, hx], axis=1)
    bsz, t_all, _ = h.shape
    xx = jnp.concatenate([_centred_shift(hc), _centred_shift(hx)], axis=1) - h
    xs = h[None] + xx[None] * mu[:, None, None, :]
    r, k, v, g = jnp.einsum("nbtd,nde->nbte", xs[:4], w_rkvg)
    dec_lora = jnp.einsum("sbtr,sre->sbte",
                          jnp.tanh(jnp.einsum("btd,sdr->sbtr", xs[4], w1)), w2)
    w_log = -jax.nn.softplus(-(w0[:, None, None, :] + dec_lora).astype(F32)) - 0.5
    decay = jnp.exp(-jnp.exp(w_log))
    iclr = jax.nn.sigmoid((a0[:, None, None, :] + jnp.einsum(
        "sbtr,sre->sbte", jnp.einsum("btd,sdr->sbtr", xs[5], a1), a2)).astype(F32))
    kk = _head_l2norm(k * k_k, RWKV_HEADS)
    k_dir = k.astype(F32)[None] * (1.0 + (iclr - 1.0) * k_a.astype(F32))

    def to_scan(a):
        a = jnp.stack([a[0], _seg_flip(a[1], n_ctx, 1)]).astype(F32)
        return a.reshape(2, bsz, t_all, RWKV_HEADS, RWKV_HEAD_DIM).transpose(2, 0, 1, 3, 4)

    def shared(a):
        return to_scan(jnp.stack([a, a]))

    s0 = jnp.zeros((2, bsz, RWKV_HEADS, RWKV_HEAD_DIM, RWKV_HEAD_DIM), F32)
    _, y = lax.scan(_rwkv7_step, s0, (shared(r), to_scan(decay), to_scan(k_dir),
                                      shared(v), shared(kk), to_scan(iclr)))
    y = y.transpose(1, 2, 0, 3, 4)
    y = y[0] + _seg_flip(y[1], n_ctx, 1)
    if not need_ctx:
        y, r, k, v, g = (a[:, n_ctx:] for a in (y, r, k, v, g))

    def heads(a):
        return a.astype(F32).reshape(a.shape[0], a.shape[1], RWKV_HEADS, RWKV_HEAD_DIM)

    mean = jnp.mean(y, axis=-1, keepdims=True)
    var = jnp.mean(jnp.square(y - mean), axis=-1, keepdims=True)
    yn = ((y - mean) * lax.rsqrt(var + RWKV_GN_EPS)
          * ln_w.astype(F32).reshape(RWKV_HEADS, RWKV_HEAD_DIM)
          + ln_b.astype(F32).reshape(RWKV_HEADS, RWKV_HEAD_DIM))
    bonus = jnp.sum(heads(r) * heads(k) * r_k.astype(F32), axis=-1, keepdims=True) * heads(v)
    o = (yn + bonus).reshape(bsz, -1, D_INNER) * jax.nn.silu(g.astype(F32))
    o = o.astype(dt) @ w_out
    if need_ctx:
        return o[:, :n_ctx], o[:, n_ctx:]
    return None, o


def _na_mixer(hc, hx, w_in, q_g, k_g, rpb, w_out, need_ctx):
    dt = hx.dtype
    bsz, n_lat, _ = hx.shape
    n_ctx = hc.shape[1]
    rows = n_lat // GRID_W
    kh = min(NA_KH_MAX, rows)
    n_nb = kh * NA_KW
    scale = NA_HEAD_DIM ** -0.5

    def project(h):
        q, k, v, z = jnp.split(h @ w_in, 4, axis=-1)
        split = lambda a: a.reshape(a.shape[0], a.shape[1], NA_HEADS, NA_HEAD_DIM)
        return _head_rms(split(q), q_g), _head_rms(split(k), k_g), split(v), z

    qc, kc, vc, zc = project(hc)
    qx, kx, vx, zx = project(hx)
    kc_h = kc.transpose(0, 2, 1, 3)
    vc_h = vc.transpose(0, 2, 1, 3)
    grid = lambda a: a.reshape(bsz, rows, GRID_W, NA_HEADS, NA_HEAD_DIM).transpose(0, 3, 1, 2, 4)
    qg, kg, vg = grid(qx), grid(kx), grid(vx)

    row_start = jnp.clip(jnp.arange(rows) - kh // 2, 0, rows - kh)
    cols = jnp.arange(GRID_W)
    col_idx = (jnp.clip(cols - NA_KW // 2, 0, GRID_W - NA_KW)[:, None]
               + jnp.arange(NA_KW)[None, :])
    rpb_cols = rpb[:, :, col_idx - cols[:, None] + NA_KW - 1].astype(F32)

    def row_block(i):
        r0 = row_start[i]
        qi = lax.dynamic_index_in_dim(qg, i, axis=2, keepdims=False)
        k_nb = lax.dynamic_slice_in_dim(kg, r0, kh, axis=2)[:, :, :, col_idx]
        v_nb = lax.dynamic_slice_in_dim(vg, r0, kh, axis=2)[:, :, :, col_idx]
        bias = jnp.take(rpb_cols, r0 + jnp.arange(kh) - i + NA_KH_MAX - 1,
                        axis=1).transpose(0, 2, 1, 3)
        s_nb = jnp.einsum("bhqd,bhrqcd->bhqrc", qi, k_nb).astype(F32) * scale + bias[None]
        s_ctx = jnp.einsum("bhqd,bhld->bhql", qi, kc_h).astype(F32) * scale
        p = jax.nn.softmax(jnp.concatenate(
            [s_nb.reshape(bsz, NA_HEADS, GRID_W, n_nb), s_ctx], axis=-1), axis=-1).astype(dt)
        p_nb = p[..., :n_nb].reshape(bsz, NA_HEADS, GRID_W, kh, NA_KW)
        return (jnp.einsum("bhqrc,bhrqcd->bhqd", p_nb, v_nb)
                + jnp.einsum("bhql,bhld->bhqd", p[..., n_nb:], vc_h))

    o_x = lax.map(row_block, jnp.arange(rows))
    o_x = o_x.transpose(1, 0, 3, 2, 4).reshape(bsz, n_lat, D_INNER)
    out_x = (o_x * jax.nn.silu(zx)) @ w_out
    if not need_ctx:
        return None, out_x
    qc_h = qc.transpose(0, 2, 1, 3)
    p_c = jax.nn.softmax(jnp.einsum("bhqd,bhkd->bhqk", qc_h, kc_h).astype(F32) * scale,
                         axis=-1).astype(dt)
    o_c = jnp.einsum("bhqk,bhkd->bhqd", p_c, vc_h).transpose(0, 2, 1, 3).reshape(bsz, n_ctx, D_INNER)
    return (o_c * jax.nn.silu(zc)) @ w_out, out_x


def _s5_mixer(hc, hx, w_in, lam_re, lam_im, log_dt, b_re, b_im, c_re, c_im, d_skip,
              w_glu, b_glu, w_out, need_ctx):
    dt = hx.dtype
    n_ctx = hc.shape[1]
    h = jnp.concatenate([hc, hx], axis=1)
    bsz, t_all, _ = h.shape
    u, z = jnp.split(h @ w_in, 2, axis=-1)
    u_t = jnp.swapaxes(u.astype(F32).reshape(bsz, t_all, S5_GROUPS, S5_GROUP), 0, 1)
    br, bi = b_re.astype(F32), b_im.astype(F32)
    cr, ci = c_re.astype(F32), c_im.astype(F32)
    y = jnp.zeros_like(u_t)
    for s in range(2):
        lr, li = lam_re[s].astype(F32), lam_im[s].astype(F32)
        step = jnp.exp(log_dt[s].astype(F32))[:, None]
        mag = jnp.exp(lr * step)
        ar, ai = mag * jnp.cos(li * step), mag * jnp.sin(li * step)
        den = lr * lr + li * li
        qr = ((ar - 1.0) * lr + ai * li) / den
        qi = (ai * lr - (ar - 1.0) * li) / den
        bbr = qr[..., None] * br - qi[..., None] * bi
        bbi = qr[..., None] * bi + qi[..., None] * br
        u_s = u_t if s == 0 else _seg_flip(u_t, n_ctx, 0)
        bu_r = jnp.einsum("gpc,tbgc->tbgp", bbr, u_s)
        bu_i = jnp.einsum("gpc,tbgc->tbgp", bbi, u_s)
        a_shape = (t_all, 1, S5_GROUPS, S5_STATE)
        _, _, xr, xi = lax.associative_scan(
            _complex_affine_combine,
            (jnp.broadcast_to(ar, a_shape), jnp.broadcast_to(ai, a_shape), bu_r, bu_i), axis=0)
        y_s = jnp.einsum("gcp,tbgp->tbgc", cr, xr) - jnp.einsum("gcp,tbgp->tbgc", ci, xi)
        y = y + (y_s if s == 0 else _seg_flip(y_s, n_ctx, 0))
    y = jnp.swapaxes(y, 0, 1).reshape(bsz, t_all, D_INNER) + d_skip.astype(F32) * u.astype(F32)
    if not need_ctx:
        y, z = y[:, n_ctx:], z[:, n_ctx:]
    y = jax.nn.gelu(y).astype(dt)
    y = y * jax.nn.sigmoid(y @ w_glu + b_glu)
    o = (y * jax.nn.silu(z)) @ w_out
    if need_ctx:
        return o[:, :n_ctx], o[:, n_ctx:]
    return None, o


def setup_inputs(seed: int = 0) -> dict:
    key = jax.random.key(seed)
    keys = iter(jax.random.split(key, 64))
    nrm = lambda shape, s: jax.random.normal(next(keys), shape, F32) * s
    D, E = D_MODEL, D_INNER
    na, nb, nc = N_LAYERS_RWKV, N_LAYERS_NA, N_LAYERS_S5
    inp = {}
    inp["x"] = nrm((BATCH, SEQ, D), 1.0)
    inp["c"] = nrm((BATCH, D), 1.0)
    inp["ctx"] = nrm((BATCH, CTX_LEN, D), 1.0)
    inp["c_ctx"] = nrm((D,), 1.0)
    inp["norm_g"] = 1.0 + nrm((DEPTH, D), 0.02)
    inp["w_mod"] = nrm((DEPTH, D, 3 * D), 0.5 * D ** -0.5)
    inp["b_mod"] = nrm((DEPTH, 3 * D), 0.02)
    inp["rwkv_mu"] = jax.random.uniform(next(keys), (na, RWKV_N_MIX, D), F32)
    inp["rwkv_w_rkvg"] = nrm((na, 4, D, E), D ** -0.5)
    ramp = -6.0 + 5.0 * jnp.linspace(0.0, 1.0, E, dtype=F32)
    inp["rwkv_w0"] = ramp[None, None, :] + nrm((na, 2, E), 0.1)
    inp["rwkv_w1"] = nrm((na, 2, D, RWKV_DECAY_LORA), D ** -0.5)
    inp["rwkv_w2"] = nrm((na, 2, RWKV_DECAY_LORA, E), 0.1 * RWKV_DECAY_LORA ** -0.5)
    inp["rwkv_a0"] = nrm((na, 2, E), 0.1)
    inp["rwkv_a1"] = nrm((na, 2, D, RWKV_ICLR_LORA), D ** -0.5)
    inp["rwkv_a2"] = nrm((na, 2, RWKV_ICLR_LORA, E), 0.1 * RWKV_ICLR_LORA ** -0.5)
    inp["rwkv_k_k"] = 0.85 + nrm((na, E), 0.02)
    inp["rwkv_k_a"] = 1.0 + nrm((na, E), 0.02)
    inp["rwkv_r_k"] = nrm((na, RWKV_HEADS, RWKV_HEAD_DIM), 0.1)
    inp["rwkv_ln_w"] = 1.0 + nrm((na, E), 0.02)
    inp["rwkv_ln_b"] = nrm((na, E), 0.02)
    inp["rwkv_w_out"] = nrm((na, E, D), E ** -0.5)
    inp["na_w_in"] = nrm((nb, D, 4 * E), D ** -0.5)
    inp["na_q_g"] = 1.0 + nrm((nb, NA_HEAD_DIM), 0.02)
    inp["na_k_g"] = 1.0 + nrm((nb, NA_HEAD_DIM), 0.02)
    inp["na_rpb"] = nrm((nb, NA_HEADS, 2 * NA_KH_MAX - 1, 2 * NA_KW - 1), 0.1)
    inp["na_w_out"] = nrm((nb, E, D), E ** -0.5)
    inp["s5_w_in"] = nrm((nc, D, 2 * E), D ** -0.5)
    inp["s5_lam_re"] = -0.5 + nrm((nc, 2, S5_GROUPS, S5_STATE), 0.01)
    inp["s5_lam_im"] = (math.pi * jnp.arange(S5_STATE, dtype=F32))[None, None, None, :] + nrm(
        (nc, 2, S5_GROUPS, S5_STATE), 0.01)
    inp["s5_log_dt"] = jax.random.uniform(next(keys), (nc, 2, S5_GROUPS), F32,
                                          math.log(S5_DT_MIN), math.log(S5_DT_MAX))
    inp["s5_b_re"] = nrm((nc, S5_GROUPS, S5_STATE, S5_GROUP), (2 * S5_GROUP) ** -0.5)
    inp["s5_b_im"] = nrm((nc, S5_GROUPS, S5_STATE, S5_GROUP), (2 * S5_GROUP) ** -0.5)
    inp["s5_c_re"] = nrm((nc, S5_GROUPS, S5_GROUP, S5_STATE), S5_STATE ** -0.5)
    inp["s5_c_im"] = nrm((nc, S5_GROUPS, S5_GROUP, S5_STATE), S5_STATE ** -0.5)
    inp["s5_d"] = nrm((nc, E), 0.5)
    inp["s5_w_glu"] = nrm((nc, E, E), E ** -0.5)
    inp["s5_b_glu"] = nrm((nc, E), 0.02)
    inp["s5_w_out"] = nrm((nc, E, D), E ** -0.5)
    return inp


def reference(x, c, ctx, c_ctx, norm_g, w_mod, b_mod,
              rwkv_mu, rwkv_w_rkvg, rwkv_w0, rwkv_w1, rwkv_w2, rwkv_a0, rwkv_a1, rwkv_a2,
              rwkv_k_k, rwkv_k_a, rwkv_r_k, rwkv_ln_w, rwkv_ln_b, rwkv_w_out,
              na_w_in, na_q_g, na_k_g, na_rpb, na_w_out,
              s5_w_in, s5_lam_re, s5_lam_im, s5_log_dt, s5_b_re, s5_b_im, s5_c_re, s5_c_im,
              s5_d, s5_w_glu, s5_b_glu, s5_w_out):
    dt = x.dtype
    silu_c = jax.nn.silu(c.astype(F32))
    silu_cc = jax.nn.silu(c_ctx.astype(F32))
    for i in range(DEPTH):
        kind, j = i % N_MIXERS, i // N_MIXERS
        need_ctx = i < DEPTH - 1
        wm, bm = w_mod[i].astype(F32), b_mod[i].astype(F32)
        mod_x = (silu_c @ wm + bm).astype(dt)
        mod_c = (silu_cc @ wm + bm).astype(dt)
        shift_x, scale_x, gate_x = jnp.split(mod_x[:, None, :], 3, axis=-1)
        shift_c, scale_c, gate_c = jnp.split(mod_c, 3, axis=-1)
        hx = _rms_norm(x, norm_g[i]) * (1.0 + scale_x) + shift_x
        hc = _rms_norm(ctx, norm_g[i]) * (1.0 + scale_c) + shift_c
        if kind == 0:
            oc, ox = _rwkv7_mixer(hc, hx, rwkv_mu[j], rwkv_w_rkvg[j], rwkv_w0[j], rwkv_w1[j],
                                  rwkv_w2[j], rwkv_a0[j], rwkv_a1[j], rwkv_a2[j], rwkv_k_k[j],
                                  rwkv_k_a[j], rwkv_r_k[j], rwkv_ln_w[j], rwkv_ln_b[j],
                                  rwkv_w_out[j], need_ctx)
        elif kind == 1:
            oc, ox = _na_mixer(hc, hx, na_w_in[j], na_q_g[j], na_k_g[j], na_rpb[j],
                               na_w_out[j], need_ctx)
        else:
            oc, ox = _s5_mixer(hc, hx, s5_w_in[j], s5_lam_re[j], s5_lam_im[j], s5_log_dt[j],
                               s5_b_re[j], s5_b_im[j], s5_c_re[j], s5_c_im[j], s5_d[j],
                               s5_w_glu[j], s5_b_glu[j], s5_w_out[j], need_ctx)
        x = x + gate_x * ox
        if need_ctx:
            ctx = ctx + gate_c * oc
    return x
```

```python
import functools
import math

import numpy as np
import jax
import jax.numpy as jnp
from jax import lax
from jax.experimental import pallas as pl
from jax.experimental.pallas import tpu as pltpu

F32 = jnp.float32
BF16 = jnp.bfloat16
NORM_EPS = 1e-6
RWKV_GN_EPS = 64e-5
HEAD_DIM = 64
RWKV_CHUNK = 64
RWKV_HEADS_PER_STEP = 8
S5_CHUNK = 16
S5_GROUP = 16
MASK_NEG = -1e30
VMEM_LIMIT = 48 * 1024 * 1024
HIGHEST = lax.Precision.HIGHEST


def _cparams(*sem):
    return pltpu.CompilerParams(dimension_semantics=sem, vmem_limit_bytes=VMEM_LIMIT)


def _token_tile(n_ctx, n_all):
    for t in (256, 128, 64):
        if n_ctx % t == 0 and n_all % t == 0:
            return t
    raise ValueError("context / sequence lengths must be multiples of 64")


def _mod_row(i, tiles_per_b, ctx_tiles, n_batch):
    return jnp.where(i % tiles_per_b < ctx_tiles, n_batch, i // tiles_per_b)


def _mod_kernel(c_ref, w_ref, b_ref, o_ref):
    c = c_ref[...]
    s = c * jax.nn.sigmoid(c)
    o_ref[...] = jnp.dot(s, w_ref[...], precision=HIGHEST, preferred_element_type=F32) + b_ref[...]


def _modulation(cc, wm, bm):
    rows, d = cc.shape
    n = wm.shape[1]
    tn = 512
    return pl.pallas_call(
        _mod_kernel,
        grid=(n // tn,),
        in_specs=[pl.BlockSpec((rows, d), lambda j: (0, 0)),
                  pl.BlockSpec((d, tn), lambda j: (0, j)),
                  pl.BlockSpec((1, tn), lambda j: (0, j))],
        out_specs=pl.BlockSpec((rows, tn), lambda j: (0, j)),
        out_shape=jax.ShapeDtypeStruct((rows, n), F32),
        compiler_params=_cparams("arbitrary"),
        name="modulation",
    )(cc, wm, bm.reshape(1, n))


def _norm_kernel(x_ref, g_ref, sc_ref, sh_ref, o_ref):
    x = x_ref[...]
    ms = jnp.mean(x * x, axis=-1, keepdims=True)
    y = x * lax.rsqrt(ms + NORM_EPS) * g_ref[...]
    o_ref[...] = y * (1.0 + sc_ref[0]) + sh_ref[0]


def _norm_mod(x2, g, scale, shift, tm, tiles_per_b, ctx_tiles, n_batch):
    m, d = x2.shape
    row = lambda i: (_mod_row(i, tiles_per_b, ctx_tiles, n_batch), 0, 0)
    return pl.pallas_call(
        _norm_kernel,
        grid=(m // tm,),
        in_specs=[pl.BlockSpec((tm, d), lambda i: (i, 0)),
                  pl.BlockSpec((1, d), lambda i: (0, 0)),
                  pl.BlockSpec((1, 1, d), row),
                  pl.BlockSpec((1, 1, d), row)],
        out_specs=pl.BlockSpec((tm, d), lambda i: (i, 0)),
        out_shape=jax.ShapeDtypeStruct((m, d), F32),
        compiler_params=_cparams("arbitrary"),
        name="norm_mod",
    )(x2, g.reshape(1, d), scale, shift)


def _mm_kernel(x_ref, w_ref, o_ref):
    o_ref[...] = jnp.dot(x_ref[...].astype(BF16), w_ref[...], preferred_element_type=F32)


def _mm_bias_kernel(x_ref, w_ref, b_ref, o_ref):
    o_ref[...] = jnp.dot(x_ref[...].astype(BF16), w_ref[...], preferred_element_type=F32) + b_ref[...]


def _mm(x2, w, tm, bias=None):
    m, k = x2.shape
    n = w.shape[1]
    tn = n if n <= 2048 else 2048
    in_specs = [pl.BlockSpec((tm, k), lambda i, j: (i, 0)),
                pl.BlockSpec((k, tn), lambda i, j: (0, j))]
    args = [x2, w.astype(BF16)]
    kern = _mm_kernel
    if bias is not None:
        in_specs.append(pl.BlockSpec((1, tn), lambda i, j: (0, j)))
        args.append(bias.reshape(1, n).astype(F32))
        kern = _mm_bias_kernel
    return pl.pallas_call(
        kern,
        grid=(m // tm, n // tn),
        in_specs=in_specs,
        out_specs=pl.BlockSpec((tm, tn), lambda i, j: (i, j)),
        out_shape=jax.ShapeDtypeStruct((m, n), F32),
        compiler_params=_cparams("arbitrary", "arbitrary"),
        name="matmul",
    )(*args)


def _mm_res_kernel(x_ref, w_ref, res_ref, gate_ref, o_ref):
    acc = jnp.dot(x_ref[...].astype(BF16), w_ref[...], preferred_element_type=F32)
    o_ref[...] = res_ref[...] + gate_ref[0] * acc


def _mm_residual(x2, w, res2, gate, tm, tiles_per_b, ctx_tiles, n_batch):
    m, k = x2.shape
    n = w.shape[1]
    return pl.pallas_call(
        _mm_res_kernel,
        grid=(m // tm,),
        in_specs=[pl.BlockSpec((tm, k), lambda i: (i, 0)),
                  pl.BlockSpec((k, n), lambda i: (0, 0)),
                  pl.BlockSpec((tm, n), lambda i: (i, 0)),
                  pl.BlockSpec((1, 1, n), lambda i: (_mod_row(i, tiles_per_b, ctx_tiles, n_batch), 0, 0))],
        out_specs=pl.BlockSpec((tm, n), lambda i: (i, 0)),
        out_shape=jax.ShapeDtypeStruct((m, n), F32),
        compiler_params=_cparams("arbitrary"),
        name="matmul_residual",
    )(x2, w.astype(BF16), res2, gate)


def _gmm_kernel(x_ref, w_ref, o_ref):
    o_ref[0] = jnp.dot(x_ref[0].astype(BF16), w_ref[0], preferred_element_type=F32)


def _gmm_add_kernel(x_ref, w_ref, a_ref, o_ref):
    o_ref[0] = a_ref[0] + jnp.dot(x_ref[0].astype(BF16), w_ref[0], preferred_element_type=F32)


def _grouped_mm(x3, w3, add=None, add_col_block=0):
    g, m, k = x3.shape
    n = w3.shape[2]
    in_specs = [pl.BlockSpec((1, m, k), lambda i: (i, 0, 0)),
                pl.BlockSpec((1, k, n), lambda i: (i, 0, 0))]
    args = [x3, w3.astype(BF16)]
    kern = _gmm_kernel
    if add is not None:
        in_specs.append(pl.BlockSpec((1, m, n), lambda i: (i, 0, add_col_block)))
        args.append(add)
        kern = _gmm_add_kernel
    return pl.pallas_call(
        kern,
        grid=(g,),
        in_specs=in_specs,
        out_specs=pl.BlockSpec((1, m, n), lambda i: (i, 0, 0)),
        out_shape=jax.ShapeDtypeStruct((g, m, n), F32),
        compiler_params=_cparams("arbitrary"),
        name="grouped_matmul",
    )(*args)


def _dot_nt(a, b):
    return lax.dot_general(a, b, (((1,), (1,)), ((), ())), preferred_element_type=F32)


def _dot_tn(a, b):
    return lax.dot_general(a, b, (((0,), (0,)), ((), ())), preferred_element_type=F32)


def _dot(a, b):
    return jnp.dot(a, b, preferred_element_type=F32)


def _rwkv_chunk(r, v, kk, lw, kd, bd, s0, incl, strict, incl_f, eye):
    lp = jnp.dot(incl_f, lw, precision=HIGHEST, preferred_element_type=F32)
    lp_tot = jnp.sum(lw, axis=0, keepdims=True)
    r_t = r * jnp.exp(lp)
    kk_t = kk * jnp.exp(lp - lw)
    e_ninc = jnp.exp(-lp)
    k_h = (kd * e_ninc).astype(BF16)
    b_h = (bd * e_ninc).astype(BF16)
    e_rem = jnp.exp(lp_tot - lp)
    k_p = kd * e_rem
    b_p = bd * e_rem

    lhs = jnp.concatenate([kk_t, r_t], axis=0).astype(BF16)
    a_k = _dot_nt(lhs, k_h)
    a_b = _dot_nt(lhs, b_h)
    c = r.shape[0]
    a_kv = jnp.where(strict, a_k[:c], 0.0)
    n_mat = jnp.where(strict, a_b[:c], 0.0)
    a_rk = jnp.where(incl, a_k[c:], 0.0)
    a_rb = jnp.where(incl, a_b[c:], 0.0)

    t_inv = eye - n_mat
    pw = n_mat.astype(BF16)
    steps = int(math.log2(c)) - 1
    for i in range(steps):
        pw_f = _dot(pw, pw)
        pw = pw_f.astype(BF16)
        t_inv = t_inv + _dot(t_inv.astype(BF16), pw)

    ls = _dot_nt(lhs, s0.astype(BF16))
    vb = v.astype(BF16)
    z = ls[:c] + _dot(a_kv.astype(BF16), vb)
    u = _dot(t_inv.astype(BF16), z.astype(BF16))
    ub = u.astype(BF16)
    y = ls[c:] + _dot(a_rk.astype(BF16), vb) - _dot(a_rb.astype(BF16), ub)
    upd = _dot_tn(jnp.concatenate([v, -u], axis=0).astype(BF16),
                  jnp.concatenate([k_p, b_p], axis=0).astype(BF16))
    s1 = s0 * jnp.exp(lp_tot) + upd
    return y, s1


def _rwkv_scan_kernel(r_ref, v_ref, kk_ref, lw_ref, kd_ref, bd_ref, y_ref, s_ref, *, nh, fwd_groups):
    g = pl.program_id(0)
    c_idx = pl.program_id(1)

    @pl.when(c_idx == 0)
    def _():
        s_ref[...] = jnp.zeros_like(s_ref)

    c = r_ref.shape[1]
    sign = jnp.where(g < fwd_groups, 1, -1)
    row = lax.broadcasted_iota(jnp.int32, (c, c), 0)
    col = lax.broadcasted_iota(jnp.int32, (c, c), 1)
    d = (row - col) * sign
    incl = d >= 0
    strict = d > 0
    incl_f = incl.astype(F32)
    eye = (row == col).astype(F32)
    for i in range(nh):
        y, s1 = _rwkv_chunk(r_ref[i], v_ref[i], kk_ref[i], lw_ref[i], kd_ref[i], bd_ref[i],
                            s_ref[i], incl, strict, incl_f, eye)
        y_ref[i] = y
        s_ref[i] = s1


def _rwkv_scan(r, v, kk, lw, kd, bd, n_ctx):
    bh, t, hd = r.shape
    nh = RWKV_HEADS_PER_STEP
    c = RWKV_CHUNK
    nc, nc_ctx = t // c, n_ctx // c
    fwd_groups = bh // nh

    def chunk(g, ci):
        rev = jnp.where(ci < nc_ctx, nc_ctx - 1 - ci, nc + nc_ctx - 1 - ci)
        return jnp.where(g < fwd_groups, ci, rev)

    shared = pl.BlockSpec((nh, c, hd), lambda g, ci: (g % fwd_groups, chunk(g, ci), 0))
    per_dir = pl.BlockSpec((nh, c, hd), lambda g, ci: (g, chunk(g, ci), 0))
    return pl.pallas_call(
        functools.partial(_rwkv_scan_kernel, nh=nh, fwd_groups=fwd_groups),
        grid=(2 * fwd_groups, nc),
        in_specs=[shared, shared, shared, per_dir, per_dir, per_dir],
        out_specs=per_dir,
        out_shape=jax.ShapeDtypeStruct((2 * bh, t, hd), F32),
        scratch_shapes=[pltpu.VMEM((nh, hd, hd), F32)],
        compiler_params=_cparams("arbitrary", "arbitrary"),
        name="rwkv_scan",
    )(r, v, kk, lw, kd, bd)


def _centred_shift(h):
    prev = jnp.pad(h[:, :-1], ((0, 0), (1, 0), (0, 0)))
    nxt = jnp.pad(h[:, 1:], ((0, 0), (0, 1), (0, 0)))
    return 0.5 * (prev + nxt)


def _rwkv_layer(stream, h, gate, n_ctx, tm, tile_args, mu, w_rkvg, w0, w1, w2, a0, a1, a2, k_k, k_a, r_k,
                ln_w, ln_b, w_out):
    bsz, t_all, d = stream.shape
    e = w_rkvg.shape[-1]
    nheads = e // HEAD_DIM
    m = bsz * t_all
    xx = jnp.concatenate([_centred_shift(h[:, :n_ctx]), _centred_shift(h[:, n_ctx:])], axis=1) - h
    xs = [(h + xx * mu[i]).reshape(m, d) for i in range(mu.shape[0])]
    r, k, v, g = (_mm(xs[i], w_rkvg[i], tm) for i in range(4))
    lr = w1.shape[-1]
    zeros = jnp.zeros((lr, e), F32)

    def lora(x2, wa, wb, act):
        mid = act(_mm(x2, jnp.concatenate([wa[0], wa[1]], axis=1), tm))
        wbd = jnp.concatenate([jnp.concatenate([wb[0], zeros], axis=1),
                               jnp.concatenate([zeros, wb[1]], axis=1)], axis=0)
        return _mm(mid, wbd, tm).reshape(m, 2, e)

    dec_lora = lora(xs[4], w1, w2, jnp.tanh)
    w_log = -jax.nn.softplus(-(w0[None] + dec_lora)) - 0.5
    lw = -jnp.exp(w_log)
    iclr = jax.nn.sigmoid(a0[None] + lora(xs[5], a1, a2, lambda x: x))
    kkf = (k * k_k).reshape(m, nheads, HEAD_DIM)
    kk = (kkf / jnp.maximum(jnp.sqrt(jnp.sum(kkf * kkf, axis=-1, keepdims=True)), 1e-12)).reshape(m, e)
    kd = k[:, None, :] * (1.0 + (iclr - 1.0) * k_a)
    bd = kk[:, None, :] * iclr

    def heads(a):
        return a.reshape(bsz, t_all, nheads, HEAD_DIM).transpose(0, 2, 1, 3).reshape(bsz * nheads, t_all, HEAD_DIM)

    def heads2(a):
        return a.reshape(bsz, t_all, 2, nheads, HEAD_DIM).transpose(2, 0, 3, 1, 4).reshape(
            2 * bsz * nheads, t_all, HEAD_DIM)

    y2 = _rwkv_scan(heads(r), heads(v), heads(kk), heads2(lw), heads2(kd), heads2(bd), n_ctx)
    y2 = y2.reshape(2, bsz, nheads, t_all, HEAD_DIM)
    y = (y2[0] + y2[1]).transpose(0, 2, 1, 3)
    mean = jnp.mean(y, axis=-1, keepdims=True)
    var = jnp.mean(jnp.square(y - mean), axis=-1, keepdims=True)
    yn = ((y - mean) * lax.rsqrt(var + RWKV_GN_EPS) * ln_w.reshape(nheads, HEAD_DIM)
          + ln_b.reshape(nheads, HEAD_DIM))
    hd4 = lambda a: a.reshape(bsz, t_all, nheads, HEAD_DIM)
    bonus = jnp.sum(hd4(r) * hd4(k) * r_k, axis=-1, keepdims=True) * hd4(v)
    o = (yn + bonus).reshape(m, e) * jax.nn.silu(g)
    return _mm_residual(o, w_out, stream.reshape(m, d), gate, tm, *tile_args).reshape(bsz, t_all, d)


def _na_kernel(q_ref, k_ref, v_ref, bias_ref, qg_ref, kg_ref, o_ref, kn_ref, vb_ref, *,
               n_ctx, grid_w, kh, rows, scale):
    qi = pl.program_id(2)
    ctx_tiles = n_ctx // grid_w
    hd = HEAD_DIM
    lanes = 2 * hd
    same_head = (lax.broadcasted_iota(jnp.int32, (lanes, lanes), 0) // hd
                 == lax.broadcasted_iota(jnp.int32, (lanes, lanes), 1) // hd).astype(BF16)
    lane_head = lax.broadcasted_iota(jnp.int32, (1, lanes), 1) // hd

    def head_rms(x, g):
        ss = jnp.dot((x * x).astype(BF16), same_head, preferred_element_type=F32)
        return x * lax.rsqrt(ss * (1.0 / hd) + NORM_EPS) * g

    @pl.when(qi == 0)
    def _():
        kn_ref[...] = head_rms(k_ref[0], kg_ref[...]).astype(BF16)
        vb_ref[...] = v_ref[0].astype(BF16)

    qn = head_rms(q_ref[0], qg_ref[...])
    k_ctx = kn_ref[0:n_ctx, :]
    v_ctx = vb_ref[0:n_ctx, :]

    def attend(k_nb, v_nb, bias_of):
        out = jnp.zeros((grid_w, lanes), F32)
        for hh in range(2):
            sel = lane_head == hh
            qh = jnp.where(sel, qn, 0.0).astype(BF16)
            s_c = _dot_nt(qh, k_ctx) * scale
            m = jnp.max(s_c, axis=-1, keepdims=True)
            if k_nb is not None:
                s_n = _dot_nt(qh, k_nb) * scale + bias_of(hh)
                m = jnp.maximum(m, jnp.max(s_n, axis=-1, keepdims=True))
                p_n = jnp.exp(s_n - m)
            p_c = jnp.exp(s_c - m)
            den = jnp.sum(p_c, axis=-1, keepdims=True)
            acc = _dot(p_c.astype(BF16), v_ctx)
            if k_nb is not None:
                den = den + jnp.sum(p_n, axis=-1, keepdims=True)
                acc = acc + _dot(p_n.astype(BF16), v_nb)
            out = jnp.where(sel, acc / den, out)
        return out

    @pl.when(qi < ctx_tiles)
    def _():
        o_ref[0] = attend(None, None, None)

    @pl.when(qi >= ctx_tiles)
    def _():
        i = qi - ctx_tiles
        r0 = jnp.clip(i - kh // 2, 0, rows - kh)
        start = pl.multiple_of(n_ctx + r0 * grid_w, grid_w)
        k_nb = kn_ref[pl.ds(start, kh * grid_w), :]
        v_nb = vb_ref[pl.ds(start, kh * grid_w), :]
        o_ref[0] = attend(k_nb, v_nb, lambda hh: bias_ref[hh, 0])


def _na_bias_table(rpb, grid_w, kh):
    khm = (rpb.shape[1] + 1) // 2
    kw = (rpb.shape[2] + 1) // 2
    var = np.arange(kh)
    kr = np.arange(kh)
    ridx = kr[None, :] - var[:, None] + khm - 1
    j = np.arange(grid_w)
    c = np.arange(grid_w)
    c0 = np.clip(j - kw // 2, 0, grid_w - kw)
    valid = (c[None, :] >= c0[:, None]) & (c[None, :] < c0[:, None] + kw)
    cidx = np.clip(c[None, :] - j[:, None] + kw - 1, 0, 2 * kw - 2)
    tbl = rpb[:, ridx[:, :, None, None], cidx[None, None, :, :]].astype(F32)
    tbl = jnp.where(valid[None, None, None], tbl, MASK_NEG)
    return tbl.transpose(0, 1, 3, 2, 4).reshape(rpb.shape[0], kh, grid_w, kh * grid_w)


def _na_attention(qkvz, rpb, q_g, k_g, n_ctx, grid_w, kh_max):
    bsz, t_all, e4 = qkvz.shape
    e = e4 // 4
    pairs = e // (2 * HEAD_DIM)
    rows = (t_all - n_ctx) // grid_w
    kh = min(kh_max, rows)
    tbl = _na_bias_table(rpb, grid_w, kh)
    nq = t_all // grid_w
    ctx_tiles = n_ctx // grid_w

    def var_of(qi):
        i = jnp.maximum(qi - ctx_tiles, 0)
        return i - jnp.clip(i - kh // 2, 0, rows - kh)

    lanes = 2 * HEAD_DIM
    g2 = lambda g: jnp.concatenate([g, g]).reshape(1, lanes).astype(F32)
    return pl.pallas_call(
        functools.partial(_na_kernel, n_ctx=n_ctx, grid_w=grid_w, kh=kh, rows=rows, scale=HEAD_DIM ** -0.5),
        grid=(bsz, pairs, nq),
        in_specs=[pl.BlockSpec((1, grid_w, lanes), lambda b, p, qi: (b, qi, p)),
                  pl.BlockSpec((1, t_all, lanes), lambda b, p, qi: (b, 0, pairs + p)),
                  pl.BlockSpec((1, t_all, lanes), lambda b, p, qi: (b, 0, 2 * pairs + p)),
                  pl.BlockSpec((2, 1, grid_w, kh * grid_w), lambda b, p, qi: (p, var_of(qi), 0, 0)),
                  pl.BlockSpec((1, lanes), lambda b, p, qi: (0, 0)),
                  pl.BlockSpec((1, lanes), lambda b, p, qi: (0, 0))],
        out_specs=pl.BlockSpec((1, grid_w, lanes), lambda b, p, qi: (b, qi, p)),
        out_shape=jax.ShapeDtypeStruct((bsz, t_all, e), F32),
        scratch_shapes=[pltpu.VMEM((t_all, lanes), BF16), pltpu.VMEM((t_all, lanes), BF16)],
        compiler_params=_cparams("arbitrary", "arbitrary", "arbitrary"),
        name="na_attention",
    )(qkvz, qkvz, qkvz, tbl, g2(q_g), g2(k_g))


def _na_layer(stream, h, gate, n_ctx, tm, tile_args, grid_w, w_in, q_g, k_g, rpb, w_out):
    bsz, t_all, d = stream.shape
    m = bsz * t_all
    e = w_in.shape[1] // 4
    qkvz = _mm(h.reshape(m, d), w_in, tm).reshape(bsz, t_all, 4 * e)
    kh_max = (rpb.shape[1] + 1) // 2
    o = _na_attention(qkvz, rpb, q_g, k_g, n_ctx, grid_w, kh_max)
    o = (o * jax.nn.silu(qkvz[..., 3 * e:])).reshape(m, e)
    return _mm_residual(o, w_out, stream.reshape(m, d), gate, tm, *tile_args).reshape(bsz, t_all, d)


def _s5_matrices(lam_re, lam_im, log_dt, b_re, b_im, c_re, c_im):
    nt = S5_CHUNK
    g, p, cg = b_re.shape
    tau = jnp.arange(nt + 1, dtype=F32)[:, None, None]
    i_idx = np.arange(nt)
    kbig, bcat, ccat, a_pow = 0.0, [], [], []
    for s in range(2):
        lr, li = lam_re[s].astype(F32), lam_im[s].astype(F32)
        step = jnp.exp(log_dt[s].astype(F32))[:, None]
        mag = jnp.exp(lr * step)
        ar, ai = mag * jnp.cos(li * step), mag * jnp.sin(li * step)
        den = lr * lr + li * li
        qr = ((ar - 1.0) * lr + ai * li) / den
        qi = (ai * lr - (ar - 1.0) * li) / den
        bbr = qr[..., None] * b_re - qi[..., None] * b_im
        bbi = qr[..., None] * b_im + qi[..., None] * b_re
        pmag = jnp.exp(lr * step * tau)
        pr, pi = pmag * jnp.cos(li * step * tau), pmag * jnp.sin(li * step * tau)
        clr = c_re[None] * pr[:, :, None, :] - c_im[None] * pi[:, :, None, :]
        cli = c_re[None] * pi[:, :, None, :] + c_im[None] * pr[:, :, None, :]
        ker = (jnp.einsum("tgop,gpc->tgoc", clr[:nt], bbr, precision=HIGHEST)
               - jnp.einsum("tgop,gpc->tgoc", cli[:nt], bbi, precision=HIGHEST))
        lbr = pr[:, :, :, None] * bbr[None] - pi[:, :, :, None] * bbi[None]
        lbi = pr[:, :, :, None] * bbi[None] + pi[:, :, :, None] * bbr[None]
        lag = (i_idx[None, :] - i_idx[:, None]) if s == 0 else (i_idx[:, None] - i_idx[None, :])
        kb = jnp.where((lag >= 0)[:, :, None, None, None], ker[np.clip(lag, 0, nt - 1)], 0.0)
        kbig = kbig + kb.transpose(2, 0, 4, 1, 3).reshape(g, nt * cg, nt * cg)
        inj = (nt - 1 - i_idx) if s == 0 else i_idx
        bb = jnp.concatenate([lbr[inj], lbi[inj]], axis=2)
        bcat.append(bb.transpose(1, 0, 3, 2).reshape(g, nt * cg, 2 * p))
        out = (i_idx + 1) if s == 0 else (nt - i_idx)
        cc = jnp.concatenate([clr[out], -cli[out]], axis=3)
        ccat.append(cc.transpose(1, 3, 0, 2).reshape(g, 2 * p, nt * cg))
        a_pow.append((pr[nt], pi[nt]))
    w_in = jnp.concatenate([kbig] + bcat, axis=2)
    w_out = jnp.concatenate(ccat, axis=1)
    return w_in, w_out, a_pow


def _s5_state_kernel(bur_ref, bui_ref, ar_ref, ai_ref, xr_ref, xi_ref):
    nck = bur_ref.shape[0]
    ar = ar_ref[...]
    ai = ai_ref[...]

    def body(c, carry):
        xr, xi = carry
        xr_ref[c] = xr
        xi_ref[c] = xi
        return (ar * xr - ai * xi + bur_ref[c], ar * xi + ai * xr + bui_ref[c])

    zero = jnp.zeros(ar.shape, F32)
    lax.fori_loop(0, nck, body, (zero, zero))


def _s5_states(bur, bui, ar, ai):
    nck, rows, lanes = bur.shape
    tl = 512
    blk = pl.BlockSpec((nck, rows, tl), lambda j: (0, 0, j))
    coef = pl.BlockSpec((rows, tl), lambda j: (0, j))
    return pl.pallas_call(
        _s5_state_kernel,
        grid=(lanes // tl,),
        in_specs=[blk, blk, coef, coef],
        out_specs=[blk, blk],
        out_shape=[jax.ShapeDtypeStruct(bur.shape, F32)] * 2,
        compiler_params=_cparams("arbitrary"),
        name="s5_states",
    )(bur, bui, ar, ai)


def _s5_layer(stream, h, gate, n_ctx, tm, tile_args, w_in, lam_re, lam_im, log_dt, b_re, b_im, c_re, c_im,
              d_skip, w_glu, b_glu, w_out):
    bsz, t_all, d = stream.shape
    m = bsz * t_all
    e = w_in.shape[1] // 2
    nt, cg = S5_CHUNK, S5_GROUP
    g = e // cg
    p = b_re.shape[1]
    nck, nck_ctx = t_all // nt, n_ctx // nt
    uz = _mm(h.reshape(m, d), w_in, tm)
    u, z = uz[:, :e], uz[:, e:]
    k_in, k_out, a_pow = _s5_matrices(lam_re, lam_im, log_dt, b_re, b_im, c_re, c_im)
    u_flat = u.reshape(bsz, nck, nt, g, cg).transpose(3, 0, 1, 2, 4).reshape(g, bsz * nck, nt * cg)
    yb = _grouped_mm(u_flat, k_in)
    bu = yb[:, :, nt * cg:].reshape(g, bsz, nck, 2, 2, p)
    rev = np.concatenate([np.arange(nck_ctx)[::-1], np.arange(nck_ctx, nck)[::-1]])
    bu = bu.transpose(4, 2, 3, 1, 0, 5)
    bu = jnp.stack([bu[:, :, 0], bu[:, rev, 1]], axis=2).reshape(2, nck, 2 * bsz, g * p)
    coef = lambda a: jnp.broadcast_to(jnp.stack(a)[:, None], (2, bsz, g, p)).reshape(2 * bsz, g * p)
    xr, xi = _s5_states(bu[0], bu[1], coef([a_pow[0][0], a_pow[1][0]]), coef([a_pow[0][1], a_pow[1][1]]))
    inv = np.argsort(rev)

    def unorder(x):
        x = x.reshape(nck, 2, bsz, g, p)
        x = jnp.stack([x[:, 0], x[inv, 1]], axis=1)
        return x.transpose(3, 2, 0, 1, 4).reshape(g, bsz * nck, 2, p)

    xr, xi = unorder(xr), unorder(xi)
    x_cat = jnp.concatenate([xr[:, :, 0], xi[:, :, 0], xr[:, :, 1], xi[:, :, 1]], axis=-1)
    y_flat = _grouped_mm(x_cat, k_out, add=yb, add_col_block=0)
    y = y_flat.reshape(g, bsz, nck, nt, cg).transpose(1, 2, 3, 0, 4).reshape(m, e)
    y = jax.nn.gelu(y + d_skip * u)
    y = y * jax.nn.sigmoid(_mm(y, w_glu, tm, bias=b_glu))
    o = y * jax.nn.silu(z)
    return _mm_residual(o, w_out, stream.reshape(m, d), gate, tm, *tile_args).reshape(bsz, t_all, d)


def kernel(x, c, ctx, c_ctx, norm_g, w_mod, b_mod, rwkv_mu, rwkv_w_rkvg, rwkv_w0, rwkv_w1, rwkv_w2, rwkv_a0, rwkv_a1, rwkv_a2, rwkv_k_k, rwkv_k_a, rwkv_r_k, rwkv_ln_w, rwkv_ln_b, rwkv_w_out, na_w_in, na_q_g, na_k_g, na_rpb, na_w_out, s5_w_in, s5_lam_re, s5_lam_im, s5_log_dt, s5_b_re, s5_b_im, s5_c_re, s5_c_im, s5_d, s5_w_glu, s5_b_glu, s5_w_out):
    bsz, n_lat, d = x.shape
    n_ctx = ctx.shape[1]
    t_all = n_ctx + n_lat
    depth = norm_g.shape[0]
    grid_w = 64
    tm = _token_tile(n_ctx, t_all)
    tile_args = (t_all // tm, n_ctx // tm, bsz)
    stream = jnp.concatenate([ctx, x], axis=1).astype(F32)
    rows = 8 * ((bsz + 1 + 7) // 8)
    cc = jnp.zeros((rows, d), F32).at[:bsz].set(c.astype(F32)).at[bsz].set(c_ctx.astype(F32))
    for i in range(depth):
        kind, j = i % 3, i // 3
        mod = _modulation(cc, w_mod[i].astype(F32), b_mod[i].astype(F32))[:bsz + 1]
        shift, scale, gate = (mod[:, k * d:(k + 1) * d].reshape(bsz + 1, 1, d) for k in range(3))
        h = _norm_mod(stream.reshape(bsz * t_all, d), norm_g[i], scale, shift, tm, *tile_args)
        h = h.reshape(bsz, t_all, d)
        if kind == 0:
            stream = _rwkv_layer(stream, h, gate, n_ctx, tm, tile_args, rwkv_mu[j], rwkv_w_rkvg[j], rwkv_w0[j],
                                 rwkv_w1[j], rwkv_w2[j], rwkv_a0[j], rwkv_a1[j], rwkv_a2[j], rwkv_k_k[j],
                                 rwkv_k_a[j], rwkv_r_k[j], rwkv_ln_w[j], rwkv_ln_b[j], rwkv_w_out[j])
        elif kind == 1:
            stream = _na_layer(stream, h, gate, n_ctx, tm, tile_args, grid_w, na_w_in[j], na_q_g[j], na_k_g[j],
                               na_rpb[j], na_w_out[j])
        else:
            stream = _s5_layer(stream, h, gate, n_ctx, tm, tile_args, s5_w_in[j], s5_lam_re[j], s5_lam_im[j],
                               s5_log_dt[j], s5_b_re[j], s5_b_im[j], s5_c_re[j], s5_c_im[j], s5_d[j],
                               s5_w_glu[j], s5_b_glu[j], s5_w_out[j])
    return stream[:, n_ctx:].astype(x.dtype)
```

```python
import functools
import math

import numpy as np
import jax
import jax.numpy as jnp
from jax import lax
from jax.experimental import pallas as pl
from jax.experimental.pallas import tpu as pltpu

F32 = jnp.float32
BF16 = jnp.bfloat16
NORM_EPS = 1e-6
RWKV_GN_EPS = 64e-5
HEAD_DIM = 64
RWKV_CHUNK = 64
RWKV_HEADS_PER_STEP = 8
S5_CHUNK = 16
S5_GROUP = 16
MASK_NEG = -1e30
VMEM_LIMIT = 48 * 1024 * 1024
HIGHEST = lax.Precision.HIGHEST


def _cparams(*sem):
    return pltpu.CompilerParams(dimension_semantics=sem, vmem_limit_bytes=VMEM_LIMIT)


def _token_tile(n_ctx, n_all):
    for t in (256, 128, 64):
        if n_ctx % t == 0 and n_all % t == 0:
            return t
    raise ValueError("context / sequence lengths must be multiples of 64")


def _mod_row(i, tiles_per_b, ctx_tiles, n_batch):
    return jnp.where(i % tiles_per_b < ctx_tiles, n_batch, i // tiles_per_b)


def _mod_kernel(c_ref, w_ref, b_ref, o_ref):
    c = c_ref[...]
    s = c * jax.nn.sigmoid(c)
    o_ref[...] = jnp.dot(s, w_ref[...], precision=HIGHEST, preferred_element_type=F32) + b_ref[...]


def _modulation(cc, wm, bm):
    rows, d = cc.shape
    n = wm.shape[1]
    tn = 512
    return pl.pallas_call(
        _mod_kernel,
        grid=(n // tn,),
        in_specs=[pl.BlockSpec((rows, d), lambda j: (0, 0)),
                  pl.BlockSpec((d, tn), lambda j: (0, j)),
                  pl.BlockSpec((1, tn), lambda j: (0, j))],
        out_specs=pl.BlockSpec((rows, tn), lambda j: (0, j)),
        out_shape=jax.ShapeDtypeStruct((rows, n), F32),
        compiler_params=_cparams("arbitrary"),
        name="modulation",
    )(cc, wm, bm.reshape(1, n))


def _norm_kernel(x_ref, g_ref, sc_ref, sh_ref, o_ref):
    x = x_ref[...]
    ms = jnp.mean(x * x, axis=-1, keepdims=True)
    y = x * lax.rsqrt(ms + NORM_EPS) * g_ref[...]
    o_ref[...] = y * (1.0 + sc_ref[0]) + sh_ref[0]


def _norm_mod(x2, g, scale, shift, tm, tiles_per_b, ctx_tiles, n_batch):
    m, d = x2.shape
    row = lambda i: (_mod_row(i, tiles_per_b, ctx_tiles, n_batch), 0, 0)
    return pl.pallas_call(
        _norm_kernel,
        grid=(m // tm,),
        in_specs=[pl.BlockSpec((tm, d), lambda i: (i, 0)),
                  pl.BlockSpec((1, d), lambda i: (0, 0)),
                  pl.BlockSpec((1, 1, d), row),
                  pl.BlockSpec((1, 1, d), row)],
        out_specs=pl.BlockSpec((tm, d), lambda i: (i, 0)),
        out_shape=jax.ShapeDtypeStruct((m, d), F32),
        compiler_params=_cparams("arbitrary"),
        name="norm_mod",
    )(x2, g.reshape(1, d), scale, shift)


def _mm_kernel(x_ref, w_ref, o_ref):
    o_ref[...] = jnp.dot(x_ref[...].astype(BF16), w_ref[...], preferred_element_type=F32)


def _mm_bias_kernel(x_ref, w_ref, b_ref, o_ref):
    o_ref[...] = jnp.dot(x_ref[...].astype(BF16), w_ref[...], preferred_element_type=F32) + b_ref[...]


def _mm(x2, w, tm, bias=None):
    m, k = x2.shape
    n = w.shape[1]
    tn = n if n <= 2048 else 2048
    in_specs = [pl.BlockSpec((tm, k), lambda i, j: (i, 0)),
                pl.BlockSpec((k, tn), lambda i, j: (0, j))]
    args = [x2, w.astype(BF16)]
    kern = _mm_kernel
    if bias is not None:
        in_specs.append(pl.BlockSpec((1, tn), lambda i, j: (0, j)))
        args.append(bias.reshape(1, n).astype(F32))
        kern = _mm_bias_kernel
    return pl.pallas_call(
        kern,
        grid=(m // tm, n // tn),
        in_specs=in_specs,
        out_specs=pl.BlockSpec((tm, tn), lambda i, j: (i, j)),
        out_shape=jax.ShapeDtypeStruct((m, n), F32),
        compiler_params=_cparams("arbitrary", "arbitrary"),
        name="matmul",
    )(*args)


def _mm_res_kernel(x_ref, w_ref, res_ref, gate_ref, o_ref):
    acc = jnp.dot(x_ref[...].astype(BF16), w_ref[...], preferred_element_type=F32)
    o_ref[...] = res_ref[...] + gate_ref[0] * acc


def _mm_residual(x2, w, res2, gate, tm, tiles_per_b, ctx_tiles, n_batch):
    m, k = x2.shape
    n = w.shape[1]
    return pl.pallas_call(
        _mm_res_kernel,
        grid=(m // tm,),
        in_specs=[pl.BlockSpec((tm, k), lambda i: (i, 0)),
                  pl.BlockSpec((k, n), lambda i: (0, 0)),
                  pl.BlockSpec((tm, n), lambda i: (i, 0)),
                  pl.BlockSpec((1, 1, n), lambda i: (_mod_row(i, tiles_per_b, ctx_tiles, n_batch), 0, 0))],
        out_specs=pl.BlockSpec((tm, n), lambda i: (i, 0)),
        out_shape=jax.ShapeDtypeStruct((m, n), F32),
        compiler_params=_cparams("arbitrary"),
        name="matmul_residual",
    )(x2, w.astype(BF16), res2, gate)


def _gmm_kernel(x_ref, w_ref, o_ref):
    o_ref[0] = jnp.dot(x_ref[0].astype(BF16), w_ref[0], preferred_element_type=F32)


def _gmm_add_kernel(x_ref, w_ref, a_ref, o_ref):
    o_ref[0] = a_ref[0] + jnp.dot(x_ref[0].astype(BF16), w_ref[0], preferred_element_type=F32)


def _grouped_mm(x3, w3, add=None, add_col_block=0):
    g, m, k = x3.shape
    n = w3.shape[2]
    in_specs = [pl.BlockSpec((1, m, k), lambda i: (i, 0, 0)),
                pl.BlockSpec((1, k, n), lambda i: (i, 0, 0))]
    args = [x3, w3.astype(BF16)]
    kern = _gmm_kernel
    if add is not None:
        in_specs.append(pl.BlockSpec((1, m, n), lambda i: (i, 0, add_col_block)))
        args.append(add)
        kern = _gmm_add_kernel
    return pl.pallas_call(
        kern,
        grid=(g,),
        in_specs=in_specs,
        out_specs=pl.BlockSpec((1, m, n), lambda i: (i, 0, 0)),
        out_shape=jax.ShapeDtypeStruct((g, m, n), F32),
        compiler_params=_cparams("arbitrary"),
        name="grouped_matmul",
    )(*args)


def _dot_nt(a, b):
    return lax.dot_general(a, b, (((1,), (1,)), ((), ())), preferred_element_type=F32)


def _dot_tn(a, b):
    return lax.dot_general(a, b, (((0,), (0,)), ((), ())), preferred_element_type=F32)


def _dot(a, b):
    return jnp.dot(a, b, preferred_element_type=F32)


def _rwkv_chunk_heads(r, v, kk, lw, kd, bd, s0, incl, strict, incl_f, eye):
    hs = range(len(r))
    c = r[0].shape[0]
    lp = [jnp.dot(incl_f, lw[h], precision=HIGHEST, preferred_element_type=F32) for h in hs]
    lp_tot = [jnp.sum(lw[h], axis=0, keepdims=True) for h in hs]
    lhs = [jnp.concatenate([kk[h] * jnp.exp(lp[h] - lw[h]), r[h] * jnp.exp(lp[h])], axis=0).astype(BF16)
           for h in hs]
    e_ninc = [jnp.exp(-lp[h]) for h in hs]
    a_k = [_dot_nt(lhs[h], (kd[h] * e_ninc[h]).astype(BF16)) for h in hs]
    a_b = [_dot_nt(lhs[h], (bd[h] * e_ninc[h]).astype(BF16)) for h in hs]
    ls = [_dot_nt(lhs[h], s0[h].astype(BF16)) for h in hs]
    vb = [v[h].astype(BF16) for h in hs]
    n_mat = [jnp.where(strict, a_b[h][:c], 0.0) for h in hs]
    z = [ls[h][:c] + _dot(jnp.where(strict, a_k[h][:c], 0.0).astype(BF16), vb[h]) for h in hs]
    y0 = [ls[h][c:] + _dot(jnp.where(incl, a_k[h][c:], 0.0).astype(BF16), vb[h]) for h in hs]

    t_inv = [eye - n_mat[h] for h in hs]
    pw = [n_mat[h].astype(BF16) for h in hs]
    for _ in range(int(math.log2(c)) - 1):
        pw = [_dot(pw[h], pw[h]).astype(BF16) for h in hs]
        t_inv = [t_inv[h] + _dot(t_inv[h].astype(BF16), pw[h]) for h in hs]

    u = [_dot(t_inv[h].astype(BF16), z[h].astype(BF16)) for h in hs]
    y = [y0[h] - _dot(jnp.where(incl, a_b[h][c:], 0.0).astype(BF16), u[h].astype(BF16)) for h in hs]
    s1 = []
    for h in hs:
        e_rem = jnp.exp(lp_tot[h] - lp[h])
        upd = _dot_tn(jnp.concatenate([v[h], -u[h]], axis=0).astype(BF16),
                      jnp.concatenate([kd[h] * e_rem, bd[h] * e_rem], axis=0).astype(BF16))
        s1.append(s0[h] * jnp.exp(lp_tot[h]) + upd)
    return y, s1


def _rwkv_scan_kernel(r_ref, v_ref, kk_ref, lw_ref, kd_ref, bd_ref, y_ref, s_ref, *, nh, fwd_groups):
    g = pl.program_id(0)
    c_idx = pl.program_id(1)

    @pl.when(c_idx == 0)
    def _():
        s_ref[...] = jnp.zeros_like(s_ref)

    c = r_ref.shape[1]
    sign = jnp.where(g < fwd_groups, 1, -1)
    row = lax.broadcasted_iota(jnp.int32, (c, c), 0)
    col = lax.broadcasted_iota(jnp.int32, (c, c), 1)
    d = (row - col) * sign
    incl = d >= 0
    strict = d > 0
    incl_f = incl.astype(F32)
    eye = (row == col).astype(F32)
    load = lambda ref: [ref[i] for i in range(nh)]
    y, s1 = _rwkv_chunk_heads(load(r_ref), load(v_ref), load(kk_ref), load(lw_ref), load(kd_ref), load(bd_ref),
                              load(s_ref), incl, strict, incl_f, eye)
    for i in range(nh):
        y_ref[i] = y[i]
        s_ref[i] = s1[i]


def _rwkv_scan(r, v, kk, lw, kd, bd, n_ctx):
    bh, t, hd = r.shape
    nh = RWKV_HEADS_PER_STEP
    c = RWKV_CHUNK
    nc, nc_ctx = t // c, n_ctx // c
    fwd_groups = bh // nh

    def chunk(g, ci):
        rev = jnp.where(ci < nc_ctx, nc_ctx - 1 - ci, nc + nc_ctx - 1 - ci)
        return jnp.where(g < fwd_groups, ci, rev)

    shared = pl.BlockSpec((nh, c, hd), lambda g, ci: (g % fwd_groups, chunk(g, ci), 0))
    per_dir = pl.BlockSpec((nh, c, hd), lambda g, ci: (g, chunk(g, ci), 0))
    return pl.pallas_call(
        functools.partial(_rwkv_scan_kernel, nh=nh, fwd_groups=fwd_groups),
        grid=(2 * fwd_groups, nc),
        in_specs=[shared, shared, shared, per_dir, per_dir, per_dir],
        out_specs=per_dir,
        out_shape=jax.ShapeDtypeStruct((2 * bh, t, hd), F32),
        scratch_shapes=[pltpu.VMEM((nh, hd, hd), F32)],
        compiler_params=_cparams("arbitrary", "arbitrary"),
        name="rwkv_scan",
    )(r, v, kk, lw, kd, bd)


def _centred_shift(h):
    prev = jnp.pad(h[:, :-1], ((0, 0), (1, 0), (0, 0)))
    nxt = jnp.pad(h[:, 1:], ((0, 0), (0, 1), (0, 0)))
    return 0.5 * (prev + nxt)


def _rwkv_layer(stream, h, gate, n_ctx, tm, tile_args, mu, w_rkvg, w0, w1, w2, a0, a1, a2, k_k, k_a, r_k,
                ln_w, ln_b, w_out):
    bsz, t_all, d = stream.shape
    e = w_rkvg.shape[-1]
    nheads = e // HEAD_DIM
    m = bsz * t_all
    xx = jnp.concatenate([_centred_shift(h[:, :n_ctx]), _centred_shift(h[:, n_ctx:])], axis=1) - h
    xs = [(h + xx * mu[i]).reshape(m, d) for i in range(mu.shape[0])]
    r, k, v, g = (_mm(xs[i], w_rkvg[i], tm) for i in range(4))
    lr = w1.shape[-1]
    zeros = jnp.zeros((lr, e), F32)

    def lora(x2, wa, wb, act):
        mid = act(_mm(x2, jnp.concatenate([wa[0], wa[1]], axis=1), tm))
        wbd = jnp.concatenate([jnp.concatenate([wb[0], zeros], axis=1),
                               jnp.concatenate([zeros, wb[1]], axis=1)], axis=0)
        return _mm(mid, wbd, tm).reshape(m, 2, e)

    dec_lora = lora(xs[4], w1, w2, jnp.tanh)
    w_log = -jax.nn.softplus(-(w0[None] + dec_lora)) - 0.5
    lw = -jnp.exp(w_log)
    iclr = jax.nn.sigmoid(a0[None] + lora(xs[5], a1, a2, lambda x: x))
    kkf = (k * k_k).reshape(m, nheads, HEAD_DIM)
    kk = (kkf / jnp.maximum(jnp.sqrt(jnp.sum(kkf * kkf, axis=-1, keepdims=True)), 1e-12)).reshape(m, e)
    kd = k[:, None, :] * (1.0 + (iclr - 1.0) * k_a)
    bd = kk[:, None, :] * iclr

    def heads(a):
        return a.reshape(bsz, t_all, nheads, HEAD_DIM).transpose(0, 2, 1, 3).reshape(bsz * nheads, t_all, HEAD_DIM)

    def heads2(a):
        return a.reshape(bsz, t_all, 2, nheads, HEAD_DIM).transpose(2, 0, 3, 1, 4).reshape(
            2 * bsz * nheads, t_all, HEAD_DIM)

    y2 = _rwkv_scan(heads(r), heads(v), heads(kk), heads2(lw), heads2(kd), heads2(bd), n_ctx)
    y2 = y2.reshape(2, bsz, nheads, t_all, HEAD_DIM)
    y = (y2[0] + y2[1]).transpose(0, 2, 1, 3)
    mean = jnp.mean(y, axis=-1, keepdims=True)
    var = jnp.mean(jnp.square(y - mean), axis=-1, keepdims=True)
    yn = ((y - mean) * lax.rsqrt(var + RWKV_GN_EPS) * ln_w.reshape(nheads, HEAD_DIM)
          + ln_b.reshape(nheads, HEAD_DIM))
    hd4 = lambda a: a.reshape(bsz, t_all, nheads, HEAD_DIM)
    bonus = jnp.sum(hd4(r) * hd4(k) * r_k, axis=-1, keepdims=True) * hd4(v)
    o = (yn + bonus).reshape(m, e) * jax.nn.silu(g)
    return _mm_residual(o, w_out, stream.reshape(m, d), gate, tm, *tile_args).reshape(bsz, t_all, d)


def _na_kernel(q_ref, k_ref, v_ref, bias_ref, qg_ref, kg_ref, o_ref, kn_ref, vb_ref, *,
               n_ctx, grid_w, kh, khm, rows, scale):
    qi = pl.program_id(2)
    ctx_tiles = n_ctx // grid_w
    hd = HEAD_DIM
    lanes = 2 * hd
    same_head = (lax.broadcasted_iota(jnp.int32, (lanes, lanes), 0) // hd
                 == lax.broadcasted_iota(jnp.int32, (lanes, lanes), 1) // hd).astype(BF16)
    lane_head = lax.broadcasted_iota(jnp.int32, (1, lanes), 1) // hd

    def head_rms(x, g):
        ss = jnp.dot((x * x).astype(BF16), same_head, preferred_element_type=F32)
        return x * lax.rsqrt(ss * (1.0 / hd) + NORM_EPS) * g

    @pl.when(qi == 0)
    def _():
        kn_ref[...] = head_rms(k_ref[0], kg_ref[...]).astype(BF16)
        vb_ref[...] = v_ref[0].astype(BF16)

    qn = head_rms(q_ref[0], qg_ref[...])
    k_ctx = kn_ref[0:n_ctx, :]
    v_ctx = vb_ref[0:n_ctx, :]

    def attend(k_nb, v_nb, bias_of):
        out = jnp.zeros((grid_w, lanes), F32)
        for hh in range(2):
            sel = lane_head == hh
            qh = jnp.where(sel, qn, 0.0).astype(BF16)
            s_c = _dot_nt(qh, k_ctx) * scale
            m = jnp.max(s_c, axis=-1, keepdims=True)
            if k_nb is not None:
                s_n = _dot_nt(qh, k_nb) * scale + bias_of(hh)
                m = jnp.maximum(m, jnp.max(s_n, axis=-1, keepdims=True))
                p_n = jnp.exp(s_n - m)
            p_c = jnp.exp(s_c - m)
            den = jnp.sum(p_c, axis=-1, keepdims=True)
            acc = _dot(p_c.astype(BF16), v_ctx)
            if k_nb is not None:
                den = den + jnp.sum(p_n, axis=-1, keepdims=True)
                acc = acc + _dot(p_n.astype(BF16), v_nb)
            out = jnp.where(sel, acc / den, out)
        return out

    @pl.when(qi < ctx_tiles)
    def _():
        o_ref[0] = attend(None, None, None)

    @pl.when(qi >= ctx_tiles)
    def _():
        i = qi - ctx_tiles
        r0 = jnp.clip(i - kh // 2, 0, rows - kh)
        start = pl.multiple_of(n_ctx + r0 * grid_w, grid_w)
        k_nb = kn_ref[pl.ds(start, kh * grid_w), :]
        v_nb = vb_ref[pl.ds(start, kh * grid_w), :]
        base = khm - 1 - (i - r0)
        o_ref[0] = attend(k_nb, v_nb, lambda hh: jnp.concatenate(
            [bias_ref[hh, base + 2 * q] for q in range(kh // 2)], axis=1))


def _na_bias_table(rpb, grid_w):
    kw = (rpb.shape[2] + 1) // 2
    j = np.arange(grid_w)[:, None]
    c = np.arange(grid_w)[None, :]
    c0 = np.clip(j - kw // 2, 0, grid_w - kw)
    valid = (c >= c0) & (c < c0 + kw)
    onehot = ((c - j + kw - 1)[None] == np.arange(2 * kw - 1)[:, None, None]) & valid[None]
    tiles = jnp.einsum("hab,bjc->hajc", rpb.astype(F32), jnp.asarray(onehot, F32), precision=HIGHEST)
    tiles = tiles + jnp.asarray(np.where(valid, 0.0, MASK_NEG), F32)
    return jnp.concatenate([tiles[:, :-1], tiles[:, 1:]], axis=-1)


def _na_attention(qkvz, rpb, q_g, k_g, n_ctx, grid_w, kh_max):
    bsz, t_all, e4 = qkvz.shape
    e = e4 // 4
    pairs = e // (2 * HEAD_DIM)
    rows = (t_all - n_ctx) // grid_w
    kh = min(kh_max, rows)
    assert kh % 2 == 0 and kh <= kh_max
    tbl = _na_bias_table(rpb, grid_w)
    nq = t_all // grid_w
    lanes = 2 * HEAD_DIM
    g2 = lambda g: jnp.concatenate([g, g]).reshape(1, lanes).astype(F32)
    return pl.pallas_call(
        functools.partial(_na_kernel, n_ctx=n_ctx, grid_w=grid_w, kh=kh, khm=kh_max, rows=rows,
                          scale=HEAD_DIM ** -0.5),
        grid=(bsz, pairs, nq),
        in_specs=[pl.BlockSpec((1, grid_w, lanes), lambda b, p, qi: (b, qi, p)),
                  pl.BlockSpec((1, t_all, lanes), lambda b, p, qi: (b, 0, pairs + p)),
                  pl.BlockSpec((1, t_all, lanes), lambda b, p, qi: (b, 0, 2 * pairs + p)),
                  pl.BlockSpec((2,) + tbl.shape[1:], lambda b, p, qi: (p, 0, 0, 0)),
                  pl.BlockSpec((1, lanes), lambda b, p, qi: (0, 0)),
                  pl.BlockSpec((1, lanes), lambda b, p, qi: (0, 0))],
        out_specs=pl.BlockSpec((1, grid_w, lanes), lambda b, p, qi: (b, qi, p)),
        out_shape=jax.ShapeDtypeStruct((bsz, t_all, e), F32),
        scratch_shapes=[pltpu.VMEM((t_all, lanes), BF16), pltpu.VMEM((t_all, lanes), BF16)],
        compiler_params=_cparams("arbitrary", "arbitrary", "arbitrary"),
        name="na_attention",
    )(qkvz, qkvz, qkvz, tbl, g2(q_g), g2(k_g))


def _na_layer(stream, h, gate, n_ctx, tm, tile_args, grid_w, w_in, q_g, k_g, rpb, w_out):
    bsz, t_all, d = stream.shape
    m = bsz * t_all
    e = w_in.shape[1] // 4
    qkvz = _mm(h.reshape(m, d), w_in, tm).reshape(bsz, t_all, 4 * e)
    kh_max = (rpb.shape[1] + 1) // 2
    o = _na_attention(qkvz, rpb, q_g, k_g, n_ctx, grid_w, kh_max)
    o = (o * jax.nn.silu(qkvz[..., 3 * e:])).reshape(m, e)
    return _mm_residual(o, w_out, stream.reshape(m, d), gate, tm, *tile_args).reshape(bsz, t_all, d)


def _s5_matrices(lam_re, lam_im, log_dt, b_re, b_im, c_re, c_im):
    nt = S5_CHUNK
    g, p, cg = b_re.shape
    tau = jnp.arange(nt + 1, dtype=F32)[:, None, None]
    i_idx = np.arange(nt)
    kbig, bcat, ccat, a_pow = 0.0, [], [], []
    for s in range(2):
        lr, li = lam_re[s].astype(F32), lam_im[s].astype(F32)
        step = jnp.exp(log_dt[s].astype(F32))[:, None]
        mag = jnp.exp(lr * step)
        ar, ai = mag * jnp.cos(li * step), mag * jnp.sin(li * step)
        den = lr * lr + li * li
        qr = ((ar - 1.0) * lr + ai * li) / den
        qi = (ai * lr - (ar - 1.0) * li) / den
        bbr = qr[..., None] * b_re - qi[..., None] * b_im
        bbi = qr[..., None] * b_im + qi[..., None] * b_re
        pmag = jnp.exp(lr * step * tau)
        pr, pi = pmag * jnp.cos(li * step * tau), pmag * jnp.sin(li * step * tau)
        clr = c_re[None] * pr[:, :, None, :] - c_im[None] * pi[:, :, None, :]
        cli = c_re[None] * pi[:, :, None, :] + c_im[None] * pr[:, :, None, :]
        ker = (jnp.einsum("tgop,gpc->tgoc", clr[:nt], bbr, precision=HIGHEST)
               - jnp.einsum("tgop,gpc->tgoc", cli[:nt], bbi, precision=HIGHEST))
        lbr = pr[:, :, :, None] * bbr[None] - pi[:, :, :, None] * bbi[None]
        lbi = pr[:, :, :, None] * bbi[None] + pi[:, :, :, None] * bbr[None]
        lag = (i_idx[None, :] - i_idx[:, None]) if s == 0 else (i_idx[:, None] - i_idx[None, :])
        kb = jnp.where((lag >= 0)[:, :, None, None, None], ker[np.clip(lag, 0, nt - 1)], 0.0)
        kbig = kbig + kb.transpose(2, 0, 4, 1, 3).reshape(g, nt * cg, nt * cg)
        inj = (nt - 1 - i_idx) if s == 0 else i_idx
        bb = jnp.concatenate([lbr[inj], lbi[inj]], axis=2)
        bcat.append(bb.transpose(1, 0, 3, 2).reshape(g, nt * cg, 2 * p))
        out = (i_idx + 1) if s == 0 else (nt - i_idx)
        cc = jnp.concatenate([clr[out], -cli[out]], axis=3)
        ccat.append(cc.transpose(1, 3, 0, 2).reshape(g, 2 * p, nt * cg))
        a_pow.append((pr[nt], pi[nt]))
    w_in = jnp.concatenate([kbig] + bcat, axis=2)
    w_out = jnp.concatenate(ccat, axis=1)
    return w_in, w_out, a_pow


def _s5_state_kernel(bur_ref, bui_ref, ar_ref, ai_ref, xr_ref, xi_ref):
    nck = bur_ref.shape[0]
    ar = ar_ref[...]
    ai = ai_ref[...]

    def body(c, carry):
        xr, xi = carry
        xr_ref[c] = xr
        xi_ref[c] = xi
        return (ar * xr - ai * xi + bur_ref[c], ar * xi + ai * xr + bui_ref[c])

    zero = jnp.zeros(ar.shape, F32)
    lax.fori_loop(0, nck, body, (zero, zero))


def _s5_states(bur, bui, ar, ai):
    nck, rows, lanes = bur.shape
    tl = 512
    blk = pl.BlockSpec((nck, rows, tl), lambda j: (0, 0, j))
    coef = pl.BlockSpec((rows, tl), lambda j: (0, j))
    return pl.pallas_call(
        _s5_state_kernel,
        grid=(lanes // tl,),
        in_specs=[blk, blk, coef, coef],
        out_specs=[blk, blk],
        out_shape=[jax.ShapeDtypeStruct(bur.shape, F32)] * 2,
        compiler_params=_cparams("arbitrary"),
        name="s5_states",
    )(bur, bui, ar, ai)


def _s5_layer(stream, h, gate, n_ctx, tm, tile_args, w_in, lam_re, lam_im, log_dt, b_re, b_im, c_re, c_im,
              d_skip, w_glu, b_glu, w_out):
    bsz, t_all, d = stream.shape
    m = bsz * t_all
    e = w_in.shape[1] // 2
    nt, cg = S5_CHUNK, S5_GROUP
    g = e // cg
    p = b_re.shape[1]
    nck, nck_ctx = t_all // nt, n_ctx // nt
    uz = _mm(h.reshape(m, d), w_in, tm)
    u, z = uz[:, :e], uz[:, e:]
    k_in, k_out, a_pow = _s5_matrices(lam_re, lam_im, log_dt, b_re, b_im, c_re, c_im)
    u_flat = u.reshape(bsz, nck, nt, g, cg).transpose(3, 0, 1, 2, 4).reshape(g, bsz * nck, nt * cg)
    yb = _grouped_mm(u_flat, k_in)
    bu = yb[:, :, nt * cg:].reshape(g, bsz, nck, 2, 2, p)
    rev = np.concatenate([np.arange(nck_ctx)[::-1], np.arange(nck_ctx, nck)[::-1]])
    bu = bu.transpose(4, 2, 3, 1, 0, 5)
    bu = jnp.stack([bu[:, :, 0], bu[:, rev, 1]], axis=2).reshape(2, nck, 2 * bsz, g * p)
    coef = lambda a: jnp.broadcast_to(jnp.stack(a)[:, None], (2, bsz, g, p)).reshape(2 * bsz, g * p)
    xr, xi = _s5_states(bu[0], bu[1], coef([a_pow[0][0], a_pow[1][0]]), coef([a_pow[0][1], a_pow[1][1]]))
    inv = np.argsort(rev)

    def unorder(x):
        x = x.reshape(nck, 2, bsz, g, p)
        x = jnp.stack([x[:, 0], x[inv, 1]], axis=1)
        return x.transpose(3, 2, 0, 1, 4).reshape(g, bsz * nck, 2, p)

    xr, xi = unorder(xr), unorder(xi)
    x_cat = jnp.concatenate([xr[:, :, 0], xi[:, :, 0], xr[:, :, 1], xi[:, :, 1]], axis=-1)
    y_flat = _grouped_mm(x_cat, k_out, add=yb, add_col_block=0)
    y = y_flat.reshape(g, bsz, nck, nt, cg).transpose(1, 2, 3, 0, 4).reshape(m, e)
    y = jax.nn.gelu(y + d_skip * u)
    y = y * jax.nn.sigmoid(_mm(y, w_glu, tm, bias=b_glu))
    o = y * jax.nn.silu(z)
    return _mm_residual(o, w_out, stream.reshape(m, d), gate, tm, *tile_args).reshape(bsz, t_all, d)


def kernel(x, c, ctx, c_ctx, norm_g, w_mod, b_mod, rwkv_mu, rwkv_w_rkvg, rwkv_w0, rwkv_w1, rwkv_w2, rwkv_a0, rwkv_a1, rwkv_a2, rwkv_k_k, rwkv_k_a, rwkv_r_k, rwkv_ln_w, rwkv_ln_b, rwkv_w_out, na_w_in, na_q_g, na_k_g, na_rpb, na_w_out, s5_w_in, s5_lam_re, s5_lam_im, s5_log_dt, s5_b_re, s5_b_im, s5_c_re, s5_c_im, s5_d, s5_w_glu, s5_b_glu, s5_w_out):
    bsz, n_lat, d = x.shape
    n_ctx = ctx.shape[1]
    t_all = n_ctx + n_lat
    depth = norm_g.shape[0]
    grid_w = 64
    tm = _token_tile(n_ctx, t_all)
    tile_args = (t_all // tm, n_ctx // tm, bsz)
    stream = jnp.concatenate([ctx, x], axis=1).astype(F32)
    rows = 8 * ((bsz + 1 + 7) // 8)
    cc = jnp.zeros((rows, d), F32).at[:bsz].set(c.astype(F32)).at[bsz].set(c_ctx.astype(F32))
    for i in range(depth):
        kind, j = i % 3, i // 3
        mod = _modulation(cc, w_mod[i].astype(F32), b_mod[i].astype(F32))[:bsz + 1]
        shift, scale, gate = (mod[:, k * d:(k + 1) * d].reshape(bsz + 1, 1, d) for k in range(3))
        h = _norm_mod(stream.reshape(bsz * t_all, d), norm_g[i], scale, shift, tm, *tile_args)
        h = h.reshape(bsz, t_all, d)
        if kind == 0:
            stream = _rwkv_layer(stream, h, gate, n_ctx, tm, tile_args, rwkv_mu[j], rwkv_w_rkvg[j], rwkv_w0[j],
                                 rwkv_w1[j], rwkv_w2[j], rwkv_a0[j], rwkv_a1[j], rwkv_a2[j], rwkv_k_k[j],
                                 rwkv_k_a[j], rwkv_r_k[j], rwkv_ln_w[j], rwkv_ln_b[j], rwkv_w_out[j])
        elif kind == 1:
            stream = _na_layer(stream, h, gate, n_ctx, tm, tile_args, grid_w, na_w_in[j], na_q_g[j], na_k_g[j],
                               na_rpb[j], na_w_out[j])
        else:
            stream = _s5_layer(stream, h, gate, n_ctx, tm, tile_args, s5_w_in[j], s5_lam_re[j], s5_lam_im[j],
                               s5_log_dt[j], s5_b_re[j], s5_b_im[j], s5_c_re[j], s5_c_im[j], s5_d[j],
                               s5_w_glu[j], s5_b_glu[j], s5_w_out[j])
    return stream[:, n_ctx:].astype(x.dtype)
```

```python
import functools
import math

import numpy as np
import jax
import jax.numpy as jnp
from jax import lax
from jax.experimental import pallas as pl
from jax.experimental.pallas import tpu as pltpu

F32 = jnp.float32
BF16 = jnp.bfloat16
NORM_EPS = 1e-6
RWKV_GN_EPS = 64e-5
HEAD_DIM = 64
RWKV_CHUNK = 64
RWKV_HEADS_PER_STEP = 8
S5_CHUNK = 16
S5_GROUP = 16
MASK_NEG = -1e30
VMEM_LIMIT = 48 * 1024 * 1024
HIGHEST = lax.Precision.HIGHEST


def _cparams(*sem):
    return pltpu.CompilerParams(dimension_semantics=sem, vmem_limit_bytes=VMEM_LIMIT)


def _token_tile(n_ctx, n_all, largest=256):
    for t in (256, 128, 64):
        if t <= largest and n_ctx % t == 0 and n_all % t == 0:
            return t
    raise ValueError("context / sequence lengths must be multiples of 64")


def _mod_row(i, tiles_per_b, ctx_tiles, n_batch):
    return jnp.where(i % tiles_per_b < ctx_tiles, n_batch, i // tiles_per_b)


def _mod_kernel(c_ref, w_ref, b_ref, o_ref):
    c = c_ref[...]
    s = c * jax.nn.sigmoid(c)
    o_ref[...] = jnp.dot(s, w_ref[...], precision=HIGHEST, preferred_element_type=F32) + b_ref[...]


def _modulation(cc, wm, bm):
    rows, d = cc.shape
    n = wm.shape[1]
    tn = 512
    return pl.pallas_call(
        _mod_kernel,
        grid=(n // tn,),
        in_specs=[pl.BlockSpec((rows, d), lambda j: (0, 0)),
                  pl.BlockSpec((d, tn), lambda j: (0, j)),
                  pl.BlockSpec((1, tn), lambda j: (0, j))],
        out_specs=pl.BlockSpec((rows, tn), lambda j: (0, j)),
        out_shape=jax.ShapeDtypeStruct((rows, n), F32),
        compiler_params=_cparams("arbitrary"),
        name="modulation",
    )(cc, wm, bm.reshape(1, n))


def _norm_kernel(x_ref, g_ref, sc_ref, sh_ref, o_ref):
    x = x_ref[...]
    ms = jnp.mean(x * x, axis=-1, keepdims=True)
    y = x * lax.rsqrt(ms + NORM_EPS) * g_ref[...]
    o_ref[...] = y * (1.0 + sc_ref[0]) + sh_ref[0]


def _norm_mod(x2, g, scale, shift, tm, tiles_per_b, ctx_tiles, n_batch):
    m, d = x2.shape
    row = lambda i: (_mod_row(i, tiles_per_b, ctx_tiles, n_batch), 0, 0)
    return pl.pallas_call(
        _norm_kernel,
        grid=(m // tm,),
        in_specs=[pl.BlockSpec((tm, d), lambda i: (i, 0)),
                  pl.BlockSpec((1, d), lambda i: (0, 0)),
                  pl.BlockSpec((1, 1, d), row),
                  pl.BlockSpec((1, 1, d), row)],
        out_specs=pl.BlockSpec((tm, d), lambda i: (i, 0)),
        out_shape=jax.ShapeDtypeStruct((m, d), F32),
        compiler_params=_cparams("arbitrary"),
        name="norm_mod",
    )(x2, g.reshape(1, d), scale, shift)


def _mm_kernel(x_ref, w_ref, o_ref):
    o_ref[...] = jnp.dot(x_ref[...].astype(BF16), w_ref[...], preferred_element_type=F32)


def _mm_bias_kernel(x_ref, w_ref, b_ref, o_ref):
    o_ref[...] = jnp.dot(x_ref[...].astype(BF16), w_ref[...], preferred_element_type=F32) + b_ref[...]


def _mm(x2, w, tm, bias=None):
    m, k = x2.shape
    n = w.shape[1]
    tn = n if n <= 2048 else 2048
    in_specs = [pl.BlockSpec((tm, k), lambda i, j: (i, 0)),
                pl.BlockSpec((k, tn), lambda i, j: (0, j))]
    args = [x2, w.astype(BF16)]
    kern = _mm_kernel
    if bias is not None:
        in_specs.append(pl.BlockSpec((1, tn), lambda i, j: (0, j)))
        args.append(bias.reshape(1, n).astype(F32))
        kern = _mm_bias_kernel
    return pl.pallas_call(
        kern,
        grid=(m // tm, n // tn),
        in_specs=in_specs,
        out_specs=pl.BlockSpec((tm, tn), lambda i, j: (i, j)),
        out_shape=jax.ShapeDtypeStruct((m, n), F32),
        compiler_params=_cparams("arbitrary", "arbitrary"),
        name="matmul",
    )(*args)


def _mm_res_kernel(x_ref, w_ref, res_ref, gate_ref, o_ref):
    acc = jnp.dot(x_ref[...].astype(BF16), w_ref[...], preferred_element_type=F32)
    o_ref[...] = res_ref[...] + gate_ref[0] * acc


def _mm_residual(x2, w, res2, gate, tm, tiles_per_b, ctx_tiles, n_batch):
    m, k = x2.shape
    n = w.shape[1]
    return pl.pallas_call(
        _mm_res_kernel,
        grid=(m // tm,),
        in_specs=[pl.BlockSpec((tm, k), lambda i: (i, 0)),
                  pl.BlockSpec((k, n), lambda i: (0, 0)),
                  pl.BlockSpec((tm, n), lambda i: (i, 0)),
                  pl.BlockSpec((1, 1, n), lambda i: (_mod_row(i, tiles_per_b, ctx_tiles, n_batch), 0, 0))],
        out_specs=pl.BlockSpec((tm, n), lambda i: (i, 0)),
        out_shape=jax.ShapeDtypeStruct((m, n), F32),
        compiler_params=_cparams("arbitrary"),
        name="matmul_residual",
    )(x2, w.astype(BF16), res2, gate)


def _gmm_kernel(x_ref, w_ref, o_ref):
    o_ref[0] = jnp.dot(x_ref[0].astype(BF16), w_ref[0], preferred_element_type=F32)


def _gmm_add_kernel(x_ref, w_ref, a_ref, o_ref):
    o_ref[0] = a_ref[0] + jnp.dot(x_ref[0].astype(BF16), w_ref[0], preferred_element_type=F32)


def _grouped_mm(x3, w3, add=None, add_col_block=0):
    g, m, k = x3.shape
    n = w3.shape[2]
    in_specs = [pl.BlockSpec((1, m, k), lambda i: (i, 0, 0)),
                pl.BlockSpec((1, k, n), lambda i: (i, 0, 0))]
    args = [x3, w3.astype(BF16)]
    kern = _gmm_kernel
    if add is not None:
        in_specs.append(pl.BlockSpec((1, m, n), lambda i: (i, 0, add_col_block)))
        args.append(add)
        kern = _gmm_add_kernel
    return pl.pallas_call(
        kern,
        grid=(g,),
        in_specs=in_specs,
        out_specs=pl.BlockSpec((1, m, n), lambda i: (i, 0, 0)),
        out_shape=jax.ShapeDtypeStruct((g, m, n), F32),
        compiler_params=_cparams("arbitrary"),
        name="grouped_matmul",
    )(*args)


def _dot_nt(a, b):
    return lax.dot_general(a, b, (((1,), (1,)), ((), ())), preferred_element_type=F32)


def _dot_tn(a, b):
    return lax.dot_general(a, b, (((0,), (0,)), ((), ())), preferred_element_type=F32)


def _dot(a, b):
    return jnp.dot(a, b, preferred_element_type=F32)


def _head_sum(x, ones_bd):
    lanes = ones_bd.shape[0]
    return jnp.concatenate(
        [jnp.dot(x[:, p * lanes:(p + 1) * lanes].astype(BF16), ones_bd, preferred_element_type=F32)
         for p in range(x.shape[1] // lanes)], axis=1)


def _same_head(lanes):
    return (lax.broadcasted_iota(jnp.int32, (lanes, lanes), 0) // HEAD_DIM
            == lax.broadcasted_iota(jnp.int32, (lanes, lanes), 1) // HEAD_DIM)


def _rwkv_proj_kernel(x_ref, xp_ref, xn_ref, g_ref, sc_ref, sh_ref, mu_ref, w_ref, w1_ref, w2_ref, a1_ref, a2_ref,
                      w0_ref, a0_ref, kk_ref, ka_ref, rk_ref,
                      r_out, v_out, kk_out, g_out, bonus_out, lw_out, kd_out, bd_out, *, tm, tiles_per_b, ctx_tiles):
    i = pl.program_id(0)
    j = i % tiles_per_b
    gain = g_ref[...]
    sc = 1.0 + sc_ref[0]
    sh = sh_ref[0]

    def norm(x):
        ms = jnp.mean(x * x, axis=-1, keepdims=True)
        return x * lax.rsqrt(ms + NORM_EPS) * gain * sc + sh

    h = norm(x_ref[...])
    has_prev = jnp.logical_and(j != 0, j != ctx_tiles)
    has_next = jnp.logical_and(j != ctx_tiles - 1, j != tiles_per_b - 1)
    h_prev = jnp.where(has_prev, norm(xp_ref[...])[7:8], 0.0)
    h_next = jnp.where(has_next, norm(xn_ref[...])[0:1], 0.0)
    row = lax.broadcasted_iota(jnp.int32, h.shape, 0)
    prev = jnp.where(row == 0, h_prev, pltpu.roll(h, 1, axis=0))
    nxt = jnp.where(row == tm - 1, h_next, pltpu.roll(h, tm - 1, axis=0))
    xx = 0.5 * (prev + nxt) - h
    mix = lambda n: (h + xx * mu_ref[n:n + 1, :]).astype(BF16)

    r = jnp.dot(mix(0), w_ref[0], preferred_element_type=F32)
    k = jnp.dot(mix(1), w_ref[1], preferred_element_type=F32)
    v = jnp.dot(mix(2), w_ref[2], preferred_element_type=F32)
    g_out[...] = jnp.dot(mix(3), w_ref[3], preferred_element_type=F32)
    dec = jnp.dot(jnp.tanh(jnp.dot(mix(4), w1_ref[...], preferred_element_type=F32)).astype(BF16), w2_ref[...],
                  preferred_element_type=F32)
    icl = jnp.dot(jnp.dot(mix(5), a1_ref[...], preferred_element_type=F32).astype(BF16), a2_ref[...],
                  preferred_element_type=F32)
    e = r.shape[1]
    ones_bd = _same_head(2 * HEAD_DIM).astype(BF16)
    kkf = k * kk_ref[...]
    kk = kkf / jnp.maximum(jnp.sqrt(_head_sum(kkf * kkf, ones_bd)), 1e-12)
    r_out[...] = r
    v_out[...] = v
    kk_out[...] = kk
    bonus_out[...] = _head_sum(r * k * rk_ref[...], ones_bd) * v
    for s in range(2):
        lw_out[s] = -math.exp(-0.5) * jax.nn.sigmoid(w0_ref[s:s + 1, :] + dec[:, s * e:(s + 1) * e])
        a = jax.nn.sigmoid(a0_ref[s:s + 1, :] + icl[:, s * e:(s + 1) * e])
        kd_out[s] = k * (1.0 + (a - 1.0) * ka_ref[...])
        bd_out[s] = kk * a


def _rwkv_proj(stream2, norm_g, scale, shift, mu, w_rkvg, w0, w1, w2, a0, a1, a2, k_k, k_a, r_k, tm, tiles_per_b,
               ctx_tiles, n_batch):
    m, d = stream2.shape
    e = w_rkvg.shape[-1]
    lr = w1.shape[-1]
    nblk = m // 8
    zeros = jnp.zeros((lr, e), F32)
    cat = lambda w: jnp.concatenate([w[0], w[1]], axis=1).astype(BF16)
    bdiag = lambda w: jnp.concatenate([jnp.concatenate([w[0], zeros], axis=1),
                                       jnp.concatenate([zeros, w[1]], axis=1)], axis=0).astype(BF16)
    row = lambda i: (_mod_row(i, tiles_per_b, ctx_tiles, n_batch), 0, 0)
    full = lambda shape: pl.BlockSpec(shape, lambda i: (0,) * len(shape))
    tok = pl.BlockSpec((tm, e), lambda i: (i, 0))
    tok2 = pl.BlockSpec((2, tm, e), lambda i: (0, i, 0))
    vec = lambda a: a.reshape(1, e).astype(F32)
    return pl.pallas_call(
        functools.partial(_rwkv_proj_kernel, tm=tm, tiles_per_b=tiles_per_b, ctx_tiles=ctx_tiles),
        grid=(m // tm,),
        in_specs=[pl.BlockSpec((tm, d), lambda i: (i, 0)),
                  pl.BlockSpec((8, d), lambda i: (jnp.maximum(i * (tm // 8) - 1, 0), 0)),
                  pl.BlockSpec((8, d), lambda i: (jnp.minimum((i + 1) * (tm // 8), nblk - 1), 0)),
                  full((1, d)), pl.BlockSpec((1, 1, d), row), pl.BlockSpec((1, 1, d), row),
                  full(mu.shape), full(w_rkvg.shape), full((d, 2 * lr)), full((2 * lr, 2 * e)),
                  full((d, 2 * lr)), full((2 * lr, 2 * e)), full((2, e)), full((2, e)),
                  full((1, e)), full((1, e)), full((1, e))],
        out_specs=[tok, tok, tok, tok, tok, tok2, tok2, tok2],
        out_shape=[jax.ShapeDtypeStruct((m, e), F32)] * 5 + [jax.ShapeDtypeStruct((2, m, e), F32)] * 3,
        compiler_params=_cparams("arbitrary"),
        name="rwkv_proj",
    )(stream2, stream2, stream2, norm_g.reshape(1, d), scale, shift, mu.astype(F32), w_rkvg.astype(BF16),
      cat(w1), bdiag(w2), cat(a1), bdiag(a2), w0.astype(F32), a0.astype(F32), vec(k_k), vec(k_a), vec(r_k))


def _rwkv_chunk_pairs(r, v, kk, lw, kd, bd, s_bd, sign):
    c = r.shape[0]
    hd = HEAD_DIM
    lanes = 2 * hd
    pairs = range(len(s_bd))
    heads = [(p, hh) for p in pairs for hh in range(2)]
    row = lax.broadcasted_iota(jnp.int32, (c, c), 0)
    col = lax.broadcasted_iota(jnp.int32, (c, c), 1)
    d = (row - col) * sign
    incl = d >= 0
    strict = d > 0
    eye = (row == col).astype(F32)
    lane_head = lax.broadcasted_iota(jnp.int32, (1, lanes), 1) // hd
    first = lane_head == 0
    same = _same_head(lanes)
    pick = lambda a, p: a[:, p * lanes:(p + 1) * lanes]
    both = lambda mats, x: jnp.where(first, _dot(mats[0].astype(BF16), x), _dot(mats[1].astype(BF16), x))

    lp = jnp.dot(incl.astype(F32), lw, precision=HIGHEST, preferred_element_type=F32)
    lp_tot = jnp.sum(lw, axis=0, keepdims=True)
    lhs = jnp.concatenate([kk * jnp.exp(lp - lw), r * jnp.exp(lp)], axis=0).astype(BF16)
    e_ninc = jnp.exp(-lp)
    k_h = (kd * e_ninc).astype(BF16)
    b_h = (bd * e_ninc).astype(BF16)
    e_rem = jnp.exp(lp_tot - lp)
    k_p = (kd * e_rem).astype(BF16)
    b_p = (bd * e_rem).astype(BF16)
    vb = v.astype(BF16)

    zero = jnp.zeros((), BF16)
    a_k = {(p, hh): _dot_nt(pick(lhs, p), jnp.where(lane_head == hh, pick(k_h, p), zero)) for p, hh in heads}
    a_b = {(p, hh): _dot_nt(pick(lhs, p), jnp.where(lane_head == hh, pick(b_h, p), zero)) for p, hh in heads}
    ls = [_dot_nt(pick(lhs, p), s_bd[p].astype(BF16)) for p in pairs]
    n_mat = {ph: jnp.where(strict, a_b[ph][:c], 0.0) for ph in heads}
    z = [ls[p][:c] + both([jnp.where(strict, a_k[p, hh][:c], 0.0) for hh in range(2)], pick(vb, p)) for p in pairs]
    y0 = [ls[p][c:] + both([jnp.where(incl, a_k[p, hh][c:], 0.0) for hh in range(2)], pick(vb, p)) for p in pairs]

    t_inv = {ph: eye - n_mat[ph] for ph in heads}
    pw = {ph: n_mat[ph].astype(BF16) for ph in heads}
    for _ in range(int(math.log2(c)) - 1):
        pw = {ph: _dot(pw[ph], pw[ph]).astype(BF16) for ph in heads}
        t_inv = {ph: t_inv[ph] + _dot(t_inv[ph].astype(BF16), pw[ph]) for ph in heads}

    u = [both([t_inv[p, hh] for hh in range(2)], z[p].astype(BF16)) for p in pairs]
    y = [y0[p] - both([jnp.where(incl, a_b[p, hh][c:], 0.0) for hh in range(2)], u[p].astype(BF16)) for p in pairs]
    s1 = []
    for p in pairs:
        upd = _dot_tn(jnp.concatenate([pick(vb, p), -u[p].astype(BF16)], axis=0),
                      jnp.concatenate([pick(k_p, p), pick(b_p, p)], axis=0))
        s1.append(s_bd[p] * jnp.exp(pick(lp_tot, p)) + jnp.where(same, upd, 0.0))
    return jnp.concatenate(y, axis=1), s1


def _rwkv_scan_kernel(r_ref, v_ref, kk_ref, lw_ref, kd_ref, bd_ref, y_ref, s_ref):
    @pl.when(pl.program_id(3) == 0)
    def _():
        s_ref[...] = jnp.zeros_like(s_ref)

    sign = 1 - 2 * pl.program_id(0)
    npairs = s_ref.shape[0]
    y, s1 = _rwkv_chunk_pairs(r_ref[0], v_ref[0], kk_ref[0], lw_ref[0, 0], kd_ref[0, 0], bd_ref[0, 0],
                              [s_ref[p] for p in range(npairs)], sign)
    y_ref[0, 0] = y
    for p in range(npairs):
        s_ref[p] = s1[p]


def _rwkv_scan(r, v, kk, lw, kd, bd, n_ctx):
    bsz, t, e = r.shape
    c = RWKV_CHUNK
    width = RWKV_HEADS_PER_STEP * HEAD_DIM
    nc, nc_ctx = t // c, n_ctx // c

    def chunk(s, ci):
        rev = jnp.where(ci < nc_ctx, nc_ctx - 1 - ci, nc + nc_ctx - 1 - ci)
        return jnp.where(s == 0, ci, rev)

    shared = pl.BlockSpec((1, c, width), lambda s, b, hg, ci: (b, chunk(s, ci), hg))
    per_dir = pl.BlockSpec((1, 1, c, width), lambda s, b, hg, ci: (s, b, chunk(s, ci), hg))
    return pl.pallas_call(
        _rwkv_scan_kernel,
        grid=(2, bsz, e // width, nc),
        in_specs=[shared, shared, shared, per_dir, per_dir, per_dir],
        out_specs=per_dir,
        out_shape=jax.ShapeDtypeStruct((2, bsz, t, e), F32),
        scratch_shapes=[pltpu.VMEM((width // (2 * HEAD_DIM), 2 * HEAD_DIM, 2 * HEAD_DIM), F32)],
        compiler_params=_cparams("arbitrary", "arbitrary", "arbitrary", "arbitrary"),
        name="rwkv_scan",
    )(r, v, kk, lw, kd, bd)


def _rwkv_out_kernel(y_ref, bonus_ref, g_ref, lnw_ref, lnb_ref, w_ref, res_ref, gate_ref, o_ref):
    ones_bd = _same_head(2 * HEAD_DIM).astype(BF16)
    y = y_ref[0] + y_ref[1]
    mean = _head_sum(y, ones_bd) * (1.0 / HEAD_DIM)
    yc = y - mean
    var = _head_sum(yc * yc, ones_bd) * (1.0 / HEAD_DIM)
    yn = yc * lax.rsqrt(var + RWKV_GN_EPS) * lnw_ref[...] + lnb_ref[...]
    g = g_ref[...]
    o = ((yn + bonus_ref[...]) * (g * jax.nn.sigmoid(g))).astype(BF16)
    o_ref[...] = res_ref[...] + gate_ref[0] * jnp.dot(o, w_ref[...], preferred_element_type=F32)


def _rwkv_out(y, bonus, g, ln_w, ln_b, w_out, res2, gate, tm, tiles_per_b, ctx_tiles, n_batch):
    m, e = bonus.shape
    d = w_out.shape[1]
    tok = pl.BlockSpec((tm, e), lambda i: (i, 0))
    vec = pl.BlockSpec((1, e), lambda i: (0, 0))
    return pl.pallas_call(
        _rwkv_out_kernel,
        grid=(m // tm,),
        in_specs=[pl.BlockSpec((2, tm, e), lambda i: (0, i, 0)), tok, tok, vec, vec,
                  pl.BlockSpec((e, d), lambda i: (0, 0)),
                  pl.BlockSpec((tm, d), lambda i: (i, 0)),
                  pl.BlockSpec((1, 1, d), lambda i: (_mod_row(i, tiles_per_b, ctx_tiles, n_batch), 0, 0))],
        out_specs=pl.BlockSpec((tm, d), lambda i: (i, 0)),
        out_shape=jax.ShapeDtypeStruct((m, d), F32),
        compiler_params=_cparams("arbitrary"),
        name="rwkv_out",
    )(y, bonus, g, ln_w.reshape(1, e).astype(F32), ln_b.reshape(1, e).astype(F32), w_out.astype(BF16), res2, gate)


def _rwkv_layer(stream, norm_g, scale, shift, gate, n_ctx, mu, w_rkvg, w0, w1, w2, a0, a1, a2, k_k, k_a, r_k,
                ln_w, ln_b, w_out):
    bsz, t_all, d = stream.shape
    e = w_rkvg.shape[-1]
    m = bsz * t_all
    tm = _token_tile(n_ctx, t_all, 128)
    tile_args = (t_all // tm, n_ctx // tm, bsz)
    s2 = stream.reshape(m, d)
    r, v, kk, g, bonus, lw, kd, bd = _rwkv_proj(s2, norm_g, scale, shift, mu, w_rkvg, w0, w1, w2, a0, a1, a2,
                                                k_k, k_a, r_k, tm, *tile_args)
    b3 = lambda a: a.reshape(bsz, t_all, e)
    b4 = lambda a: a.reshape(2, bsz, t_all, e)
    y = _rwkv_scan(b3(r), b3(v), b3(kk), b4(lw), b4(kd), b4(bd), n_ctx).reshape(2, m, e)
    return _rwkv_out(y, bonus, g, ln_w, ln_b, w_out, s2, gate, tm, *tile_args).reshape(bsz, t_all, d)


def _na_kernel(q_ref, k_ref, v_ref, bias_ref, qg_ref, kg_ref, o_ref, kn_ref, vb_ref, *,
               n_ctx, grid_w, kh, khm, rows, scale):
    qi = pl.program_id(2)
    ctx_tiles = n_ctx // grid_w
    hd = HEAD_DIM
    lanes = 2 * hd
    same_head = (lax.broadcasted_iota(jnp.int32, (lanes, lanes), 0) // hd
                 == lax.broadcasted_iota(jnp.int32, (lanes, lanes), 1) // hd).astype(BF16)
    lane_head = lax.broadcasted_iota(jnp.int32, (1, lanes), 1) // hd

    def head_rms(x, g):
        ss = jnp.dot((x * x).astype(BF16), same_head, preferred_element_type=F32)
        return x * lax.rsqrt(ss * (1.0 / hd) + NORM_EPS) * g

    @pl.when(qi == 0)
    def _():
        kn_ref[...] = head_rms(k_ref[0], kg_ref[...]).astype(BF16)
        vb_ref[...] = v_ref[0].astype(BF16)

    qn = head_rms(q_ref[0], qg_ref[...])
    k_ctx = kn_ref[0:n_ctx, :]
    v_ctx = vb_ref[0:n_ctx, :]

    def attend(k_nb, v_nb, bias_of):
        out = jnp.zeros((grid_w, lanes), F32)
        for hh in range(2):
            sel = lane_head == hh
            qh = jnp.where(sel, qn, 0.0).astype(BF16)
            s_c = _dot_nt(qh, k_ctx) * scale
            m = jnp.max(s_c, axis=-1, keepdims=True)
            if k_nb is not None:
                s_n = _dot_nt(qh, k_nb) * scale + bias_of(hh)
                m = jnp.maximum(m, jnp.max(s_n, axis=-1, keepdims=True))
                p_n = jnp.exp(s_n - m)
            p_c = jnp.exp(s_c - m)
            den = jnp.sum(p_c, axis=-1, keepdims=True)
            acc = _dot(p_c.astype(BF16), v_ctx)
            if k_nb is not None:
                den = den + jnp.sum(p_n, axis=-1, keepdims=True)
                acc = acc + _dot(p_n.astype(BF16), v_nb)
            out = jnp.where(sel, acc / den, out)
        return out

    @pl.when(qi < ctx_tiles)
    def _():
        o_ref[0] = attend(None, None, None)

    @pl.when(qi >= ctx_tiles)
    def _():
        i = qi - ctx_tiles
        r0 = jnp.clip(i - kh // 2, 0, rows - kh)
        start = pl.multiple_of(n_ctx + r0 * grid_w, grid_w)
        k_nb = kn_ref[pl.ds(start, kh * grid_w), :]
        v_nb = vb_ref[pl.ds(start, kh * grid_w), :]
        base = khm - 1 - (i - r0)
        o_ref[0] = attend(k_nb, v_nb, lambda hh: jnp.concatenate(
            [bias_ref[hh, base + 2 * q] for q in range(kh // 2)], axis=1))


def _na_bias_table(rpb, grid_w):
    kw = (rpb.shape[2] + 1) // 2
    j = np.arange(grid_w)[:, None]
    c = np.arange(grid_w)[None, :]
    c0 = np.clip(j - kw // 2, 0, grid_w - kw)
    valid = (c >= c0) & (c < c0 + kw)
    onehot = ((c - j + kw - 1)[None] == np.arange(2 * kw - 1)[:, None, None]) & valid[None]
    tiles = jnp.einsum("hab,bjc->hajc", rpb.astype(F32), jnp.asarray(onehot, F32), precision=HIGHEST)
    tiles = tiles + jnp.asarray(np.where(valid, 0.0, MASK_NEG), F32)
    return jnp.concatenate([tiles[:, :-1], tiles[:, 1:]], axis=-1)


def _na_attention(qkvz, rpb, q_g, k_g, n_ctx, grid_w, kh_max):
    bsz, t_all, e4 = qkvz.shape
    e = e4 // 4
    pairs = e // (2 * HEAD_DIM)
    rows = (t_all - n_ctx) // grid_w
    kh = min(kh_max, rows)
    assert kh % 2 == 0 and kh <= kh_max
    tbl = _na_bias_table(rpb, grid_w)
    nq = t_all // grid_w
    lanes = 2 * HEAD_DIM
    g2 = lambda g: jnp.concatenate([g, g]).reshape(1, lanes).astype(F32)
    return pl.pallas_call(
        functools.partial(_na_kernel, n_ctx=n_ctx, grid_w=grid_w, kh=kh, khm=kh_max, rows=rows,
                          scale=HEAD_DIM ** -0.5),
        grid=(bsz, pairs, nq),
        in_specs=[pl.BlockSpec((1, grid_w, lanes), lambda b, p, qi: (b, qi, p)),
                  pl.BlockSpec((1, t_all, lanes), lambda b, p, qi: (b, 0, pairs + p)),
                  pl.BlockSpec((1, t_all, lanes), lambda b, p, qi: (b, 0, 2 * pairs + p)),
                  pl.BlockSpec((2,) + tbl.shape[1:], lambda b, p, qi: (p, 0, 0, 0)),
                  pl.BlockSpec((1, lanes), lambda b, p, qi: (0, 0)),
                  pl.BlockSpec((1, lanes), lambda b, p, qi: (0, 0))],
        out_specs=pl.BlockSpec((1, grid_w, lanes), lambda b, p, qi: (b, qi, p)),
        out_shape=jax.ShapeDtypeStruct((bsz, t_all, e), F32),
        scratch_shapes=[pltpu.VMEM((t_all, lanes), BF16), pltpu.VMEM((t_all, lanes), BF16)],
        compiler_params=_cparams("arbitrary", "arbitrary", "arbitrary"),
        name="na_attention",
    )(qkvz, qkvz, qkvz, tbl, g2(q_g), g2(k_g))


def _na_layer(stream, h, gate, n_ctx, tm, tile_args, grid_w, w_in, q_g, k_g, rpb, w_out):
    bsz, t_all, d = stream.shape
    m = bsz * t_all
    e = w_in.shape[1] // 4
    qkvz = _mm(h.reshape(m, d), w_in, tm).reshape(bsz, t_all, 4 * e)
    kh_max = (rpb.shape[1] + 1) // 2
    o = _na_attention(qkvz, rpb, q_g, k_g, n_ctx, grid_w, kh_max)
    o = (o * jax.nn.silu(qkvz[..., 3 * e:])).reshape(m, e)
    return _mm_residual(o, w_out, stream.reshape(m, d), gate, tm, *tile_args).reshape(bsz, t_all, d)


def _s5_matrices(lam_re, lam_im, log_dt, b_re, b_im, c_re, c_im):
    nt = S5_CHUNK
    g, p, cg = b_re.shape
    tau = jnp.arange(nt + 1, dtype=F32)[:, None, None]
    i_idx = np.arange(nt)
    kbig, bcat, ccat, a_pow = 0.0, [], [], []
    for s in range(2):
        lr, li = lam_re[s].astype(F32), lam_im[s].astype(F32)
        step = jnp.exp(log_dt[s].astype(F32))[:, None]
        mag = jnp.exp(lr * step)
        ar, ai = mag * jnp.cos(li * step), mag * jnp.sin(li * step)
        den = lr * lr + li * li
        qr = ((ar - 1.0) * lr + ai * li) / den
        qi = (ai * lr - (ar - 1.0) * li) / den
        bbr = qr[..., None] * b_re - qi[..., None] * b_im
        bbi = qr[..., None] * b_im + qi[..., None] * b_re
        pmag = jnp.exp(lr * step * tau)
        pr, pi = pmag * jnp.cos(li * step * tau), pmag * jnp.sin(li * step * tau)
        clr = c_re[None] * pr[:, :, None, :] - c_im[None] * pi[:, :, None, :]
        cli = c_re[None] * pi[:, :, None, :] + c_im[None] * pr[:, :, None, :]
        ker = (jnp.einsum("tgop,gpc->tgoc", clr[:nt], bbr, precision=HIGHEST)
               - jnp.einsum("tgop,gpc->tgoc", cli[:nt], bbi, precision=HIGHEST))
        lbr = pr[:, :, :, None] * bbr[None] - pi[:, :, :, None] * bbi[None]
        lbi = pr[:, :, :, None] * bbi[None] + pi[:, :, :, None] * bbr[None]
        lag = (i_idx[None, :] - i_idx[:, None]) if s == 0 else (i_idx[:, None] - i_idx[None, :])
        kb = jnp.where((lag >= 0)[:, :, None, None, None], ker[np.clip(lag, 0, nt - 1)], 0.0)
        kbig = kbig + kb.transpose(2, 0, 4, 1, 3).reshape(g, nt * cg, nt * cg)
        inj = (nt - 1 - i_idx) if s == 0 else i_idx
        bb = jnp.concatenate([lbr[inj], lbi[inj]], axis=2)
        bcat.append(bb.transpose(1, 0, 3, 2).reshape(g, nt * cg, 2 * p))
        out = (i_idx + 1) if s == 0 else (nt - i_idx)
        cc = jnp.concatenate([clr[out], -cli[out]], axis=3)
        ccat.append(cc.transpose(1, 3, 0, 2).reshape(g, 2 * p, nt * cg))
        a_pow.append((pr[nt], pi[nt]))
    w_in = jnp.concatenate([kbig] + bcat, axis=2)
    w_out = jnp.concatenate(ccat, axis=1)
    return w_in, w_out, a_pow


def _s5_state_kernel(bur_ref, bui_ref, ar_ref, ai_ref, xr_ref, xi_ref):
    nck = bur_ref.shape[0]
    ar = ar_ref[...]
    ai = ai_ref[...]

    def body(c, carry):
        xr, xi = carry
        xr_ref[c] = xr
        xi_ref[c] = xi
        return (ar * xr - ai * xi + bur_ref[c], ar * xi + ai * xr + bui_ref[c])

    zero = jnp.zeros(ar.shape, F32)
    lax.fori_loop(0, nck, body, (zero, zero))


def _s5_states(bur, bui, ar, ai):
    nck, rows, lanes = bur.shape
    tl = 512
    blk = pl.BlockSpec((nck, rows, tl), lambda j: (0, 0, j))
    coef = pl.BlockSpec((rows, tl), lambda j: (0, j))
    return pl.pallas_call(
        _s5_state_kernel,
        grid=(lanes // tl,),
        in_specs=[blk, blk, coef, coef],
        out_specs=[blk, blk],
        out_shape=[jax.ShapeDtypeStruct(bur.shape, F32)] * 2,
        compiler_params=_cparams("arbitrary"),
        name="s5_states",
    )(bur, bui, ar, ai)


def _s5_layer(stream, h, gate, n_ctx, tm, tile_args, w_in, lam_re, lam_im, log_dt, b_re, b_im, c_re, c_im,
              d_skip, w_glu, b_glu, w_out):
    bsz, t_all, d = stream.shape
    m = bsz * t_all
    e = w_in.shape[1] // 2
    nt, cg = S5_CHUNK, S5_GROUP
    g = e // cg
    p = b_re.shape[1]
    nck, nck_ctx = t_all // nt, n_ctx // nt
    uz = _mm(h.reshape(m, d), w_in, tm)
    u, z = uz[:, :e], uz[:, e:]
    k_in, k_out, a_pow = _s5_matrices(lam_re, lam_im, log_dt, b_re, b_im, c_re, c_im)
    u_flat = u.reshape(bsz, nck, nt, g, cg).transpose(3, 0, 1, 2, 4).reshape(g, bsz * nck, nt * cg)
    yb = _grouped_mm(u_flat, k_in)
    bu = yb[:, :, nt * cg:].reshape(g, bsz, nck, 2, 2, p)
    rev = np.concatenate([np.arange(nck_ctx)[::-1], np.arange(nck_ctx, nck)[::-1]])
    bu = bu.transpose(4, 2, 3, 1, 0, 5)
    bu = jnp.stack([bu[:, :, 0], bu[:, rev, 1]], axis=2).reshape(2, nck, 2 * bsz, g * p)
    coef = lambda a: jnp.broadcast_to(jnp.stack(a)[:, None], (2, bsz, g, p)).reshape(2 * bsz, g * p)
    xr, xi = _s5_states(bu[0], bu[1], coef([a_pow[0][0], a_pow[1][0]]), coef([a_pow[0][1], a_pow[1][1]]))
    inv = np.argsort(rev)

    def unorder(x):
        x = x.reshape(nck, 2, bsz, g, p)
        x = jnp.stack([x[:, 0], x[inv, 1]], axis=1)
        return x.transpose(3, 2, 0, 1, 4).reshape(g, bsz * nck, 2, p)

    xr, xi = unorder(xr), unorder(xi)
    x_cat = jnp.concatenate([xr[:, :, 0], xi[:, :, 0], xr[:, :, 1], xi[:, :, 1]], axis=-1)
    y_flat = _grouped_mm(x_cat, k_out, add=yb, add_col_block=0)
    y = y_flat.reshape(g, bsz, nck, nt, cg).transpose(1, 2, 3, 0, 4).reshape(m, e)
    y = jax.nn.gelu(y + d_skip * u)
    y = y * jax.nn.sigmoid(_mm(y, w_glu, tm, bias=b_glu))
    o = y * jax.nn.silu(z)
    return _mm_residual(o, w_out, stream.reshape(m, d), gate, tm, *tile_args).reshape(bsz, t_all, d)


def kernel(x, c, ctx, c_ctx, norm_g, w_mod, b_mod, rwkv_mu, rwkv_w_rkvg, rwkv_w0, rwkv_w1, rwkv_w2, rwkv_a0, rwkv_a1, rwkv_a2, rwkv_k_k, rwkv_k_a, rwkv_r_k, rwkv_ln_w, rwkv_ln_b, rwkv_w_out, na_w_in, na_q_g, na_k_g, na_rpb, na_w_out, s5_w_in, s5_lam_re, s5_lam_im, s5_log_dt, s5_b_re, s5_b_im, s5_c_re, s5_c_im, s5_d, s5_w_glu, s5_b_glu, s5_w_out):
    bsz, n_lat, d = x.shape
    n_ctx = ctx.shape[1]
    t_all = n_ctx + n_lat
    depth = norm_g.shape[0]
    grid_w = 64
    tm = _token_tile(n_ctx, t_all)
    tile_args = (t_all // tm, n_ctx // tm, bsz)
    stream = jnp.concatenate([ctx, x], axis=1).astype(F32)
    rows = 8 * ((bsz + 1 + 7) // 8)
    cc = jnp.zeros((rows, d), F32).at[:bsz].set(c.astype(F32)).at[bsz].set(c_ctx.astype(F32))
    for i in range(depth):
        kind, j = i % 3, i // 3
        mod = _modulation(cc, w_mod[i].astype(F32), b_mod[i].astype(F32))[:bsz + 1]
        shift, scale, gate = (mod[:, k * d:(k + 1) * d].reshape(bsz + 1, 1, d) for k in range(3))
        if kind == 0:
            stream = _rwkv_layer(stream, norm_g[i], scale, shift, gate, n_ctx, rwkv_mu[j], rwkv_w_rkvg[j], rwkv_w0[j],
                                 rwkv_w1[j], rwkv_w2[j], rwkv_a0[j], rwkv_a1[j], rwkv_a2[j], rwkv_k_k[j],
                                 rwkv_k_a[j], rwkv_r_k[j], rwkv_ln_w[j], rwkv_ln_b[j], rwkv_w_out[j])
            continue
        h = _norm_mod(stream.reshape(bsz * t_all, d), norm_g[i], scale, shift, tm, *tile_args).reshape(bsz, t_all, d)
        if kind == 1:
            stream = _na_layer(stream, h, gate, n_ctx, tm, tile_args, grid_w, na_w_in[j], na_q_g[j], na_k_g[j],
                               na_rpb[j], na_w_out[j])
        else:
            stream = _s5_layer(stream, h, gate, n_ctx, tm, tile_args, s5_w_in[j], s5_lam_re[j], s5_lam_im[j],
                               s5_log_dt[j], s5_b_re[j], s5_b_im[j], s5_c_re[j], s5_c_im[j], s5_d[j],
                               s5_w_glu[j], s5_b_glu[j], s5_w_out[j])
    return stream[:, n_ctx:].astype(x.dtype)
```

```python
import functools
import math

import numpy as np
import jax
import jax.numpy as jnp
from jax import lax
from jax.experimental import pallas as pl
from jax.experimental.pallas import tpu as pltpu

F32 = jnp.float32
BF16 = jnp.bfloat16
NORM_EPS = 1e-6
RWKV_GN_EPS = 64e-5
HEAD_DIM = 64
RWKV_CHUNK = 64
RWKV_HEADS_PER_STEP = 8
NA_ROWS_PER_STEP = 2
S5_CHUNK = 16
S5_GROUP = 16
MASK_NEG = -1e30
VMEM_LIMIT = 48 * 1024 * 1024
HIGHEST = lax.Precision.HIGHEST


def _cparams(*sem):
    return pltpu.CompilerParams(dimension_semantics=sem, vmem_limit_bytes=VMEM_LIMIT)


def _token_tile(n_ctx, n_all, largest=256):
    for t in (256, 128, 64):
        if t <= largest and n_ctx % t == 0 and n_all % t == 0:
            return t
    raise ValueError("context / sequence lengths must be multiples of 64")


def _mod_row(i, tiles_per_b, ctx_tiles, n_batch):
    return jnp.where(i % tiles_per_b < ctx_tiles, n_batch, i // tiles_per_b)


def _mod_kernel(c_ref, w_ref, b_ref, o_ref):
    c = c_ref[...]
    s = c * jax.nn.sigmoid(c)
    o_ref[...] = jnp.dot(s, w_ref[...], precision=HIGHEST, preferred_element_type=F32) + b_ref[...]


def _modulation(cc, wm, bm):
    rows, d = cc.shape
    n = wm.shape[1]
    tn = 512
    return pl.pallas_call(
        _mod_kernel,
        grid=(n // tn,),
        in_specs=[pl.BlockSpec((rows, d), lambda j: (0, 0)),
                  pl.BlockSpec((d, tn), lambda j: (0, j)),
                  pl.BlockSpec((1, tn), lambda j: (0, j))],
        out_specs=pl.BlockSpec((rows, tn), lambda j: (0, j)),
        out_shape=jax.ShapeDtypeStruct((rows, n), F32),
        compiler_params=_cparams("arbitrary"),
        name="modulation",
    )(cc, wm, bm.reshape(1, n))


def _norm_mm_kernel(x_ref, g_ref, sc_ref, sh_ref, w_ref, o_ref, h_ref):
    @pl.when(pl.program_id(1) == 0)
    def _():
        x = x_ref[...]
        ms = jnp.mean(x * x, axis=-1, keepdims=True)
        y = x * lax.rsqrt(ms + NORM_EPS) * g_ref[...]
        h_ref[...] = (y * (1.0 + sc_ref[0]) + sh_ref[0]).astype(BF16)

    o_ref[...] = jnp.dot(h_ref[...], w_ref[...], preferred_element_type=F32)


def _norm_mm(x2, g, scale, shift, w, tm, tiles_per_b, ctx_tiles, n_batch):
    m, d = x2.shape
    n = w.shape[1]
    tn = n if n <= 2048 else 2048
    row = lambda i, j: (_mod_row(i, tiles_per_b, ctx_tiles, n_batch), 0, 0)
    return pl.pallas_call(
        _norm_mm_kernel,
        grid=(m // tm, n // tn),
        in_specs=[pl.BlockSpec((tm, d), lambda i, j: (i, 0)),
                  pl.BlockSpec((1, d), lambda i, j: (0, 0)),
                  pl.BlockSpec((1, 1, d), row),
                  pl.BlockSpec((1, 1, d), row),
                  pl.BlockSpec((d, tn), lambda i, j: (0, j))],
        out_specs=pl.BlockSpec((tm, tn), lambda i, j: (i, j)),
        out_shape=jax.ShapeDtypeStruct((m, n), F32),
        scratch_shapes=[pltpu.VMEM((tm, d), BF16)],
        compiler_params=_cparams("arbitrary", "arbitrary"),
        name="norm_matmul",
    )(x2, g.reshape(1, d), scale, shift, w.astype(BF16))


def _mm_kernel(x_ref, w_ref, o_ref):
    o_ref[...] = jnp.dot(x_ref[...].astype(BF16), w_ref[...], preferred_element_type=F32)


def _mm_bias_kernel(x_ref, w_ref, b_ref, o_ref):
    o_ref[...] = jnp.dot(x_ref[...].astype(BF16), w_ref[...], preferred_element_type=F32) + b_ref[...]


def _mm(x2, w, tm, bias=None):
    m, k = x2.shape
    n = w.shape[1]
    tn = n if n <= 2048 else 2048
    in_specs = [pl.BlockSpec((tm, k), lambda i, j: (i, 0)),
                pl.BlockSpec((k, tn), lambda i, j: (0, j))]
    args = [x2, w.astype(BF16)]
    kern = _mm_kernel
    if bias is not None:
        in_specs.append(pl.BlockSpec((1, tn), lambda i, j: (0, j)))
        args.append(bias.reshape(1, n).astype(F32))
        kern = _mm_bias_kernel
    return pl.pallas_call(
        kern,
        grid=(m // tm, n // tn),
        in_specs=in_specs,
        out_specs=pl.BlockSpec((tm, tn), lambda i, j: (i, j)),
        out_shape=jax.ShapeDtypeStruct((m, n), F32),
        compiler_params=_cparams("arbitrary", "arbitrary"),
        name="matmul",
    )(*args)


def _mm_res_kernel(x_ref, w_ref, res_ref, gate_ref, o_ref):
    acc = jnp.dot(x_ref[...].astype(BF16), w_ref[...], preferred_element_type=F32)
    o_ref[...] = res_ref[...] + gate_ref[0] * acc


def _mm_residual(x2, w, res2, gate, tm, tiles_per_b, ctx_tiles, n_batch):
    m, k = x2.shape
    n = w.shape[1]
    return pl.pallas_call(
        _mm_res_kernel,
        grid=(m // tm,),
        in_specs=[pl.BlockSpec((tm, k), lambda i: (i, 0)),
                  pl.BlockSpec((k, n), lambda i: (0, 0)),
                  pl.BlockSpec((tm, n), lambda i: (i, 0)),
                  pl.BlockSpec((1, 1, n), lambda i: (_mod_row(i, tiles_per_b, ctx_tiles, n_batch), 0, 0))],
        out_specs=pl.BlockSpec((tm, n), lambda i: (i, 0)),
        out_shape=jax.ShapeDtypeStruct((m, n), F32),
        compiler_params=_cparams("arbitrary"),
        name="matmul_residual",
    )(x2, w.astype(BF16), res2, gate)


def _gmm_kernel(x_ref, w_ref, o_ref):
    o_ref[0] = jnp.dot(x_ref[0].astype(BF16), w_ref[0], preferred_element_type=F32)


def _gmm_add_kernel(x_ref, w_ref, a_ref, o_ref):
    o_ref[0] = a_ref[0] + jnp.dot(x_ref[0].astype(BF16), w_ref[0], preferred_element_type=F32)


def _grouped_mm(x3, w3, add=None, add_col_block=0):
    g, m, k = x3.shape
    n = w3.shape[2]
    in_specs = [pl.BlockSpec((1, m, k), lambda i: (i, 0, 0)),
                pl.BlockSpec((1, k, n), lambda i: (i, 0, 0))]
    args = [x3, w3.astype(BF16)]
    kern = _gmm_kernel
    if add is not None:
        in_specs.append(pl.BlockSpec((1, m, n), lambda i: (i, 0, add_col_block)))
        args.append(add)
        kern = _gmm_add_kernel
    return pl.pallas_call(
        kern,
        grid=(g,),
        in_specs=in_specs,
        out_specs=pl.BlockSpec((1, m, n), lambda i: (i, 0, 0)),
        out_shape=jax.ShapeDtypeStruct((g, m, n), F32),
        compiler_params=_cparams("arbitrary"),
        name="grouped_matmul",
    )(*args)


def _dot_nt(a, b):
    return lax.dot_general(a, b, (((1,), (1,)), ((), ())), preferred_element_type=F32)


def _dot_tn(a, b):
    return lax.dot_general(a, b, (((0,), (0,)), ((), ())), preferred_element_type=F32)


def _dot(a, b):
    return jnp.dot(a, b, preferred_element_type=F32)


def _head_sum(x, ones_bd):
    lanes = ones_bd.shape[0]
    return jnp.concatenate(
        [jnp.dot(x[:, p * lanes:(p + 1) * lanes].astype(BF16), ones_bd, preferred_element_type=F32)
         for p in range(x.shape[1] // lanes)], axis=1)


def _same_head(lanes):
    return (lax.broadcasted_iota(jnp.int32, (lanes, lanes), 0) // HEAD_DIM
            == lax.broadcasted_iota(jnp.int32, (lanes, lanes), 1) // HEAD_DIM)


def _rwkv_proj_kernel(x_ref, xp_ref, xn_ref, g_ref, sc_ref, sh_ref, mu_ref, w_ref, w1_ref, w2_ref, a1_ref, a2_ref,
                      w0_ref, a0_ref, kk_ref, ka_ref, rk_ref,
                      r_out, v_out, kk_out, g_out, bonus_out, lw_out, kd_out, bd_out, *, tm, tiles_per_b, ctx_tiles):
    i = pl.program_id(0)
    j = i % tiles_per_b
    gain = g_ref[...]
    sc = 1.0 + sc_ref[0]
    sh = sh_ref[0]

    def norm(x):
        ms = jnp.mean(x * x, axis=-1, keepdims=True)
        return x * lax.rsqrt(ms + NORM_EPS) * gain * sc + sh

    h = norm(x_ref[...])
    has_prev = jnp.logical_and(j != 0, j != ctx_tiles)
    has_next = jnp.logical_and(j != ctx_tiles - 1, j != tiles_per_b - 1)
    h_prev = jnp.where(has_prev, norm(xp_ref[...])[7:8], 0.0)
    h_next = jnp.where(has_next, norm(xn_ref[...])[0:1], 0.0)
    row = lax.broadcasted_iota(jnp.int32, h.shape, 0)
    prev = jnp.where(row == 0, h_prev, pltpu.roll(h, 1, axis=0))
    nxt = jnp.where(row == tm - 1, h_next, pltpu.roll(h, tm - 1, axis=0))
    xx = 0.5 * (prev + nxt) - h
    mix = lambda n: (h + xx * mu_ref[n:n + 1, :]).astype(BF16)

    r = jnp.dot(mix(0), w_ref[0], preferred_element_type=F32)
    k = jnp.dot(mix(1), w_ref[1], preferred_element_type=F32)
    v = jnp.dot(mix(2), w_ref[2], preferred_element_type=F32)
    g_out[...] = jnp.dot(mix(3), w_ref[3], preferred_element_type=F32)
    dec = jnp.dot(jnp.tanh(jnp.dot(mix(4), w1_ref[...], preferred_element_type=F32)).astype(BF16), w2_ref[...],
                  preferred_element_type=F32)
    icl = jnp.dot(jnp.dot(mix(5), a1_ref[...], preferred_element_type=F32).astype(BF16), a2_ref[...],
                  preferred_element_type=F32)
    e = r.shape[1]
    ones_bd = _same_head(2 * HEAD_DIM).astype(BF16)
    kkf = k * kk_ref[...]
    kk = kkf / jnp.maximum(jnp.sqrt(_head_sum(kkf * kkf, ones_bd)), 1e-12)
    r_out[...] = r
    v_out[...] = v
    kk_out[...] = kk
    bonus_out[...] = _head_sum(r * k * rk_ref[...], ones_bd) * v
    for s in range(2):
        lw_out[s] = -math.exp(-0.5) * jax.nn.sigmoid(w0_ref[s:s + 1, :] + dec[:, s * e:(s + 1) * e])
        a = jax.nn.sigmoid(a0_ref[s:s + 1, :] + icl[:, s * e:(s + 1) * e])
        kd_out[s] = k * (1.0 + (a - 1.0) * ka_ref[...])
        bd_out[s] = kk * a


def _rwkv_proj(stream2, norm_g, scale, shift, mu, w_rkvg, w0, w1, w2, a0, a1, a2, k_k, k_a, r_k, tm, tiles_per_b,
               ctx_tiles, n_batch):
    m, d = stream2.shape
    e = w_rkvg.shape[-1]
    lr = w1.shape[-1]
    nblk = m // 8
    zeros = jnp.zeros((lr, e), F32)
    cat = lambda w: jnp.concatenate([w[0], w[1]], axis=1).astype(BF16)
    bdiag = lambda w: jnp.concatenate([jnp.concatenate([w[0], zeros], axis=1),
                                       jnp.concatenate([zeros, w[1]], axis=1)], axis=0).astype(BF16)
    row = lambda i: (_mod_row(i, tiles_per_b, ctx_tiles, n_batch), 0, 0)
    full = lambda shape: pl.BlockSpec(shape, lambda i: (0,) * len(shape))
    tok = pl.BlockSpec((tm, e), lambda i: (i, 0))
    tok2 = pl.BlockSpec((2, tm, e), lambda i: (0, i, 0))
    vec = lambda a: a.reshape(1, e).astype(F32)
    return pl.pallas_call(
        functools.partial(_rwkv_proj_kernel, tm=tm, tiles_per_b=tiles_per_b, ctx_tiles=ctx_tiles),
        grid=(m // tm,),
        in_specs=[pl.BlockSpec((tm, d), lambda i: (i, 0)),
                  pl.BlockSpec((8, d), lambda i: (jnp.maximum(i * (tm // 8) - 1, 0), 0)),
                  pl.BlockSpec((8, d), lambda i: (jnp.minimum((i + 1) * (tm // 8), nblk - 1), 0)),
                  full((1, d)), pl.BlockSpec((1, 1, d), row), pl.BlockSpec((1, 1, d), row),
                  full(mu.shape), full(w_rkvg.shape), full((d, 2 * lr)), full((2 * lr, 2 * e)),
                  full((d, 2 * lr)), full((2 * lr, 2 * e)), full((2, e)), full((2, e)),
                  full((1, e)), full((1, e)), full((1, e))],
        out_specs=[tok, tok, tok, tok, tok, tok2, tok2, tok2],
        out_shape=[jax.ShapeDtypeStruct((m, e), F32)] * 5 + [jax.ShapeDtypeStruct((2, m, e), F32)] * 3,
        compiler_params=_cparams("arbitrary"),
        name="rwkv_proj",
    )(stream2, stream2, stream2, norm_g.reshape(1, d), scale, shift, mu.astype(F32), w_rkvg.astype(BF16),
      cat(w1), bdiag(w2), cat(a1), bdiag(a2), w0.astype(F32), a0.astype(F32), vec(k_k), vec(k_a), vec(r_k))


def _cumsum_rows(x, reverse):
    c = x.shape[0]
    row = lax.broadcasted_iota(jnp.int32, x.shape, 0)
    shift = 1
    while shift < c:
        if reverse:
            x = x + jnp.where(row < c - shift, pltpu.roll(x, c - shift, axis=0), 0.0)
        else:
            x = x + jnp.where(row >= shift, pltpu.roll(x, shift, axis=0), 0.0)
        shift *= 2
    return x


def _rwkv_chunk_streams(streams):
    c = streams[0][0].shape[0]
    hd = HEAD_DIM
    lanes = 2 * hd
    row = lax.broadcasted_iota(jnp.int32, (c, c), 0)
    col = lax.broadcasted_iota(jnp.int32, (c, c), 1)
    eye = (row == col).astype(F32)
    lane_head = lax.broadcasted_iota(jnp.int32, (1, lanes), 1) // hd
    first = lane_head == 0
    same = _same_head(lanes)
    pick = lambda a, p: a[:, p * lanes:(p + 1) * lanes]
    both = lambda mats, x: jnp.where(first, _dot(mats[0].astype(BF16), x), _dot(mats[1].astype(BF16), x))
    zero = jnp.zeros((), BF16)

    units, incl, strict = [], {}, {}
    lhs, k_h, b_h, k_p, b_p, vb, lp_tot, s_bd = {}, {}, {}, {}, {}, {}, {}, {}
    for si, (r, v, kk, lw, kd, bd, states, reverse) in enumerate(streams):
        incl[si] = (col >= row) if reverse else (col <= row)
        strict[si] = (col > row) if reverse else (col < row)
        lp = _cumsum_rows(lw, reverse)
        tot = jnp.sum(lw, axis=0, keepdims=True)
        lhs_f = jnp.concatenate([kk * jnp.exp(lp - lw), r * jnp.exp(lp)], axis=0).astype(BF16)
        e_ninc = jnp.exp(-lp)
        e_rem = jnp.exp(tot - lp)
        full = dict(lhs=lhs_f, k_h=(kd * e_ninc).astype(BF16), b_h=(bd * e_ninc).astype(BF16),
                    k_p=(kd * e_rem).astype(BF16), b_p=(bd * e_rem).astype(BF16), vb=v.astype(BF16), tot=tot)
        for p in range(len(states)):
            u_ = (si, p)
            units.append(u_)
            lhs[u_], k_h[u_], b_h[u_] = pick(full["lhs"], p), pick(full["k_h"], p), pick(full["b_h"], p)
            k_p[u_], b_p[u_], vb[u_] = pick(full["k_p"], p), pick(full["b_p"], p), pick(full["vb"], p)
            lp_tot[u_], s_bd[u_] = pick(full["tot"], p), states[p]
    heads = [(u_, hh) for u_ in units for hh in range(2)]

    a_k = {(u_, hh): _dot_nt(lhs[u_], jnp.where(lane_head == hh, k_h[u_], zero)) for u_, hh in heads}
    a_b = {(u_, hh): _dot_nt(lhs[u_], jnp.where(lane_head == hh, b_h[u_], zero)) for u_, hh in heads}
    ls = {u_: _dot_nt(lhs[u_], s_bd[u_].astype(BF16)) for u_ in units}
    n_mat = {(u_, hh): jnp.where(strict[u_[0]], a_b[u_, hh][:c], 0.0) for u_, hh in heads}
    z = {u_: ls[u_][:c] + both([jnp.where(strict[u_[0]], a_k[u_, hh][:c], 0.0) for hh in range(2)], vb[u_])
         for u_ in units}
    y0 = {u_: ls[u_][c:] + both([jnp.where(incl[u_[0]], a_k[u_, hh][c:], 0.0) for hh in range(2)], vb[u_])
          for u_ in units}

    t_inv = {h_: eye - n_mat[h_] for h_ in heads}
    pw = {h_: n_mat[h_].astype(BF16) for h_ in heads}
    for _ in range(int(math.log2(c)) - 1):
        pw = {h_: _dot(pw[h_], pw[h_]).astype(BF16) for h_ in heads}
        t_inv = {h_: t_inv[h_] + _dot(t_inv[h_].astype(BF16), pw[h_]) for h_ in heads}

    u = {u_: both([t_inv[u_, hh] for hh in range(2)], z[u_].astype(BF16)) for u_ in units}
    y = {u_: y0[u_] - both([jnp.where(incl[u_[0]], a_b[u_, hh][c:], 0.0) for hh in range(2)], u[u_].astype(BF16))
         for u_ in units}
    s1 = {}
    for u_ in units:
        upd = _dot_tn(jnp.concatenate([vb[u_], -u[u_].astype(BF16)], axis=0),
                      jnp.concatenate([k_p[u_], b_p[u_]], axis=0))
        s1[u_] = s_bd[u_] * jnp.exp(lp_tot[u_]) + jnp.where(same, upd, 0.0)
    return [(jnp.concatenate([y[si, p] for p in range(len(st[6]))], axis=1),
             [s1[si, p] for p in range(len(st[6]))]) for si, st in enumerate(streams)]


def _rwkv_scan_kernel(rf_ref, vf_ref, kkf_ref, lwf_ref, kdf_ref, bdf_ref,
                      rb_ref, vb_ref, kkb_ref, lwb_ref, kdb_ref, bdb_ref, yf_ref, yb_ref, s_ref):
    @pl.when(pl.program_id(2) == 0)
    def _():
        s_ref[...] = jnp.zeros_like(s_ref)

    npairs = s_ref.shape[0] // 2
    fwd = (rf_ref[0], vf_ref[0], kkf_ref[0], lwf_ref[0, 0], kdf_ref[0, 0], bdf_ref[0, 0],
           [s_ref[p] for p in range(npairs)], False)
    bwd = (rb_ref[0], vb_ref[0], kkb_ref[0], lwb_ref[0, 0], kdb_ref[0, 0], bdb_ref[0, 0],
           [s_ref[npairs + p] for p in range(npairs)], True)
    (y_f, s_f), (y_b, s_b) = _rwkv_chunk_streams([fwd, bwd])
    yf_ref[0] = y_f
    yb_ref[0] = y_b
    for p in range(npairs):
        s_ref[p] = s_f[p]
        s_ref[npairs + p] = s_b[p]


def _rwkv_scan(r, v, kk, lw, kd, bd, n_ctx):
    bsz, t, e = r.shape
    c = RWKV_CHUNK
    width = RWKV_HEADS_PER_STEP * HEAD_DIM
    nc, nc_ctx = t // c, n_ctx // c
    rev = lambda ci: jnp.where(ci < nc_ctx, nc_ctx - 1 - ci, nc + nc_ctx - 1 - ci)
    tok_f = pl.BlockSpec((1, c, width), lambda b, hg, ci: (b, ci, hg))
    tok_b = pl.BlockSpec((1, c, width), lambda b, hg, ci: (b, rev(ci), hg))
    dir_f = pl.BlockSpec((1, 1, c, width), lambda b, hg, ci: (0, b, ci, hg))
    dir_b = pl.BlockSpec((1, 1, c, width), lambda b, hg, ci: (1, b, rev(ci), hg))
    return pl.pallas_call(
        _rwkv_scan_kernel,
        grid=(bsz, e // width, nc),
        in_specs=[tok_f, tok_f, tok_f, dir_f, dir_f, dir_f, tok_b, tok_b, tok_b, dir_b, dir_b, dir_b],
        out_specs=[tok_f, tok_b],
        out_shape=[jax.ShapeDtypeStruct((bsz, t, e), F32)] * 2,
        scratch_shapes=[pltpu.VMEM((2 * width // (2 * HEAD_DIM), 2 * HEAD_DIM, 2 * HEAD_DIM), F32)],
        compiler_params=_cparams("arbitrary", "arbitrary", "arbitrary"),
        name="rwkv_scan",
    )(r, v, kk, lw, kd, bd, r, v, kk, lw, kd, bd)


def _rwkv_out_kernel(yf_ref, yb_ref, bonus_ref, g_ref, lnw_ref, lnb_ref, w_ref, res_ref, gate_ref, o_ref):
    ones_bd = _same_head(2 * HEAD_DIM).astype(BF16)
    y = yf_ref[...] + yb_ref[...]
    mean = _head_sum(y, ones_bd) * (1.0 / HEAD_DIM)
    yc = y - mean
    var = _head_sum(yc * yc, ones_bd) * (1.0 / HEAD_DIM)
    yn = yc * lax.rsqrt(var + RWKV_GN_EPS) * lnw_ref[...] + lnb_ref[...]
    g = g_ref[...]
    o = ((yn + bonus_ref[...]) * (g * jax.nn.sigmoid(g))).astype(BF16)
    o_ref[...] = res_ref[...] + gate_ref[0] * jnp.dot(o, w_ref[...], preferred_element_type=F32)


def _rwkv_out(y_f, y_b, bonus, g, ln_w, ln_b, w_out, res2, gate, tm, tiles_per_b, ctx_tiles, n_batch):
    m, e = bonus.shape
    d = w_out.shape[1]
    tok = pl.BlockSpec((tm, e), lambda i: (i, 0))
    vec = pl.BlockSpec((1, e), lambda i: (0, 0))
    return pl.pallas_call(
        _rwkv_out_kernel,
        grid=(m // tm,),
        in_specs=[tok, tok, tok, tok, vec, vec,
                  pl.BlockSpec((e, d), lambda i: (0, 0)),
                  pl.BlockSpec((tm, d), lambda i: (i, 0)),
                  pl.BlockSpec((1, 1, d), lambda i: (_mod_row(i, tiles_per_b, ctx_tiles, n_batch), 0, 0))],
        out_specs=pl.BlockSpec((tm, d), lambda i: (i, 0)),
        out_shape=jax.ShapeDtypeStruct((m, d), F32),
        compiler_params=_cparams("arbitrary"),
        name="rwkv_out",
    )(y_f, y_b, bonus, g, ln_w.reshape(1, e).astype(F32), ln_b.reshape(1, e).astype(F32), w_out.astype(BF16),
      res2, gate)


def _rwkv_layer(stream, norm_g, scale, shift, gate, n_ctx, mu, w_rkvg, w0, w1, w2, a0, a1, a2, k_k, k_a, r_k,
                ln_w, ln_b, w_out):
    bsz, t_all, d = stream.shape
    e = w_rkvg.shape[-1]
    m = bsz * t_all
    tm = _token_tile(n_ctx, t_all, 128)
    tile_args = (t_all // tm, n_ctx // tm, bsz)
    s2 = stream.reshape(m, d)
    r, v, kk, g, bonus, lw, kd, bd = _rwkv_proj(s2, norm_g, scale, shift, mu, w_rkvg, w0, w1, w2, a0, a1, a2,
                                                k_k, k_a, r_k, tm, *tile_args)
    b3 = lambda a: a.reshape(bsz, t_all, e)
    b4 = lambda a: a.reshape(2, bsz, t_all, e)
    y_f, y_b = _rwkv_scan(b3(r), b3(v), b3(kk), b4(lw), b4(kd), b4(bd), n_ctx)
    return _rwkv_out(y_f.reshape(m, e), y_b.reshape(m, e), bonus, g, ln_w, ln_b, w_out, s2, gate, tm,
                     *tile_args).reshape(bsz, t_all, d)


def _na_kernel(q_ref, k_ref, v_ref, z_ref, bias_ref, qg_ref, kg_ref, o_ref, kn_ref, vb_ref, *,
               n_ctx, grid_w, kh, khm, rows, scale, rq):
    step = pl.program_id(2)
    ctx_steps = n_ctx // (grid_w * rq)
    hd = HEAD_DIM
    lanes = 2 * hd
    same_head = _same_head(lanes).astype(BF16)
    lane_head = lax.broadcasted_iota(jnp.int32, (1, lanes), 1) // hd
    first = lane_head == 0

    def head_rms(x, g):
        ss = jnp.dot((x * x).astype(BF16), same_head, preferred_element_type=F32)
        return x * lax.rsqrt(ss * (1.0 / hd) + NORM_EPS) * g

    @pl.when(step == 0)
    def _():
        kn_ref[...] = head_rms(k_ref[0], kg_ref[...]).astype(BF16)
        vb_ref[...] = v_ref[0].astype(BF16)

    qn = head_rms(q_ref[0], qg_ref[...])
    z = z_ref[0]
    zgate = z * jax.nn.sigmoid(z)
    k_ctx = kn_ref[0:n_ctx, :]
    v_ctx = vb_ref[0:n_ctx, :]
    chains = [(s, hh) for s in range(rq) for hh in range(2)]
    qh = {(s, hh): jnp.where(lane_head == hh, qn[s * grid_w:(s + 1) * grid_w], 0.0).astype(BF16)
          for s, hh in chains}

    def attend(bands):
        s_c = {ch: _dot_nt(qh[ch], k_ctx) * scale for ch in chains}
        m = {ch: jnp.max(s_c[ch], axis=-1, keepdims=True) for ch in chains}
        if bands is not None:
            s_n = {(s, hh): _dot_nt(qh[s, hh], bands[s][0]) * scale + bands[s][2](hh) for s, hh in chains}
            m = {ch: jnp.maximum(m[ch], jnp.max(s_n[ch], axis=-1, keepdims=True)) for ch in chains}
            p_n = {ch: jnp.exp(s_n[ch] - m[ch]) for ch in chains}
        p_c = {ch: jnp.exp(s_c[ch] - m[ch]) for ch in chains}
        den = {ch: jnp.sum(p_c[ch], axis=-1, keepdims=True) for ch in chains}
        acc = {ch: _dot(p_c[ch].astype(BF16), v_ctx) for ch in chains}
        if bands is not None:
            den = {ch: den[ch] + jnp.sum(p_n[ch], axis=-1, keepdims=True) for ch in chains}
            acc = {(s, hh): acc[s, hh] + _dot(p_n[s, hh].astype(BF16), bands[s][1]) for s, hh in chains}
        out = [jnp.where(first, acc[s, 0] / den[s, 0], acc[s, 1] / den[s, 1]) for s in range(rq)]
        return jnp.concatenate(out, axis=0) * zgate

    @pl.when(step < ctx_steps)
    def _():
        o_ref[0] = attend(None)

    @pl.when(step >= ctx_steps)
    def _():
        bands = []
        for s in range(rq):
            i = (step - ctx_steps) * rq + s
            r0 = jnp.clip(i - kh // 2, 0, rows - kh)
            start = pl.multiple_of(n_ctx + r0 * grid_w, grid_w)
            base = khm - 1 - (i - r0)
            bias_of = functools.partial(
                lambda hh, base: jnp.concatenate([bias_ref[hh, base + 2 * q] for q in range(kh // 2)], axis=1),
                base=base)
            bands.append((kn_ref[pl.ds(start, kh * grid_w), :], vb_ref[pl.ds(start, kh * grid_w), :], bias_of))
        o_ref[0] = attend(bands)


def _na_bias_table(rpb, grid_w):
    kw = (rpb.shape[2] + 1) // 2
    j = np.arange(grid_w)[:, None]
    c = np.arange(grid_w)[None, :]
    c0 = np.clip(j - kw // 2, 0, grid_w - kw)
    valid = (c >= c0) & (c < c0 + kw)
    onehot = ((c - j + kw - 1)[None] == np.arange(2 * kw - 1)[:, None, None]) & valid[None]
    tiles = jnp.einsum("hab,bjc->hajc", rpb.astype(F32), jnp.asarray(onehot, F32), precision=HIGHEST)
    tiles = tiles + jnp.asarray(np.where(valid, 0.0, MASK_NEG), F32)
    return jnp.concatenate([tiles[:, :-1], tiles[:, 1:]], axis=-1)


def _na_attention(qkvz, rpb, q_g, k_g, n_ctx, grid_w, kh_max):
    bsz, t_all, e4 = qkvz.shape
    e = e4 // 4
    pairs = e // (2 * HEAD_DIM)
    rows = (t_all - n_ctx) // grid_w
    kh = min(kh_max, rows)
    assert kh % 2 == 0 and kh <= kh_max
    tbl = _na_bias_table(rpb, grid_w)
    ctx_tiles = n_ctx // grid_w
    rq = NA_ROWS_PER_STEP if (ctx_tiles % NA_ROWS_PER_STEP == 0 and rows % NA_ROWS_PER_STEP == 0) else 1
    lanes = 2 * HEAD_DIM
    g2 = lambda g: jnp.concatenate([g, g]).reshape(1, lanes).astype(F32)
    tok = lambda col0: pl.BlockSpec((1, rq * grid_w, lanes), lambda b, p, i: (b, i, col0 + p))
    seq = lambda col0: pl.BlockSpec((1, t_all, lanes), lambda b, p, i: (b, 0, col0 + p))
    return pl.pallas_call(
        functools.partial(_na_kernel, n_ctx=n_ctx, grid_w=grid_w, kh=kh, khm=kh_max, rows=rows,
                          scale=HEAD_DIM ** -0.5, rq=rq),
        grid=(bsz, pairs, t_all // (rq * grid_w)),
        in_specs=[tok(0), seq(pairs), seq(2 * pairs), tok(3 * pairs),
                  pl.BlockSpec((2,) + tbl.shape[1:], lambda b, p, i: (p, 0, 0, 0)),
                  pl.BlockSpec((1, lanes), lambda b, p, i: (0, 0)),
                  pl.BlockSpec((1, lanes), lambda b, p, i: (0, 0))],
        out_specs=tok(0),
        out_shape=jax.ShapeDtypeStruct((bsz, t_all, e), F32),
        scratch_shapes=[pltpu.VMEM((t_all, lanes), BF16), pltpu.VMEM((t_all, lanes), BF16)],
        compiler_params=_cparams("arbitrary", "arbitrary", "arbitrary"),
        name="na_attention",
    )(qkvz, qkvz, qkvz, qkvz, tbl, g2(q_g), g2(k_g))


def _na_layer(stream, norm_g, scale, shift, gate, n_ctx, tm, tile_args, grid_w, w_in, q_g, k_g, rpb, w_out):
    bsz, t_all, d = stream.shape
    m = bsz * t_all
    s2 = stream.reshape(m, d)
    qkvz = _norm_mm(s2, norm_g, scale, shift, w_in, tm, *tile_args).reshape(bsz, t_all, w_in.shape[1])
    kh_max = (rpb.shape[1] + 1) // 2
    o = _na_attention(qkvz, rpb, q_g, k_g, n_ctx, grid_w, kh_max)
    return _mm_residual(o.reshape(m, -1), w_out, s2, gate, tm, *tile_args).reshape(bsz, t_all, d)


def _s5_matrices(lam_re, lam_im, log_dt, b_re, b_im, c_re, c_im):
    nt = S5_CHUNK
    g, p, cg = b_re.shape
    tau = jnp.arange(nt + 1, dtype=F32)[:, None, None]
    i_idx = np.arange(nt)
    kbig, bcat, ccat, a_pow = 0.0, [], [], []
    for s in range(2):
        lr, li = lam_re[s].astype(F32), lam_im[s].astype(F32)
        step = jnp.exp(log_dt[s].astype(F32))[:, None]
        mag = jnp.exp(lr * step)
        ar, ai = mag * jnp.cos(li * step), mag * jnp.sin(li * step)
        den = lr * lr + li * li
        qr = ((ar - 1.0) * lr + ai * li) / den
        qi = (ai * lr - (ar - 1.0) * li) / den
        bbr = qr[..., None] * b_re - qi[..., None] * b_im
        bbi = qr[..., None] * b_im + qi[..., None] * b_re
        pmag = jnp.exp(lr * step * tau)
        pr, pi = pmag * jnp.cos(li * step * tau), pmag * jnp.sin(li * step * tau)
        clr = c_re[None] * pr[:, :, None, :] - c_im[None] * pi[:, :, None, :]
        cli = c_re[None] * pi[:, :, None, :] + c_im[None] * pr[:, :, None, :]
        ker = (jnp.einsum("tgop,gpc->tgoc", clr[:nt], bbr, precision=HIGHEST)
               - jnp.einsum("tgop,gpc->tgoc", cli[:nt], bbi, precision=HIGHEST))
        lbr = pr[:, :, :, None] * bbr[None] - pi[:, :, :, None] * bbi[None]
        lbi = pr[:, :, :, None] * bbi[None] + pi[:, :, :, None] * bbr[None]
        lag = (i_idx[None, :] - i_idx[:, None]) if s == 0 else (i_idx[:, None] - i_idx[None, :])
        kb = jnp.where((lag >= 0)[:, :, None, None, None], ker[np.clip(lag, 0, nt - 1)], 0.0)
        kbig = kbig + kb.transpose(2, 0, 4, 1, 3).reshape(g, nt * cg, nt * cg)
        inj = (nt - 1 - i_idx) if s == 0 else i_idx
        bb = jnp.concatenate([lbr[inj], lbi[inj]], axis=2)
        bcat.append(bb.transpose(1, 0, 3, 2).reshape(g, nt * cg, 2 * p))
        out = (i_idx + 1) if s == 0 else (nt - i_idx)
        cc = jnp.concatenate([clr[out], -cli[out]], axis=3)
        ccat.append(cc.transpose(1, 3, 0, 2).reshape(g, 2 * p, nt * cg))
        a_pow.append((pr[nt], pi[nt]))
    w_in = jnp.concatenate([kbig] + bcat, axis=2)
    w_out = jnp.concatenate(ccat, axis=1)
    return w_in, w_out, a_pow


def _s5_state_kernel(bur_ref, bui_ref, ar_ref, ai_ref, xr_ref, xi_ref):
    nck = bur_ref.shape[0]
    ar = ar_ref[...]
    ai = ai_ref[...]

    def body(c, carry):
        xr, xi = carry
        xr_ref[c] = xr
        xi_ref[c] = xi
        return (ar * xr - ai * xi + bur_ref[c], ar * xi + ai * xr + bui_ref[c])

    zero = jnp.zeros(ar.shape, F32)
    lax.fori_loop(0, nck, body, (zero, zero))


def _s5_states(bur, bui, ar, ai):
    nck, rows, lanes = bur.shape
    tl = 512
    blk = pl.BlockSpec((nck, rows, tl), lambda j: (0, 0, j))
    coef = pl.BlockSpec((rows, tl), lambda j: (0, j))
    return pl.pallas_call(
        _s5_state_kernel,
        grid=(lanes // tl,),
        in_specs=[blk, blk, coef, coef],
        out_specs=[blk, blk],
        out_shape=[jax.ShapeDtypeStruct(bur.shape, F32)] * 2,
        compiler_params=_cparams("arbitrary"),
        name="s5_states",
    )(bur, bui, ar, ai)


def _s5_out_kernel(y_ref, u_ref, z_ref, d_ref, wg_ref, bg_ref, w_ref, res_ref, gate_ref, o_ref):
    y = jax.nn.gelu(y_ref[...] + d_ref[...] * u_ref[...])
    y = y * jax.nn.sigmoid(jnp.dot(y.astype(BF16), wg_ref[...], preferred_element_type=F32) + bg_ref[...])
    z = z_ref[...]
    o = (y * (z * jax.nn.sigmoid(z))).astype(BF16)
    o_ref[...] = res_ref[...] + gate_ref[0] * jnp.dot(o, w_ref[...], preferred_element_type=F32)


def _s5_out(y, uz, d_skip, w_glu, b_glu, w_out, res2, gate, tm, tiles_per_b, ctx_tiles, n_batch):
    m, e = y.shape
    d = w_out.shape[1]
    vec = pl.BlockSpec((1, e), lambda i: (0, 0))
    return pl.pallas_call(
        _s5_out_kernel,
        grid=(m // tm,),
        in_specs=[pl.BlockSpec((tm, e), lambda i: (i, 0)),
                  pl.BlockSpec((tm, e), lambda i: (i, 0)),
                  pl.BlockSpec((tm, e), lambda i: (i, 1)),
                  vec, pl.BlockSpec((e, e), lambda i: (0, 0)), vec,
                  pl.BlockSpec((e, d), lambda i: (0, 0)),
                  pl.BlockSpec((tm, d), lambda i: (i, 0)),
                  pl.BlockSpec((1, 1, d), lambda i: (_mod_row(i, tiles_per_b, ctx_tiles, n_batch), 0, 0))],
        out_specs=pl.BlockSpec((tm, d), lambda i: (i, 0)),
        out_shape=jax.ShapeDtypeStruct((m, d), F32),
        compiler_params=_cparams("arbitrary"),
        name="s5_out",
    )(y, uz, uz, d_skip.reshape(1, e).astype(F32), w_glu.astype(BF16), b_glu.reshape(1, e).astype(F32),
      w_out.astype(BF16), res2, gate)


def _s5_layer(stream, norm_g, scale, shift, gate, n_ctx, tm, tile_args, w_in, lam_re, lam_im, log_dt, b_re, b_im,
              c_re, c_im, d_skip, w_glu, b_glu, w_out):
    bsz, t_all, d = stream.shape
    m = bsz * t_all
    e = w_in.shape[1] // 2
    nt, cg = S5_CHUNK, S5_GROUP
    g = e // cg
    p = b_re.shape[1]
    nck, nck_ctx = t_all // nt, n_ctx // nt
    uz = _norm_mm(stream.reshape(m, d), norm_g, scale, shift, w_in, tm, *tile_args)
    u, z = uz[:, :e], uz[:, e:]
    k_in, k_out, a_pow = _s5_matrices(lam_re, lam_im, log_dt, b_re, b_im, c_re, c_im)
    u_flat = u.reshape(bsz, nck, nt, g, cg).transpose(3, 0, 1, 2, 4).reshape(g, bsz * nck, nt * cg)
    yb = _grouped_mm(u_flat, k_in)
    bu = yb[:, :, nt * cg:].reshape(g, bsz, nck, 2, 2, p)
    rev = np.concatenate([np.arange(nck_ctx)[::-1], np.arange(nck_ctx, nck)[::-1]])
    bu = bu.transpose(4, 2, 3, 1, 0, 5)
    bu = jnp.stack([bu[:, :, 0], bu[:, rev, 1]], axis=2).reshape(2, nck, 2 * bsz, g * p)
    coef = lambda a: jnp.broadcast_to(jnp.stack(a)[:, None], (2, bsz, g, p)).reshape(2 * bsz, g * p)
    xr, xi = _s5_states(bu[0], bu[1], coef([a_pow[0][0], a_pow[1][0]]), coef([a_pow[0][1], a_pow[1][1]]))
    inv = np.argsort(rev)

    def unorder(x):
        x = x.reshape(nck, 2, bsz, g, p)
        x = jnp.stack([x[:, 0], x[inv, 1]], axis=1)
        return x.transpose(3, 2, 0, 1, 4).reshape(g, bsz * nck, 2, p)

    xr, xi = unorder(xr), unorder(xi)
    x_cat = jnp.concatenate([xr[:, :, 0], xi[:, :, 0], xr[:, :, 1], xi[:, :, 1]], axis=-1)
    y_flat = _grouped_mm(x_cat, k_out, add=yb, add_col_block=0)
    y = y_flat.reshape(g, bsz, nck, nt, cg).transpose(1, 2, 3, 0, 4).reshape(m, e)
    return _s5_out(y, uz, d_skip, w_glu, b_glu, w_out, stream.reshape(m, d), gate, tm,
                   *tile_args).reshape(bsz, t_all, d)


def kernel(x, c, ctx, c_ctx, norm_g, w_mod, b_mod, rwkv_mu, rwkv_w_rkvg, rwkv_w0, rwkv_w1, rwkv_w2, rwkv_a0, rwkv_a1, rwkv_a2, rwkv_k_k, rwkv_k_a, rwkv_r_k, rwkv_ln_w, rwkv_ln_b, rwkv_w_out, na_w_in, na_q_g, na_k_g, na_rpb, na_w_out, s5_w_in, s5_lam_re, s5_lam_im, s5_log_dt, s5_b_re, s5_b_im, s5_c_re, s5_c_im, s5_d, s5_w_glu, s5_b_glu, s5_w_out):
    bsz, n_lat, d = x.shape
    n_ctx = ctx.shape[1]
    t_all = n_ctx + n_lat
    depth = norm_g.shape[0]
    grid_w = 64
    tm = _token_tile(n_ctx, t_all)
    tile_args = (t_all // tm, n_ctx // tm, bsz)
    stream = jnp.concatenate([ctx, x], axis=1).astype(F32)
    rows = 8 * ((bsz + 1 + 7) // 8)
    cc = jnp.zeros((rows, d), F32).at[:bsz].set(c.astype(F32)).at[bsz].set(c_ctx.astype(F32))
    for i in range(depth):
        kind, j = i % 3, i // 3
        mod = _modulation(cc, w_mod[i].astype(F32), b_mod[i].astype(F32))[:bsz + 1]
        shift, scale, gate = (mod[:, k * d:(k + 1) * d].reshape(bsz + 1, 1, d) for k in range(3))
        if kind == 0:
            stream = _rwkv_layer(stream, norm_g[i], scale, shift, gate, n_ctx, rwkv_mu[j], rwkv_w_rkvg[j], rwkv_w0[j],
                                 rwkv_w1[j], rwkv_w2[j], rwkv_a0[j], rwkv_a1[j], rwkv_a2[j], rwkv_k_k[j],
                                 rwkv_k_a[j], rwkv_r_k[j], rwkv_ln_w[j], rwkv_ln_b[j], rwkv_w_out[j])
        elif kind == 1:
            stream = _na_layer(stream, norm_g[i], scale, shift, gate, n_ctx, tm, tile_args, grid_w, na_w_in[j],
                               na_q_g[j], na_k_g[j], na_rpb[j], na_w_out[j])
        else:
            stream = _s5_layer(stream, norm_g[i], scale, shift, gate, n_ctx, tm, tile_args, s5_w_in[j],
                               s5_lam_re[j], s5_lam_im[j], s5_log_dt[j], s5_b_re[j], s5_b_im[j], s5_c_re[j],
                               s5_c_im[j], s5_d[j], s5_w_glu[j], s5_b_glu[j], s5_w_out[j])
    return stream[:, n_ctx:].astype(x.dtype)
```

```python
import functools
import math

import numpy as np
import jax
import jax.numpy as jnp
from jax import lax
from jax.experimental import pallas as pl
from jax.experimental.pallas import tpu as pltpu

F32 = jnp.float32
BF16 = jnp.bfloat16
NORM_EPS = 1e-6
RWKV_GN_EPS = 64e-5
HEAD_DIM = 64
RWKV_CHUNK = 64
RWKV_HEADS_PER_STEP = 8
NA_ROWS_PER_STEP = 2
S5_CHUNK = 16
S5_GROUP = 16
MASK_NEG = -1e30
VMEM_LIMIT = 48 * 1024 * 1024
HIGHEST = lax.Precision.HIGHEST


def _cparams(*sem):
    return pltpu.CompilerParams(dimension_semantics=sem, vmem_limit_bytes=VMEM_LIMIT)


def _token_tile(n_ctx, n_all, largest=256):
    for t in (256, 128, 64):
        if t <= largest and n_ctx % t == 0 and n_all % t == 0:
            return t
    raise ValueError("context / sequence lengths must be multiples of 64")


def _mod_row(i, tiles_per_b, ctx_tiles, n_batch):
    return jnp.where(i % tiles_per_b < ctx_tiles, n_batch, i // tiles_per_b)


def _mod_kernel(c_ref, w_ref, b_ref, o_ref):
    c = c_ref[...]
    s = c * jax.nn.sigmoid(c)
    o_ref[...] = jnp.dot(s, w_ref[...], precision=HIGHEST, preferred_element_type=F32) + b_ref[...]


def _modulation(cc, wm, bm):
    rows, d = cc.shape
    n = wm.shape[1]
    tn = 512
    return pl.pallas_call(
        _mod_kernel,
        grid=(n // tn,),
        in_specs=[pl.BlockSpec((rows, d), lambda j: (0, 0)),
                  pl.BlockSpec((d, tn), lambda j: (0, j)),
                  pl.BlockSpec((1, tn), lambda j: (0, j))],
        out_specs=pl.BlockSpec((rows, tn), lambda j: (0, j)),
        out_shape=jax.ShapeDtypeStruct((rows, n), F32),
        compiler_params=_cparams("arbitrary"),
        name="modulation",
    )(cc, wm, bm.reshape(1, n))


def _norm_mm_kernel(x_ref, g_ref, sc_ref, sh_ref, w_ref, o_ref, h_ref):
    @pl.when(pl.program_id(1) == 0)
    def _():
        x = x_ref[...]
        ms = jnp.mean(x * x, axis=-1, keepdims=True)
        y = x * lax.rsqrt(ms + NORM_EPS) * g_ref[...]
        h_ref[...] = (y * (1.0 + sc_ref[0]) + sh_ref[0]).astype(BF16)

    o_ref[...] = jnp.dot(h_ref[...], w_ref[...], preferred_element_type=F32)


def _norm_mm(x2, g, scale, shift, w, tm, tiles_per_b, ctx_tiles, n_batch):
    m, d = x2.shape
    n = w.shape[1]
    tn = n if n <= 2048 else 2048
    row = lambda i, j: (_mod_row(i, tiles_per_b, ctx_tiles, n_batch), 0, 0)
    return pl.pallas_call(
        _norm_mm_kernel,
        grid=(m // tm, n // tn),
        in_specs=[pl.BlockSpec((tm, d), lambda i, j: (i, 0)),
                  pl.BlockSpec((1, d), lambda i, j: (0, 0)),
                  pl.BlockSpec((1, 1, d), row),
                  pl.BlockSpec((1, 1, d), row),
                  pl.BlockSpec((d, tn), lambda i, j: (0, j))],
        out_specs=pl.BlockSpec((tm, tn), lambda i, j: (i, j)),
        out_shape=jax.ShapeDtypeStruct((m, n), F32),
        scratch_shapes=[pltpu.VMEM((tm, d), BF16)],
        compiler_params=_cparams("arbitrary", "arbitrary"),
        name="norm_matmul",
    )(x2, g.reshape(1, d), scale, shift, w.astype(BF16))


def _mm_res_kernel(x_ref, w_ref, res_ref, gate_ref, o_ref):
    acc = jnp.dot(x_ref[...].astype(BF16), w_ref[...], preferred_element_type=F32)
    o_ref[...] = res_ref[...] + gate_ref[0] * acc


def _mm_residual(x2, w, res2, gate, tm, tiles_per_b, ctx_tiles, n_batch):
    m, k = x2.shape
    n = w.shape[1]
    return pl.pallas_call(
        _mm_res_kernel,
        grid=(m // tm,),
        in_specs=[pl.BlockSpec((tm, k), lambda i: (i, 0)),
                  pl.BlockSpec((k, n), lambda i: (0, 0)),
                  pl.BlockSpec((tm, n), lambda i: (i, 0)),
                  pl.BlockSpec((1, 1, n), lambda i: (_mod_row(i, tiles_per_b, ctx_tiles, n_batch), 0, 0))],
        out_specs=pl.BlockSpec((tm, n), lambda i: (i, 0)),
        out_shape=jax.ShapeDtypeStruct((m, n), F32),
        compiler_params=_cparams("arbitrary"),
        name="matmul_residual",
    )(x2, w.astype(BF16), res2, gate)


def _dot_nt(a, b):
    return lax.dot_general(a, b, (((1,), (1,)), ((), ())), preferred_element_type=F32)


def _dot_tn(a, b):
    return lax.dot_general(a, b, (((0,), (0,)), ((), ())), preferred_element_type=F32)


def _dot(a, b):
    return jnp.dot(a, b, preferred_element_type=F32)


def _head_sum(x, ones_bd):
    lanes = ones_bd.shape[0]
    return jnp.concatenate(
        [jnp.dot(x[:, p * lanes:(p + 1) * lanes].astype(BF16), ones_bd, preferred_element_type=F32)
         for p in range(x.shape[1] // lanes)], axis=1)


def _same_head(lanes):
    return (lax.broadcasted_iota(jnp.int32, (lanes, lanes), 0) // HEAD_DIM
            == lax.broadcasted_iota(jnp.int32, (lanes, lanes), 1) // HEAD_DIM)


def _rwkv_proj_kernel(x_ref, xp_ref, xn_ref, g_ref, sc_ref, sh_ref, mu_ref, w_ref, w1_ref, w2_ref, a1_ref, a2_ref,
                      w0_ref, a0_ref, kk_ref, ka_ref, rk_ref,
                      r_out, v_out, kk_out, g_out, bonus_out, lw_out, kd_out, bd_out, *, tm, tiles_per_b, ctx_tiles):
    i = pl.program_id(0)
    j = i % tiles_per_b
    gain = g_ref[...]
    sc = 1.0 + sc_ref[0]
    sh = sh_ref[0]

    def norm(x):
        ms = jnp.mean(x * x, axis=-1, keepdims=True)
        return x * lax.rsqrt(ms + NORM_EPS) * gain * sc + sh

    h = norm(x_ref[...])
    has_prev = jnp.logical_and(j != 0, j != ctx_tiles)
    has_next = jnp.logical_and(j != ctx_tiles - 1, j != tiles_per_b - 1)
    h_prev = jnp.where(has_prev, norm(xp_ref[...])[7:8], 0.0)
    h_next = jnp.where(has_next, norm(xn_ref[...])[0:1], 0.0)
    row = lax.broadcasted_iota(jnp.int32, h.shape, 0)
    prev = jnp.where(row == 0, h_prev, pltpu.roll(h, 1, axis=0))
    nxt = jnp.where(row == tm - 1, h_next, pltpu.roll(h, tm - 1, axis=0))
    xx = 0.5 * (prev + nxt) - h
    mix = lambda n: (h + xx * mu_ref[n:n + 1, :]).astype(BF16)

    r = jnp.dot(mix(0), w_ref[0], preferred_element_type=F32)
    k = jnp.dot(mix(1), w_ref[1], preferred_element_type=F32)
    v = jnp.dot(mix(2), w_ref[2], preferred_element_type=F32)
    g_out[...] = jnp.dot(mix(3), w_ref[3], preferred_element_type=F32)
    dec = jnp.dot(jnp.tanh(jnp.dot(mix(4), w1_ref[...], preferred_element_type=F32)).astype(BF16), w2_ref[...],
                  preferred_element_type=F32)
    icl = jnp.dot(jnp.dot(mix(5), a1_ref[...], preferred_element_type=F32).astype(BF16), a2_ref[...],
                  preferred_element_type=F32)
    e = r.shape[1]
    ones_bd = _same_head(2 * HEAD_DIM).astype(BF16)
    kkf = k * kk_ref[...]
    kk = kkf / jnp.maximum(jnp.sqrt(_head_sum(kkf * kkf, ones_bd)), 1e-12)
    r_out[...] = r
    v_out[...] = v
    kk_out[...] = kk
    bonus_out[...] = _head_sum(r * k * rk_ref[...], ones_bd) * v
    for s in range(2):
        lw_out[s] = -math.exp(-0.5) * jax.nn.sigmoid(w0_ref[s:s + 1, :] + dec[:, s * e:(s + 1) * e])
        a = jax.nn.sigmoid(a0_ref[s:s + 1, :] + icl[:, s * e:(s + 1) * e])
        kd_out[s] = k * (1.0 + (a - 1.0) * ka_ref[...])
        bd_out[s] = kk * a


def _rwkv_proj(stream2, norm_g, scale, shift, mu, w_rkvg, w0, w1, w2, a0, a1, a2, k_k, k_a, r_k, tm, tiles_per_b,
               ctx_tiles, n_batch):
    m, d = stream2.shape
    e = w_rkvg.shape[-1]
    lr = w1.shape[-1]
    nblk = m // 8
    zeros = jnp.zeros((lr, e), F32)
    cat = lambda w: jnp.concatenate([w[0], w[1]], axis=1).astype(BF16)
    bdiag = lambda w: jnp.concatenate([jnp.concatenate([w[0], zeros], axis=1),
                                       jnp.concatenate([zeros, w[1]], axis=1)], axis=0).astype(BF16)
    row = lambda i: (_mod_row(i, tiles_per_b, ctx_tiles, n_batch), 0, 0)
    full = lambda shape: pl.BlockSpec(shape, lambda i: (0,) * len(shape))
    tok = pl.BlockSpec((tm, e), lambda i: (i, 0))
    tok2 = pl.BlockSpec((2, tm, e), lambda i: (0, i, 0))
    vec = lambda a: a.reshape(1, e).astype(F32)
    return pl.pallas_call(
        functools.partial(_rwkv_proj_kernel, tm=tm, tiles_per_b=tiles_per_b, ctx_tiles=ctx_tiles),
        grid=(m // tm,),
        in_specs=[pl.BlockSpec((tm, d), lambda i: (i, 0)),
                  pl.BlockSpec((8, d), lambda i: (jnp.maximum(i * (tm // 8) - 1, 0), 0)),
                  pl.BlockSpec((8, d), lambda i: (jnp.minimum((i + 1) * (tm // 8), nblk - 1), 0)),
                  full((1, d)), pl.BlockSpec((1, 1, d), row), pl.BlockSpec((1, 1, d), row),
                  full(mu.shape), full(w_rkvg.shape), full((d, 2 * lr)), full((2 * lr, 2 * e)),
                  full((d, 2 * lr)), full((2 * lr, 2 * e)), full((2, e)), full((2, e)),
                  full((1, e)), full((1, e)), full((1, e))],
        out_specs=[tok, tok, tok, tok, tok, tok2, tok2, tok2],
        out_shape=[jax.ShapeDtypeStruct((m, e), F32)] * 5 + [jax.ShapeDtypeStruct((2, m, e), F32)] * 3,
        compiler_params=_cparams("arbitrary"),
        name="rwkv_proj",
    )(stream2, stream2, stream2, norm_g.reshape(1, d), scale, shift, mu.astype(F32), w_rkvg.astype(BF16),
      cat(w1), bdiag(w2), cat(a1), bdiag(a2), w0.astype(F32), a0.astype(F32), vec(k_k), vec(k_a), vec(r_k))


def _cumsum_rows(x, reverse):
    c = x.shape[0]
    row = lax.broadcasted_iota(jnp.int32, x.shape, 0)
    shift = 1
    while shift < c:
        if reverse:
            x = x + jnp.where(row < c - shift, pltpu.roll(x, c - shift, axis=0), 0.0)
        else:
            x = x + jnp.where(row >= shift, pltpu.roll(x, shift, axis=0), 0.0)
        shift *= 2
    return x


def _rwkv_chunk_streams(streams):
    c = streams[0][0].shape[0]
    hd = HEAD_DIM
    lanes = 2 * hd
    row = lax.broadcasted_iota(jnp.int32, (c, c), 0)
    col = lax.broadcasted_iota(jnp.int32, (c, c), 1)
    eye = (row == col).astype(F32)
    lane_head = lax.broadcasted_iota(jnp.int32, (1, lanes), 1) // hd
    first = lane_head == 0
    same = _same_head(lanes)
    pick = lambda a, p: a[:, p * lanes:(p + 1) * lanes]
    both = lambda mats, x: jnp.where(first, _dot(mats[0].astype(BF16), x), _dot(mats[1].astype(BF16), x))
    zero = jnp.zeros((), BF16)

    units, incl, strict = [], {}, {}
    lhs, k_h, b_h, k_p, b_p, vb, lp_tot, s_bd = {}, {}, {}, {}, {}, {}, {}, {}
    for si, (r, v, kk, lw, kd, bd, states, reverse) in enumerate(streams):
        incl[si] = (col >= row) if reverse else (col <= row)
        strict[si] = (col > row) if reverse else (col < row)
        lp = _cumsum_rows(lw, reverse)
        tot = jnp.sum(lw, axis=0, keepdims=True)
        lhs_f = jnp.concatenate([kk * jnp.exp(lp - lw), r * jnp.exp(lp)], axis=0).astype(BF16)
        e_ninc = jnp.exp(-lp)
        e_rem = jnp.exp(tot - lp)
        full = dict(lhs=lhs_f, k_h=(kd * e_ninc).astype(BF16), b_h=(bd * e_ninc).astype(BF16),
                    k_p=(kd * e_rem).astype(BF16), b_p=(bd * e_rem).astype(BF16), vb=v.astype(BF16), tot=tot)
        for p in range(len(states)):
            u_ = (si, p)
            units.append(u_)
            lhs[u_], k_h[u_], b_h[u_] = pick(full["lhs"], p), pick(full["k_h"], p), pick(full["b_h"], p)
            k_p[u_], b_p[u_], vb[u_] = pick(full["k_p"], p), pick(full["b_p"], p), pick(full["vb"], p)
            lp_tot[u_], s_bd[u_] = pick(full["tot"], p), states[p]
    heads = [(u_, hh) for u_ in units for hh in range(2)]

    a_k = {(u_, hh): _dot_nt(lhs[u_], jnp.where(lane_head == hh, k_h[u_], zero)) for u_, hh in heads}
    a_b = {(u_, hh): _dot_nt(lhs[u_], jnp.where(lane_head == hh, b_h[u_], zero)) for u_, hh in heads}
    ls = {u_: _dot_nt(lhs[u_], s_bd[u_].astype(BF16)) for u_ in units}
    n_mat = {(u_, hh): jnp.where(strict[u_[0]], a_b[u_, hh][:c], 0.0) for u_, hh in heads}
    z = {u_: ls[u_][:c] + both([jnp.where(strict[u_[0]], a_k[u_, hh][:c], 0.0) for hh in range(2)], vb[u_])
         for u_ in units}
    y0 = {u_: ls[u_][c:] + both([jnp.where(incl[u_[0]], a_k[u_, hh][c:], 0.0) for hh in range(2)], vb[u_])
          for u_ in units}

    t_inv = {h_: eye - n_mat[h_] for h_ in heads}
    pw = {h_: n_mat[h_].astype(BF16) for h_ in heads}
    for _ in range(int(math.log2(c)) - 1):
        pw = {h_: _dot(pw[h_], pw[h_]).astype(BF16) for h_ in heads}
        t_inv = {h_: t_inv[h_] + _dot(t_inv[h_].astype(BF16), pw[h_]) for h_ in heads}

    u = {u_: both([t_inv[u_, hh] for hh in range(2)], z[u_].astype(BF16)) for u_ in units}
    y = {u_: y0[u_] - both([jnp.where(incl[u_[0]], a_b[u_, hh][c:], 0.0) for hh in range(2)], u[u_].astype(BF16))
         for u_ in units}
    s1 = {}
    for u_ in units:
        upd = _dot_tn(jnp.concatenate([vb[u_], -u[u_].astype(BF16)], axis=0),
                      jnp.concatenate([k_p[u_], b_p[u_]], axis=0))
        s1[u_] = s_bd[u_] * jnp.exp(lp_tot[u_]) + jnp.where(same, upd, 0.0)
    return [(jnp.concatenate([y[si, p] for p in range(len(st[6]))], axis=1),
             [s1[si, p] for p in range(len(st[6]))]) for si, st in enumerate(streams)]


def _rwkv_scan_kernel(rf_ref, vf_ref, kkf_ref, lwf_ref, kdf_ref, bdf_ref,
                      rb_ref, vb_ref, kkb_ref, lwb_ref, kdb_ref, bdb_ref, yf_ref, yb_ref, s_ref):
    @pl.when(pl.program_id(2) == 0)
    def _():
        s_ref[...] = jnp.zeros_like(s_ref)

    npairs = s_ref.shape[0] // 2
    fwd = (rf_ref[0], vf_ref[0], kkf_ref[0], lwf_ref[0, 0], kdf_ref[0, 0], bdf_ref[0, 0],
           [s_ref[p] for p in range(npairs)], False)
    bwd = (rb_ref[0], vb_ref[0], kkb_ref[0], lwb_ref[0, 0], kdb_ref[0, 0], bdb_ref[0, 0],
           [s_ref[npairs + p] for p in range(npairs)], True)
    (y_f, s_f), (y_b, s_b) = _rwkv_chunk_streams([fwd, bwd])
    yf_ref[0] = y_f
    yb_ref[0] = y_b
    for p in range(npairs):
        s_ref[p] = s_f[p]
        s_ref[npairs + p] = s_b[p]


def _rwkv_scan(r, v, kk, lw, kd, bd, n_ctx):
    bsz, t, e = r.shape
    c = RWKV_CHUNK
    width = RWKV_HEADS_PER_STEP * HEAD_DIM
    nc, nc_ctx = t // c, n_ctx // c
    rev = lambda ci: jnp.where(ci < nc_ctx, nc_ctx - 1 - ci, nc + nc_ctx - 1 - ci)
    tok_f = pl.BlockSpec((1, c, width), lambda b, hg, ci: (b, ci, hg))
    tok_b = pl.BlockSpec((1, c, width), lambda b, hg, ci: (b, rev(ci), hg))
    dir_f = pl.BlockSpec((1, 1, c, width), lambda b, hg, ci: (0, b, ci, hg))
    dir_b = pl.BlockSpec((1, 1, c, width), lambda b, hg, ci: (1, b, rev(ci), hg))
    return pl.pallas_call(
        _rwkv_scan_kernel,
        grid=(bsz, e // width, nc),
        in_specs=[tok_f, tok_f, tok_f, dir_f, dir_f, dir_f, tok_b, tok_b, tok_b, dir_b, dir_b, dir_b],
        out_specs=[tok_f, tok_b],
        out_shape=[jax.ShapeDtypeStruct((bsz, t, e), F32)] * 2,
        scratch_shapes=[pltpu.VMEM((2 * width // (2 * HEAD_DIM), 2 * HEAD_DIM, 2 * HEAD_DIM), F32)],
        compiler_params=_cparams("arbitrary", "arbitrary", "arbitrary"),
        name="rwkv_scan",
    )(r, v, kk, lw, kd, bd, r, v, kk, lw, kd, bd)


def _rwkv_out_kernel(yf_ref, yb_ref, bonus_ref, g_ref, lnw_ref, lnb_ref, w_ref, res_ref, gate_ref, o_ref):
    ones_bd = _same_head(2 * HEAD_DIM).astype(BF16)
    y = yf_ref[...] + yb_ref[...]
    mean = _head_sum(y, ones_bd) * (1.0 / HEAD_DIM)
    yc = y - mean
    var = _head_sum(yc * yc, ones_bd) * (1.0 / HEAD_DIM)
    yn = yc * lax.rsqrt(var + RWKV_GN_EPS) * lnw_ref[...] + lnb_ref[...]
    g = g_ref[...]
    o = ((yn + bonus_ref[...]) * (g * jax.nn.sigmoid(g))).astype(BF16)
    o_ref[...] = res_ref[...] + gate_ref[0] * jnp.dot(o, w_ref[...], preferred_element_type=F32)


def _rwkv_out(y_f, y_b, bonus, g, ln_w, ln_b, w_out, res2, gate, tm, tiles_per_b, ctx_tiles, n_batch):
    m, e = bonus.shape
    d = w_out.shape[1]
    tok = pl.BlockSpec((tm, e), lambda i: (i, 0))
    vec = pl.BlockSpec((1, e), lambda i: (0, 0))
    return pl.pallas_call(
        _rwkv_out_kernel,
        grid=(m // tm,),
        in_specs=[tok, tok, tok, tok, vec, vec,
                  pl.BlockSpec((e, d), lambda i: (0, 0)),
                  pl.BlockSpec((tm, d), lambda i: (i, 0)),
                  pl.BlockSpec((1, 1, d), lambda i: (_mod_row(i, tiles_per_b, ctx_tiles, n_batch), 0, 0))],
        out_specs=pl.BlockSpec((tm, d), lambda i: (i, 0)),
        out_shape=jax.ShapeDtypeStruct((m, d), F32),
        compiler_params=_cparams("arbitrary"),
        name="rwkv_out",
    )(y_f, y_b, bonus, g, ln_w.reshape(1, e).astype(F32), ln_b.reshape(1, e).astype(F32), w_out.astype(BF16),
      res2, gate)


def _rwkv_layer(stream, norm_g, scale, shift, gate, n_ctx, mu, w_rkvg, w0, w1, w2, a0, a1, a2, k_k, k_a, r_k,
                ln_w, ln_b, w_out):
    bsz, t_all, d = stream.shape
    e = w_rkvg.shape[-1]
    m = bsz * t_all
    tm = _token_tile(n_ctx, t_all, 128)
    tile_args = (t_all // tm, n_ctx // tm, bsz)
    s2 = stream.reshape(m, d)
    r, v, kk, g, bonus, lw, kd, bd = _rwkv_proj(s2, norm_g, scale, shift, mu, w_rkvg, w0, w1, w2, a0, a1, a2,
                                                k_k, k_a, r_k, tm, *tile_args)
    b3 = lambda a: a.reshape(bsz, t_all, e)
    b4 = lambda a: a.reshape(2, bsz, t_all, e)
    y_f, y_b = _rwkv_scan(b3(r), b3(v), b3(kk), b4(lw), b4(kd), b4(bd), n_ctx)
    return _rwkv_out(y_f.reshape(m, e), y_b.reshape(m, e), bonus, g, ln_w, ln_b, w_out, s2, gate, tm,
                     *tile_args).reshape(bsz, t_all, d)


def _na_kernel(q_ref, k_ref, v_ref, z_ref, bias_ref, qg_ref, kg_ref, o_ref, kn_ref, vb_ref, *,
               n_ctx, grid_w, kh, khm, rows, scale, rq):
    step = pl.program_id(2)
    ctx_steps = n_ctx // (grid_w * rq)
    hd = HEAD_DIM
    lanes = 2 * hd
    same_head = _same_head(lanes).astype(BF16)
    lane_head = lax.broadcasted_iota(jnp.int32, (1, lanes), 1) // hd
    first = lane_head == 0

    def head_rms(x, g):
        ss = jnp.dot((x * x).astype(BF16), same_head, preferred_element_type=F32)
        return x * lax.rsqrt(ss * (1.0 / hd) + NORM_EPS) * g

    @pl.when(step == 0)
    def _():
        kn_ref[...] = head_rms(k_ref[0], kg_ref[...]).astype(BF16)
        vb_ref[...] = v_ref[0].astype(BF16)

    qn = head_rms(q_ref[0], qg_ref[...])
    z = z_ref[0]
    zgate = z * jax.nn.sigmoid(z)
    k_ctx = kn_ref[0:n_ctx, :]
    v_ctx = vb_ref[0:n_ctx, :]
    chains = [(s, hh) for s in range(rq) for hh in range(2)]
    qh = {(s, hh): jnp.where(lane_head == hh, qn[s * grid_w:(s + 1) * grid_w], 0.0).astype(BF16)
          for s, hh in chains}

    def attend(bands):
        s_c = {ch: _dot_nt(qh[ch], k_ctx) * scale for ch in chains}
        m = {ch: jnp.max(s_c[ch], axis=-1, keepdims=True) for ch in chains}
        if bands is not None:
            s_n = {(s, hh): _dot_nt(qh[s, hh], bands[s][0]) * scale + bands[s][2](hh) for s, hh in chains}
            m = {ch: jnp.maximum(m[ch], jnp.max(s_n[ch], axis=-1, keepdims=True)) for ch in chains}
            p_n = {ch: jnp.exp(s_n[ch] - m[ch]) for ch in chains}
        p_c = {ch: jnp.exp(s_c[ch] - m[ch]) for ch in chains}
        den = {ch: jnp.sum(p_c[ch], axis=-1, keepdims=True) for ch in chains}
        acc = {ch: _dot(p_c[ch].astype(BF16), v_ctx) for ch in chains}
        if bands is not None:
            den = {ch: den[ch] + jnp.sum(p_n[ch], axis=-1, keepdims=True) for ch in chains}
            acc = {(s, hh): acc[s, hh] + _dot(p_n[s, hh].astype(BF16), bands[s][1]) for s, hh in chains}
        out = [jnp.where(first, acc[s, 0] / den[s, 0], acc[s, 1] / den[s, 1]) for s in range(rq)]
        return jnp.concatenate(out, axis=0) * zgate

    @pl.when(step < ctx_steps)
    def _():
        o_ref[0] = attend(None)

    @pl.when(step >= ctx_steps)
    def _():
        bands = []
        for s in range(rq):
            i = (step - ctx_steps) * rq + s
            r0 = jnp.clip(i - kh // 2, 0, rows - kh)
            start = pl.multiple_of(n_ctx + r0 * grid_w, grid_w)
            base = khm - 1 - (i - r0)
            bias_of = functools.partial(
                lambda hh, base: jnp.concatenate([bias_ref[hh, base + 2 * q] for q in range(kh // 2)], axis=1),
                base=base)
            bands.append((kn_ref[pl.ds(start, kh * grid_w), :], vb_ref[pl.ds(start, kh * grid_w), :], bias_of))
        o_ref[0] = attend(bands)


def _na_bias_table(rpb, grid_w):
    kw = (rpb.shape[2] + 1) // 2
    j = np.arange(grid_w)[:, None]
    c = np.arange(grid_w)[None, :]
    c0 = np.clip(j - kw // 2, 0, grid_w - kw)
    valid = (c >= c0) & (c < c0 + kw)
    onehot = ((c - j + kw - 1)[None] == np.arange(2 * kw - 1)[:, None, None]) & valid[None]
    tiles = jnp.einsum("hab,bjc->hajc", rpb.astype(F32), jnp.asarray(onehot, F32), precision=HIGHEST)
    tiles = tiles + jnp.asarray(np.where(valid, 0.0, MASK_NEG), F32)
    return jnp.concatenate([tiles[:, :-1], tiles[:, 1:]], axis=-1)


def _na_attention(qkvz, rpb, q_g, k_g, n_ctx, grid_w, kh_max):
    bsz, t_all, e4 = qkvz.shape
    e = e4 // 4
    pairs = e // (2 * HEAD_DIM)
    rows = (t_all - n_ctx) // grid_w
    kh = min(kh_max, rows)
    assert kh % 2 == 0 and kh <= kh_max
    tbl = _na_bias_table(rpb, grid_w)
    ctx_tiles = n_ctx // grid_w
    rq = NA_ROWS_PER_STEP if (ctx_tiles % NA_ROWS_PER_STEP == 0 and rows % NA_ROWS_PER_STEP == 0) else 1
    lanes = 2 * HEAD_DIM
    g2 = lambda g: jnp.concatenate([g, g]).reshape(1, lanes).astype(F32)
    tok = lambda col0: pl.BlockSpec((1, rq * grid_w, lanes), lambda b, p, i: (b, i, col0 + p))
    seq = lambda col0: pl.BlockSpec((1, t_all, lanes), lambda b, p, i: (b, 0, col0 + p))
    return pl.pallas_call(
        functools.partial(_na_kernel, n_ctx=n_ctx, grid_w=grid_w, kh=kh, khm=kh_max, rows=rows,
                          scale=HEAD_DIM ** -0.5, rq=rq),
        grid=(bsz, pairs, t_all // (rq * grid_w)),
        in_specs=[tok(0), seq(pairs), seq(2 * pairs), tok(3 * pairs),
                  pl.BlockSpec((2,) + tbl.shape[1:], lambda b, p, i: (p, 0, 0, 0)),
                  pl.BlockSpec((1, lanes), lambda b, p, i: (0, 0)),
                  pl.BlockSpec((1, lanes), lambda b, p, i: (0, 0))],
        out_specs=tok(0),
        out_shape=jax.ShapeDtypeStruct((bsz, t_all, e), F32),
        scratch_shapes=[pltpu.VMEM((t_all, lanes), BF16), pltpu.VMEM((t_all, lanes), BF16)],
        compiler_params=_cparams("arbitrary", "arbitrary", "arbitrary"),
        name="na_attention",
    )(qkvz, qkvz, qkvz, qkvz, tbl, g2(q_g), g2(k_g))


def _na_layer(stream, norm_g, scale, shift, gate, n_ctx, tm, tile_args, grid_w, w_in, q_g, k_g, rpb, w_out):
    bsz, t_all, d = stream.shape
    m = bsz * t_all
    s2 = stream.reshape(m, d)
    qkvz = _norm_mm(s2, norm_g, scale, shift, w_in, tm, *tile_args).reshape(bsz, t_all, w_in.shape[1])
    kh_max = (rpb.shape[1] + 1) // 2
    o = _na_attention(qkvz, rpb, q_g, k_g, n_ctx, grid_w, kh_max)
    return _mm_residual(o.reshape(m, -1), w_out, s2, gate, tm, *tile_args).reshape(bsz, t_all, d)


def _s5_matrices(lam_re, lam_im, log_dt, b_re, b_im, c_re, c_im):
    nt = S5_CHUNK
    g, p, cg = b_re.shape
    tau = jnp.arange(nt + 1, dtype=F32)[:, None, None]
    i_idx = np.arange(nt)
    kbig, b_re_cols, b_im_cols, c_re_rows, c_im_rows, a_re, a_im = 0.0, [], [], [], [], [], []
    for s in range(2):
        lr, li = lam_re[s].astype(F32), lam_im[s].astype(F32)
        step = jnp.exp(log_dt[s].astype(F32))[:, None]
        mag = jnp.exp(lr * step)
        ar, ai = mag * jnp.cos(li * step), mag * jnp.sin(li * step)
        den = lr * lr + li * li
        qr = ((ar - 1.0) * lr + ai * li) / den
        qi = (ai * lr - (ar - 1.0) * li) / den
        bbr = qr[..., None] * b_re - qi[..., None] * b_im
        bbi = qr[..., None] * b_im + qi[..., None] * b_re
        pmag = jnp.exp(lr * step * tau)
        pr, pi = pmag * jnp.cos(li * step * tau), pmag * jnp.sin(li * step * tau)
        clr = c_re[None] * pr[:, :, None, :] - c_im[None] * pi[:, :, None, :]
        cli = c_re[None] * pi[:, :, None, :] + c_im[None] * pr[:, :, None, :]
        ker = (jnp.einsum("tgop,gpc->tgoc", clr[:nt], bbr, precision=HIGHEST)
               - jnp.einsum("tgop,gpc->tgoc", cli[:nt], bbi, precision=HIGHEST))
        lbr = pr[:, :, :, None] * bbr[None] - pi[:, :, :, None] * bbi[None]
        lbi = pr[:, :, :, None] * bbi[None] + pi[:, :, :, None] * bbr[None]
        lag = (i_idx[None, :] - i_idx[:, None]) if s == 0 else (i_idx[:, None] - i_idx[None, :])
        kb = jnp.where((lag >= 0)[:, :, None, None, None], ker[np.clip(lag, 0, nt - 1)], 0.0)
        kbig = kbig + kb.transpose(2, 0, 4, 1, 3).reshape(g, nt * cg, nt * cg)
        inj = (nt - 1 - i_idx) if s == 0 else i_idx
        flat_in = lambda a: a[inj].transpose(1, 0, 3, 2).reshape(g, nt * cg, p)
        b_re_cols.append(flat_in(lbr))
        b_im_cols.append(flat_in(lbi))
        out = (i_idx + 1) if s == 0 else (nt - i_idx)
        flat_out = lambda a: a[out].transpose(1, 3, 0, 2).reshape(g, p, nt * cg)
        c_re_rows.append(flat_out(clr))
        c_im_rows.append(flat_out(-cli))
        a_re.append(pr[nt])
        a_im.append(pi[nt])
    w_in = jnp.concatenate([kbig] + b_re_cols + b_im_cols, axis=2)
    w_out = jnp.concatenate(c_re_rows + c_im_rows, axis=1)
    coef = lambda a: jnp.concatenate(a, axis=-1).reshape(1, g * 2 * p)
    return w_in, w_out, coef(a_re), coef(a_im)


def _s5_core_kernel(u_ref, win_ref, wout_ref, are_ref, aim_ref, y_ref,
                    x_ref, yi_ref, bre_ref, bim_ref, xa_re, xa_im, xb_re, xb_im, *, nck, nck_ctx):
    nt, cg = S5_CHUNK, S5_GROUP
    gpb = u_ref.shape[1] // cg
    feat = nt * cg
    half = feat // 2
    pst = bre_ref.shape[1] // gpb
    lane_blk = lax.broadcasted_iota(jnp.int32, (1, u_ref.shape[1]), 1) // cg

    def regroup(piece_of, key_of):
        acc = None
        for n in range(gpb):
            src, shift = piece_of(n)
            rolled = pltpu.roll(src, shift, axis=1) if shift else src
            acc = rolled if acc is None else jnp.where(lane_blk == key_of(n), rolled, acc)
        return acc

    for hf in range(nt // gpb):
        for g8 in range(gpb):
            tile = regroup(lambda jj: (u_ref[pl.ds(gpb * hf + jj, nck, stride=nt), :], (cg * (jj - g8)) % (gpb * cg)),
                           lambda jj: jj)
            x_ref[:, g8 * feat + hf * half:g8 * feat + (hf + 1) * half] = tile.astype(BF16)

    for g8 in range(gpb):
        yb = jnp.dot(x_ref[:, g8 * feat:(g8 + 1) * feat], win_ref[g8], preferred_element_type=F32)
        yi_ref[:, g8 * feat:(g8 + 1) * feat] = yb[:, :feat]
        bre_ref[:, g8 * pst:(g8 + 1) * pst] = yb[:, feat:feat + pst]
        bim_ref[:, g8 * pst:(g8 + 1) * pst] = yb[:, feat + pst:feat + 2 * pst]

    dir0 = (lax.broadcasted_iota(jnp.int32, (1, bre_ref.shape[1]), 1) % pst) < pst // 2
    a_re = are_ref[...]
    a_im = aim_ref[...]

    def body(k, carry):
        xr, xi = carry
        rk = jnp.where(k < nck_ctx, nck_ctx - 1 - k, nck + nck_ctx - 1 - k)
        xa_re[pl.ds(k, 1), :] = xr
        xa_im[pl.ds(k, 1), :] = xi
        xb_re[pl.ds(rk, 1), :] = xr
        xb_im[pl.ds(rk, 1), :] = xi
        b_r = jnp.where(dir0, bre_ref[pl.ds(k, 1), :], bre_ref[pl.ds(rk, 1), :])
        b_i = jnp.where(dir0, bim_ref[pl.ds(k, 1), :], bim_ref[pl.ds(rk, 1), :])
        return a_re * xr - a_im * xi + b_r, a_re * xi + a_im * xr + b_i

    zero = jnp.zeros(a_re.shape, F32)
    lax.fori_loop(0, nck, body, (zero, zero))

    d0 = dir0[:, :pst]
    for g8 in range(gpb):
        blk = slice(g8 * pst, (g8 + 1) * pst)
        state = jnp.concatenate([jnp.where(d0, xa_re[:, blk], xb_re[:, blk]),
                                 jnp.where(d0, xa_im[:, blk], xb_im[:, blk])], axis=1).astype(BF16)
        yi_ref[:, g8 * feat:(g8 + 1) * feat] += jnp.dot(state, wout_ref[g8], preferred_element_type=F32)

    for i in range(nt):
        hf, ii = divmod(i, gpb)
        tile = regroup(lambda g8: (yi_ref[:, g8 * feat + hf * half:g8 * feat + (hf + 1) * half],
                                   (cg * (g8 - ii)) % (gpb * cg)),
                       lambda g8: g8)
        y_ref[pl.ds(i, nck, stride=nt), :] = tile


def _s5_core(uz, w_in, w_out, a_re, a_im, bsz, n_ctx, e):
    m = uz.shape[0]
    t_all = m // bsz
    nt, cg = S5_CHUNK, S5_GROUP
    lanes = 128
    gpb = lanes // cg
    nck, nck_ctx = t_all // nt, n_ctx // nt
    feat = nt * cg
    pst = w_out.shape[1] // 2
    assert w_in.shape[1:] == (feat, feat + 2 * pst) and w_out.shape[1:] == (2 * pst, feat)
    return pl.pallas_call(
        functools.partial(_s5_core_kernel, nck=nck, nck_ctx=nck_ctx),
        grid=(bsz, e // lanes),
        in_specs=[pl.BlockSpec((t_all, lanes), lambda b, q: (b, q)),
                  pl.BlockSpec((gpb,) + w_in.shape[1:], lambda b, q: (q, 0, 0)),
                  pl.BlockSpec((gpb,) + w_out.shape[1:], lambda b, q: (q, 0, 0)),
                  pl.BlockSpec((1, gpb * pst), lambda b, q: (0, q)),
                  pl.BlockSpec((1, gpb * pst), lambda b, q: (0, q))],
        out_specs=pl.BlockSpec((t_all, lanes), lambda b, q: (b, q)),
        out_shape=jax.ShapeDtypeStruct((m, e), F32),
        scratch_shapes=[pltpu.VMEM((nck, gpb * feat), BF16), pltpu.VMEM((nck, gpb * feat), F32)]
                       + [pltpu.VMEM((nck, gpb * pst), F32)] * 6,
        compiler_params=_cparams("arbitrary", "arbitrary"),
        name="s5_core",
    )(uz, w_in.astype(BF16), w_out.astype(BF16), a_re, a_im)


def _s5_out_kernel(y_ref, u_ref, z_ref, d_ref, wg_ref, bg_ref, w_ref, res_ref, gate_ref, o_ref):
    y = jax.nn.gelu(y_ref[...] + d_ref[...] * u_ref[...])
    y = y * jax.nn.sigmoid(jnp.dot(y.astype(BF16), wg_ref[...], preferred_element_type=F32) + bg_ref[...])
    z = z_ref[...]
    o = (y * (z * jax.nn.sigmoid(z))).astype(BF16)
    o_ref[...] = res_ref[...] + gate_ref[0] * jnp.dot(o, w_ref[...], preferred_element_type=F32)


def _s5_out(y, uz, d_skip, w_glu, b_glu, w_out, res2, gate, tm, tiles_per_b, ctx_tiles, n_batch):
    m, e = y.shape
    d = w_out.shape[1]
    vec = pl.BlockSpec((1, e), lambda i: (0, 0))
    return pl.pallas_call(
        _s5_out_kernel,
        grid=(m // tm,),
        in_specs=[pl.BlockSpec((tm, e), lambda i: (i, 0)),
                  pl.BlockSpec((tm, e), lambda i: (i, 0)),
                  pl.BlockSpec((tm, e), lambda i: (i, 1)),
                  vec, pl.BlockSpec((e, e), lambda i: (0, 0)), vec,
                  pl.BlockSpec((e, d), lambda i: (0, 0)),
                  pl.BlockSpec((tm, d), lambda i: (i, 0)),
                  pl.BlockSpec((1, 1, d), lambda i: (_mod_row(i, tiles_per_b, ctx_tiles, n_batch), 0, 0))],
        out_specs=pl.BlockSpec((tm, d), lambda i: (i, 0)),
        out_shape=jax.ShapeDtypeStruct((m, d), F32),
        compiler_params=_cparams("arbitrary"),
        name="s5_out",
    )(y, uz, uz, d_skip.reshape(1, e).astype(F32), w_glu.astype(BF16), b_glu.reshape(1, e).astype(F32),
      w_out.astype(BF16), res2, gate)


def _s5_layer(stream, norm_g, scale, shift, gate, n_ctx, tm, tile_args, w_in, lam_re, lam_im, log_dt, b_re, b_im,
              c_re, c_im, d_skip, w_glu, b_glu, w_out):
    bsz, t_all, d = stream.shape
    m = bsz * t_all
    e = w_in.shape[1] // 2
    uz = _norm_mm(stream.reshape(m, d), norm_g, scale, shift, w_in, tm, *tile_args)
    k_in, k_out, a_re, a_im = _s5_matrices(lam_re, lam_im, log_dt, b_re, b_im, c_re, c_im)
    y = _s5_core(uz, k_in, k_out, a_re, a_im, bsz, n_ctx, e)
    return _s5_out(y, uz, d_skip, w_glu, b_glu, w_out, stream.reshape(m, d), gate, tm,
                   *tile_args).reshape(bsz, t_all, d)


def kernel(x, c, ctx, c_ctx, norm_g, w_mod, b_mod, rwkv_mu, rwkv_w_rkvg, rwkv_w0, rwkv_w1, rwkv_w2, rwkv_a0, rwkv_a1, rwkv_a2, rwkv_k_k, rwkv_k_a, rwkv_r_k, rwkv_ln_w, rwkv_ln_b, rwkv_w_out, na_w_in, na_q_g, na_k_g, na_rpb, na_w_out, s5_w_in, s5_lam_re, s5_lam_im, s5_log_dt, s5_b_re, s5_b_im, s5_c_re, s5_c_im, s5_d, s5_w_glu, s5_b_glu, s5_w_out):
    bsz, n_lat, d = x.shape
    n_ctx = ctx.shape[1]
    t_all = n_ctx + n_lat
    depth = norm_g.shape[0]
    grid_w = 64
    tm = _token_tile(n_ctx, t_all)
    tile_args = (t_all // tm, n_ctx // tm, bsz)
    stream = jnp.concatenate([ctx, x], axis=1).astype(F32)
    rows = 8 * ((bsz + 1 + 7) // 8)
    cc = jnp.zeros((rows, d), F32).at[:bsz].set(c.astype(F32)).at[bsz].set(c_ctx.astype(F32))
    for i in range(depth):
        kind, j = i % 3, i // 3
        mod = _modulation(cc, w_mod[i].astype(F32), b_mod[i].astype(F32))[:bsz + 1]
        shift, scale, gate = (mod[:, k * d:(k + 1) * d].reshape(bsz + 1, 1, d) for k in range(3))
        if kind == 0:
            stream = _rwkv_layer(stream, norm_g[i], scale, shift, gate, n_ctx, rwkv_mu[j], rwkv_w_rkvg[j], rwkv_w0[j],
                                 rwkv_w1[j], rwkv_w2[j], rwkv_a0[j], rwkv_a1[j], rwkv_a2[j], rwkv_k_k[j],
                                 rwkv_k_a[j], rwkv_r_k[j], rwkv_ln_w[j], rwkv_ln_b[j], rwkv_w_out[j])
        elif kind == 1:
            stream = _na_layer(stream, norm_g[i], scale, shift, gate, n_ctx, tm, tile_args, grid_w, na_w_in[j],
                               na_q_g[j], na_k_g[j], na_rpb[j], na_w_out[j])
        else:
            stream = _s5_layer(stream, norm_g[i], scale, shift, gate, n_ctx, tm, tile_args, s5_w_in[j],
                               s5_lam_re[j], s5_lam_im[j], s5_log_dt[j], s5_b_re[j], s5_b_im[j], s5_c_re[j],
                               s5_c_im[j], s5_d[j], s5_w_glu[j], s5_b_glu[j], s5_w_out[j])
    return stream[:, n_ctx:].astype(x.dtype)
```

```python
import functools
import math

import numpy as np
import jax
import jax.numpy as jnp
from jax import lax
from jax.experimental import pallas as pl
from jax.experimental.pallas import tpu as pltpu

F32 = jnp.float32
BF16 = jnp.bfloat16
NORM_EPS = 1e-6
RWKV_GN_EPS = 64e-5
HEAD_DIM = 64
RWKV_CHUNK = 64
RWKV_HEADS_PER_STEP = 16
NA_ROWS_PER_STEP = 4
S5_CHUNK = 16
S5_GROUP = 16
MASK_NEG = -1e30
VMEM_LIMIT = 48 * 1024 * 1024
HIGHEST = lax.Precision.HIGHEST


def _cparams(*sem):
    return pltpu.CompilerParams(dimension_semantics=sem, vmem_limit_bytes=VMEM_LIMIT)


def _token_tile(n_ctx, n_all, largest=256):
    for t in (256, 128, 64):
        if t <= largest and n_ctx % t == 0 and n_all % t == 0:
            return t
    raise ValueError("context / sequence lengths must be multiples of 64")


def _mod_row(i, tiles_per_b, ctx_tiles, n_batch):
    return jnp.where(i % tiles_per_b < ctx_tiles, n_batch, i // tiles_per_b)


def _mod_kernel(c_ref, w_ref, b_ref, o_ref):
    c = c_ref[...]
    s = c * jax.nn.sigmoid(c)
    o_ref[...] = jnp.dot(s, w_ref[...], precision=HIGHEST, preferred_element_type=F32) + b_ref[...]


def _modulation(cc, wm, bm):
    rows, d = cc.shape
    n = wm.shape[1]
    tn = 512
    return pl.pallas_call(
        _mod_kernel,
        grid=(n // tn,),
        in_specs=[pl.BlockSpec((rows, d), lambda j: (0, 0)),
                  pl.BlockSpec((d, tn), lambda j: (0, j)),
                  pl.BlockSpec((1, tn), lambda j: (0, j))],
        out_specs=pl.BlockSpec((rows, tn), lambda j: (0, j)),
        out_shape=jax.ShapeDtypeStruct((rows, n), F32),
        compiler_params=_cparams("arbitrary"),
        name="modulation",
    )(cc, wm, bm.reshape(1, n))


def _norm_mm_kernel(x_ref, g_ref, sc_ref, sh_ref, w_ref, o_ref, h_ref):
    @pl.when(pl.program_id(1) == 0)
    def _():
        x = x_ref[...]
        ms = jnp.mean(x * x, axis=-1, keepdims=True)
        y = x * lax.rsqrt(ms + NORM_EPS) * g_ref[...]
        h_ref[...] = (y * (1.0 + sc_ref[0]) + sh_ref[0]).astype(BF16)

    o_ref[...] = jnp.dot(h_ref[...], w_ref[...], preferred_element_type=F32)


def _norm_mm(x2, g, scale, shift, w, tm, tiles_per_b, ctx_tiles, n_batch):
    m, d = x2.shape
    n = w.shape[1]
    tn = n if n <= 2048 else 2048
    row = lambda i, j: (_mod_row(i, tiles_per_b, ctx_tiles, n_batch), 0, 0)
    return pl.pallas_call(
        _norm_mm_kernel,
        grid=(m // tm, n // tn),
        in_specs=[pl.BlockSpec((tm, d), lambda i, j: (i, 0)),
                  pl.BlockSpec((1, d), lambda i, j: (0, 0)),
                  pl.BlockSpec((1, 1, d), row),
                  pl.BlockSpec((1, 1, d), row),
                  pl.BlockSpec((d, tn), lambda i, j: (0, j))],
        out_specs=pl.BlockSpec((tm, tn), lambda i, j: (i, j)),
        out_shape=jax.ShapeDtypeStruct((m, n), F32),
        scratch_shapes=[pltpu.VMEM((tm, d), BF16)],
        compiler_params=_cparams("arbitrary", "arbitrary"),
        name="norm_matmul",
    )(x2, g.reshape(1, d), scale, shift, w.astype(BF16))


def _mm_res_kernel(x_ref, w_ref, res_ref, gate_ref, o_ref):
    acc = jnp.dot(x_ref[...].astype(BF16), w_ref[...], preferred_element_type=F32)
    o_ref[...] = res_ref[...] + gate_ref[0] * acc


def _mm_residual(x2, w, res2, gate, tm, tiles_per_b, ctx_tiles, n_batch):
    m, k = x2.shape
    n = w.shape[1]
    return pl.pallas_call(
        _mm_res_kernel,
        grid=(m // tm,),
        in_specs=[pl.BlockSpec((tm, k), lambda i: (i, 0)),
                  pl.BlockSpec((k, n), lambda i: (0, 0)),
                  pl.BlockSpec((tm, n), lambda i: (i, 0)),
                  pl.BlockSpec((1, 1, n), lambda i: (_mod_row(i, tiles_per_b, ctx_tiles, n_batch), 0, 0))],
        out_specs=pl.BlockSpec((tm, n), lambda i: (i, 0)),
        out_shape=jax.ShapeDtypeStruct((m, n), F32),
        compiler_params=_cparams("arbitrary"),
        name="matmul_residual",
    )(x2, w.astype(BF16), res2, gate)


def _dot_nt(a, b):
    return lax.dot_general(a, b, (((1,), (1,)), ((), ())), preferred_element_type=F32)


def _dot_tn(a, b):
    return lax.dot_general(a, b, (((0,), (0,)), ((), ())), preferred_element_type=F32)


def _dot(a, b):
    return jnp.dot(a, b, preferred_element_type=F32)


def _head_sum(x, ones_bd):
    lanes = ones_bd.shape[0]
    return jnp.concatenate(
        [jnp.dot(x[:, p * lanes:(p + 1) * lanes].astype(BF16), ones_bd, preferred_element_type=F32)
         for p in range(x.shape[1] // lanes)], axis=1)


def _same_head(lanes):
    return (lax.broadcasted_iota(jnp.int32, (lanes, lanes), 0) // HEAD_DIM
            == lax.broadcasted_iota(jnp.int32, (lanes, lanes), 1) // HEAD_DIM)


def _rwkv_proj_kernel(x_ref, xp_ref, xn_ref, g_ref, sc_ref, sh_ref, mu_ref, w_ref, w1_ref, w2_ref, a1_ref, a2_ref,
                      w0_ref, a0_ref, kk_ref, ka_ref, rk_ref,
                      r_out, v_out, kk_out, g_out, bonus_out, lw_out, kd_out, bd_out, *, tm, tiles_per_b, ctx_tiles):
    i = pl.program_id(0)
    j = i % tiles_per_b
    gain = g_ref[...]
    sc = 1.0 + sc_ref[0]
    sh = sh_ref[0]

    def norm(x):
        ms = jnp.mean(x * x, axis=-1, keepdims=True)
        return x * lax.rsqrt(ms + NORM_EPS) * gain * sc + sh

    h = norm(x_ref[...])
    has_prev = jnp.logical_and(j != 0, j != ctx_tiles)
    has_next = jnp.logical_and(j != ctx_tiles - 1, j != tiles_per_b - 1)
    h_prev = jnp.where(has_prev, norm(xp_ref[...])[7:8], 0.0)
    h_next = jnp.where(has_next, norm(xn_ref[...])[0:1], 0.0)
    row = lax.broadcasted_iota(jnp.int32, h.shape, 0)
    prev = jnp.where(row == 0, h_prev, pltpu.roll(h, 1, axis=0))
    nxt = jnp.where(row == tm - 1, h_next, pltpu.roll(h, tm - 1, axis=0))
    xx = 0.5 * (prev + nxt) - h
    mix = lambda n: (h + xx * mu_ref[n:n + 1, :]).astype(BF16)

    r = jnp.dot(mix(0), w_ref[0], preferred_element_type=F32)
    k = jnp.dot(mix(1), w_ref[1], preferred_element_type=F32)
    v = jnp.dot(mix(2), w_ref[2], preferred_element_type=F32)
    g_out[...] = jnp.dot(mix(3), w_ref[3], preferred_element_type=F32)
    dec = jnp.dot(jnp.tanh(jnp.dot(mix(4), w1_ref[...], preferred_element_type=F32)).astype(BF16), w2_ref[...],
                  preferred_element_type=F32)
    icl = jnp.dot(jnp.dot(mix(5), a1_ref[...], preferred_element_type=F32).astype(BF16), a2_ref[...],
                  preferred_element_type=F32)
    e = r.shape[1]
    ones_bd = _same_head(2 * HEAD_DIM).astype(BF16)
    kkf = k * kk_ref[...]
    kk = kkf / jnp.maximum(jnp.sqrt(_head_sum(kkf * kkf, ones_bd)), 1e-12)
    r_out[...] = r
    v_out[...] = v
    kk_out[...] = kk
    bonus_out[...] = _head_sum(r * k * rk_ref[...], ones_bd) * v
    for s in range(2):
        lw_out[s] = -math.exp(-0.5) * jax.nn.sigmoid(w0_ref[s:s + 1, :] + dec[:, s * e:(s + 1) * e])
        a = jax.nn.sigmoid(a0_ref[s:s + 1, :] + icl[:, s * e:(s + 1) * e])
        kd_out[s] = k * (1.0 + (a - 1.0) * ka_ref[...])
        bd_out[s] = kk * a


def _rwkv_proj(stream2, norm_g, scale, shift, mu, w_rkvg, w0, w1, w2, a0, a1, a2, k_k, k_a, r_k, tm, tiles_per_b,
               ctx_tiles, n_batch):
    m, d = stream2.shape
    e = w_rkvg.shape[-1]
    lr = w1.shape[-1]
    nblk = m // 8
    zeros = jnp.zeros((lr, e), F32)
    cat = lambda w: jnp.concatenate([w[0], w[1]], axis=1).astype(BF16)
    bdiag = lambda w: jnp.concatenate([jnp.concatenate([w[0], zeros], axis=1),
                                       jnp.concatenate([zeros, w[1]], axis=1)], axis=0).astype(BF16)
    row = lambda i: (_mod_row(i, tiles_per_b, ctx_tiles, n_batch), 0, 0)
    full = lambda shape: pl.BlockSpec(shape, lambda i: (0,) * len(shape))
    tok = pl.BlockSpec((tm, e), lambda i: (i, 0))
    tok2 = pl.BlockSpec((2, tm, e), lambda i: (0, i, 0))
    vec = lambda a: a.reshape(1, e).astype(F32)
    return pl.pallas_call(
        functools.partial(_rwkv_proj_kernel, tm=tm, tiles_per_b=tiles_per_b, ctx_tiles=ctx_tiles),
        grid=(m // tm,),
        in_specs=[pl.BlockSpec((tm, d), lambda i: (i, 0)),
                  pl.BlockSpec((8, d), lambda i: (jnp.maximum(i * (tm // 8) - 1, 0), 0)),
                  pl.BlockSpec((8, d), lambda i: (jnp.minimum((i + 1) * (tm // 8), nblk - 1), 0)),
                  full((1, d)), pl.BlockSpec((1, 1, d), row), pl.BlockSpec((1, 1, d), row),
                  full(mu.shape), full(w_rkvg.shape), full((d, 2 * lr)), full((2 * lr, 2 * e)),
                  full((d, 2 * lr)), full((2 * lr, 2 * e)), full((2, e)), full((2, e)),
                  full((1, e)), full((1, e)), full((1, e))],
        out_specs=[tok, tok, tok, tok, tok, tok2, tok2, tok2],
        out_shape=[jax.ShapeDtypeStruct((m, e), F32)] * 5 + [jax.ShapeDtypeStruct((2, m, e), F32)] * 3,
        compiler_params=_cparams("arbitrary"),
        name="rwkv_proj",
    )(stream2, stream2, stream2, norm_g.reshape(1, d), scale, shift, mu.astype(F32), w_rkvg.astype(BF16),
      cat(w1), bdiag(w2), cat(a1), bdiag(a2), w0.astype(F32), a0.astype(F32), vec(k_k), vec(k_a), vec(r_k))


def _cumsum_rows(x, reverse):
    c = x.shape[0]
    row = lax.broadcasted_iota(jnp.int32, x.shape, 0)
    shift = 1
    while shift < c:
        if reverse:
            x = x + jnp.where(row < c - shift, pltpu.roll(x, c - shift, axis=0), 0.0)
        else:
            x = x + jnp.where(row >= shift, pltpu.roll(x, shift, axis=0), 0.0)
        shift *= 2
    return x


def _rwkv_chunk_streams(streams):
    c = streams[0][0].shape[0]
    hd = HEAD_DIM
    lanes = 2 * hd
    row = lax.broadcasted_iota(jnp.int32, (c, 2 * c), 0)
    col = lax.broadcasted_iota(jnp.int32, (c, 2 * c), 1) % c
    eye2 = (row == col).astype(F32)
    lane_head = lax.broadcasted_iota(jnp.int32, (1, lanes), 1) // hd
    first = lane_head == 0
    same = _same_head(lanes)
    pick = lambda a, p: a[:, p * lanes:(p + 1) * lanes]
    zero = jnp.zeros((), BF16)

    units, incl2, strict2 = [], {}, {}
    lhs, k_h, b_h, k_p, b_p, vb, lp_tot, s_bd = {}, {}, {}, {}, {}, {}, {}, {}
    for si, (r, v, kk, lw, kd, bd, states, reverse) in enumerate(streams):
        incl2[si] = (col >= row) if reverse else (col <= row)
        strict2[si] = (col > row) if reverse else (col < row)
        lp = _cumsum_rows(lw, reverse)
        tot = jnp.sum(lw, axis=0, keepdims=True)
        lhs_f = jnp.concatenate([kk * jnp.exp(lp - lw), r * jnp.exp(lp)], axis=0).astype(BF16)
        e_ninc = jnp.exp(-lp)
        e_rem = jnp.exp(tot - lp)
        full = dict(lhs=lhs_f, k_h=(kd * e_ninc).astype(BF16), b_h=(bd * e_ninc).astype(BF16),
                    k_p=(kd * e_rem).astype(BF16), b_p=(bd * e_rem).astype(BF16), vb=v.astype(BF16), tot=tot)
        for p in range(len(states)):
            u_ = (si, p)
            units.append(u_)
            lhs[u_], k_h[u_], b_h[u_] = pick(full["lhs"], p), pick(full["k_h"], p), pick(full["b_h"], p)
            k_p[u_], b_p[u_], vb[u_] = pick(full["k_p"], p), pick(full["b_p"], p), pick(full["vb"], p)
            lp_tot[u_], s_bd[u_] = pick(full["tot"], p), states[p]
    bd = lambda x: jnp.concatenate([jnp.where(first, x, zero), jnp.where(first, zero, x)], axis=0)

    a_k = {u_: _dot_nt(lhs[u_], bd(k_h[u_])) for u_ in units}
    a_b = {u_: _dot_nt(lhs[u_], bd(b_h[u_])) for u_ in units}
    ls = {u_: _dot_nt(lhs[u_], s_bd[u_].astype(BF16)) for u_ in units}
    av = {u_: ls[u_] + _dot(jnp.concatenate([jnp.where(strict2[u_[0]], a_k[u_][:c], 0.0),
                                             jnp.where(incl2[u_[0]], a_k[u_][c:], 0.0)], axis=0).astype(BF16),
                            bd(vb[u_])) for u_ in units}
    z = {u_: av[u_][:c] for u_ in units}
    y0 = {u_: av[u_][c:] for u_ in units}

    m_neg = {u_: jnp.where(strict2[u_[0]], -a_b[u_][:c], 0.0) for u_ in units}
    t_inv = {u_: eye2 + m_neg[u_] for u_ in units}
    pw = {u_: m_neg[u_].astype(BF16) for u_ in units}
    pw = {u_: _dot(pw[u_], bd(pw[u_])).astype(BF16) for u_ in units}
    for _ in range(int(math.log2(c)) - 2):
        sq = {u_: _dot(jnp.concatenate([pw[u_], t_inv[u_].astype(BF16)], axis=0), bd(pw[u_])) for u_ in units}
        t_inv = {u_: t_inv[u_] + sq[u_][c:] for u_ in units}
        pw = {u_: sq[u_][:c].astype(BF16) for u_ in units}
    t_inv = {u_: t_inv[u_] + _dot(t_inv[u_].astype(BF16), bd(pw[u_])) for u_ in units}

    u = {u_: _dot(t_inv[u_].astype(BF16), bd(z[u_].astype(BF16))) for u_ in units}
    y = {u_: y0[u_] - _dot(jnp.where(incl2[u_[0]], a_b[u_][c:], 0.0).astype(BF16), bd(u[u_].astype(BF16)))
         for u_ in units}
    s1 = {}
    for u_ in units:
        upd = _dot_tn(jnp.concatenate([vb[u_], -u[u_].astype(BF16)], axis=0),
                      jnp.concatenate([k_p[u_], b_p[u_]], axis=0))
        s1[u_] = s_bd[u_] * jnp.exp(lp_tot[u_]) + jnp.where(same, upd, 0.0)
    return [(jnp.concatenate([y[si, p] for p in range(len(st[6]))], axis=1),
             [s1[si, p] for p in range(len(st[6]))]) for si, st in enumerate(streams)]


def _rwkv_scan_kernel(rf_ref, vf_ref, kkf_ref, lwf_ref, kdf_ref, bdf_ref,
                      rb_ref, vb_ref, kkb_ref, lwb_ref, kdb_ref, bdb_ref, yf_ref, yb_ref, s_ref):
    @pl.when(pl.program_id(2) == 0)
    def _():
        s_ref[...] = jnp.zeros_like(s_ref)

    npairs = s_ref.shape[0] // 2
    fwd = (rf_ref[0], vf_ref[0], kkf_ref[0], lwf_ref[0, 0], kdf_ref[0, 0], bdf_ref[0, 0],
           [s_ref[p] for p in range(npairs)], False)
    bwd = (rb_ref[0], vb_ref[0], kkb_ref[0], lwb_ref[0, 0], kdb_ref[0, 0], bdb_ref[0, 0],
           [s_ref[npairs + p] for p in range(npairs)], True)
    (y_f, s_f), (y_b, s_b) = _rwkv_chunk_streams([fwd, bwd])
    yf_ref[0] = y_f
    yb_ref[0] = y_b
    for p in range(npairs):
        s_ref[p] = s_f[p]
        s_ref[npairs + p] = s_b[p]


def _rwkv_scan(r, v, kk, lw, kd, bd, n_ctx):
    bsz, t, e = r.shape
    c = RWKV_CHUNK
    width = RWKV_HEADS_PER_STEP * HEAD_DIM
    nc, nc_ctx = t // c, n_ctx // c
    rev = lambda ci: jnp.where(ci < nc_ctx, nc_ctx - 1 - ci, nc + nc_ctx - 1 - ci)
    tok_f = pl.BlockSpec((1, c, width), lambda b, hg, ci: (b, ci, hg))
    tok_b = pl.BlockSpec((1, c, width), lambda b, hg, ci: (b, rev(ci), hg))
    dir_f = pl.BlockSpec((1, 1, c, width), lambda b, hg, ci: (0, b, ci, hg))
    dir_b = pl.BlockSpec((1, 1, c, width), lambda b, hg, ci: (1, b, rev(ci), hg))
    return pl.pallas_call(
        _rwkv_scan_kernel,
        grid=(bsz, e // width, nc),
        in_specs=[tok_f, tok_f, tok_f, dir_f, dir_f, dir_f, tok_b, tok_b, tok_b, dir_b, dir_b, dir_b],
        out_specs=[tok_f, tok_b],
        out_shape=[jax.ShapeDtypeStruct((bsz, t, e), F32)] * 2,
        scratch_shapes=[pltpu.VMEM((2 * width // (2 * HEAD_DIM), 2 * HEAD_DIM, 2 * HEAD_DIM), F32)],
        compiler_params=_cparams("arbitrary", "arbitrary", "arbitrary"),
        name="rwkv_scan",
    )(r, v, kk, lw, kd, bd, r, v, kk, lw, kd, bd)


def _rwkv_out_kernel(yf_ref, yb_ref, bonus_ref, g_ref, lnw_ref, lnb_ref, w_ref, res_ref, gate_ref, o_ref):
    ones_bd = _same_head(2 * HEAD_DIM).astype(BF16)
    y = yf_ref[...] + yb_ref[...]
    mean = _head_sum(y, ones_bd) * (1.0 / HEAD_DIM)
    yc = y - mean
    var = _head_sum(yc * yc, ones_bd) * (1.0 / HEAD_DIM)
    yn = yc * lax.rsqrt(var + RWKV_GN_EPS) * lnw_ref[...] + lnb_ref[...]
    g = g_ref[...]
    o = ((yn + bonus_ref[...]) * (g * jax.nn.sigmoid(g))).astype(BF16)
    o_ref[...] = res_ref[...] + gate_ref[0] * jnp.dot(o, w_ref[...], preferred_element_type=F32)


def _rwkv_out(y_f, y_b, bonus, g, ln_w, ln_b, w_out, res2, gate, tm, tiles_per_b, ctx_tiles, n_batch):
    m, e = bonus.shape
    d = w_out.shape[1]
    tok = pl.BlockSpec((tm, e), lambda i: (i, 0))
    vec = pl.BlockSpec((1, e), lambda i: (0, 0))
    return pl.pallas_call(
        _rwkv_out_kernel,
        grid=(m // tm,),
        in_specs=[tok, tok, tok, tok, vec, vec,
                  pl.BlockSpec((e, d), lambda i: (0, 0)),
                  pl.BlockSpec((tm, d), lambda i: (i, 0)),
                  pl.BlockSpec((1, 1, d), lambda i: (_mod_row(i, tiles_per_b, ctx_tiles, n_batch), 0, 0))],
        out_specs=pl.BlockSpec((tm, d), lambda i: (i, 0)),
        out_shape=jax.ShapeDtypeStruct((m, d), F32),
        compiler_params=_cparams("arbitrary"),
        name="rwkv_out",
    )(y_f, y_b, bonus, g, ln_w.reshape(1, e).astype(F32), ln_b.reshape(1, e).astype(F32), w_out.astype(BF16),
      res2, gate)


def _rwkv_layer(stream, norm_g, scale, shift, gate, n_ctx, mu, w_rkvg, w0, w1, w2, a0, a1, a2, k_k, k_a, r_k,
                ln_w, ln_b, w_out):
    bsz, t_all, d = stream.shape
    e = w_rkvg.shape[-1]
    m = bsz * t_all
    tm = _token_tile(n_ctx, t_all, 256)
    tile_args = (t_all // tm, n_ctx // tm, bsz)
    s2 = stream.reshape(m, d)
    r, v, kk, g, bonus, lw, kd, bd = _rwkv_proj(s2, norm_g, scale, shift, mu, w_rkvg, w0, w1, w2, a0, a1, a2,
                                                k_k, k_a, r_k, tm, *tile_args)
    b3 = lambda a: a.reshape(bsz, t_all, e)
    b4 = lambda a: a.reshape(2, bsz, t_all, e)
    y_f, y_b = _rwkv_scan(b3(r), b3(v), b3(kk), b4(lw), b4(kd), b4(bd), n_ctx)
    return _rwkv_out(y_f.reshape(m, e), y_b.reshape(m, e), bonus, g, ln_w, ln_b, w_out, s2, gate, tm,
                     *tile_args).reshape(bsz, t_all, d)


def _na_kernel(q_ref, k_ref, v_ref, z_ref, bias_ref, qg_ref, kg_ref, o_ref, kn_ref, vb_ref, *,
               n_ctx, grid_w, kh, khm, rows, scale, rq):
    step = pl.program_id(2)
    ctx_steps = n_ctx // (grid_w * rq)
    hd = HEAD_DIM
    lanes = 2 * hd
    same_head = _same_head(lanes).astype(BF16)
    lane_head = lax.broadcasted_iota(jnp.int32, (1, lanes), 1) // hd
    first = lane_head == 0

    def head_rms(x, g):
        ss = jnp.dot((x * x).astype(BF16), same_head, preferred_element_type=F32)
        return x * lax.rsqrt(ss * (1.0 / hd) + NORM_EPS) * g

    @pl.when(step == 0)
    def _():
        kn_ref[...] = head_rms(k_ref[0], kg_ref[...]).astype(BF16)
        vb_ref[...] = v_ref[0].astype(BF16)

    qn = head_rms(q_ref[0], qg_ref[...])
    z = z_ref[0]
    zgate = z * jax.nn.sigmoid(z)
    k_ctx = kn_ref[0:n_ctx, :]
    v_ctx = vb_ref[0:n_ctx, :]
    subs = range(rq)
    q2 = {s: jnp.concatenate([jnp.where(lane_head == hh, qn[s * grid_w:(s + 1) * grid_w], 0.0).astype(BF16)
                              for hh in range(2)], axis=0) for s in subs}

    def attend(bands):
        s_c = {s: _dot_nt(q2[s], k_ctx) * scale for s in subs}
        m = {s: jnp.max(s_c[s], axis=-1, keepdims=True) for s in subs}
        if bands is not None:
            s_n = {s: _dot_nt(q2[s], bands[s][0]) * scale
                      + jnp.concatenate([bands[s][2](hh) for hh in range(2)], axis=0) for s in subs}
            m = {s: jnp.maximum(m[s], jnp.max(s_n[s], axis=-1, keepdims=True)) for s in subs}
            p_n = {s: jnp.exp(s_n[s] - m[s]) for s in subs}
        p_c = {s: jnp.exp(s_c[s] - m[s]) for s in subs}
        den = {s: jnp.sum(p_c[s], axis=-1, keepdims=True) for s in subs}
        acc = {s: _dot(p_c[s].astype(BF16), v_ctx) for s in subs}
        if bands is not None:
            den = {s: den[s] + jnp.sum(p_n[s], axis=-1, keepdims=True) for s in subs}
            acc = {s: acc[s] + _dot(p_n[s].astype(BF16), bands[s][1]) for s in subs}
        o2 = {s: acc[s] / den[s] for s in subs}
        out = [jnp.where(first, o2[s][:grid_w], o2[s][grid_w:]) for s in subs]
        return jnp.concatenate(out, axis=0) * zgate

    @pl.when(step < ctx_steps)
    def _():
        o_ref[0] = attend(None)

    @pl.when(step >= ctx_steps)
    def _():
        bands = []
        for s in range(rq):
            i = (step - ctx_steps) * rq + s
            r0 = jnp.clip(i - kh // 2, 0, rows - kh)
            start = pl.multiple_of(n_ctx + r0 * grid_w, grid_w)
            base = khm - 1 - (i - r0)
            bias_of = functools.partial(
                lambda hh, base: jnp.concatenate([bias_ref[hh, base + 2 * q] for q in range(kh // 2)], axis=1),
                base=base)
            bands.append((kn_ref[pl.ds(start, kh * grid_w), :], vb_ref[pl.ds(start, kh * grid_w), :], bias_of))
        o_ref[0] = attend(bands)


def _na_bias_table(rpb, grid_w):
    kw = (rpb.shape[2] + 1) // 2
    j = np.arange(grid_w)[:, None]
    c = np.arange(grid_w)[None, :]
    c0 = np.clip(j - kw // 2, 0, grid_w - kw)
    valid = (c >= c0) & (c < c0 + kw)
    onehot = ((c - j + kw - 1)[None] == np.arange(2 * kw - 1)[:, None, None]) & valid[None]
    tiles = jnp.einsum("hab,bjc->hajc", rpb.astype(F32), jnp.asarray(onehot, F32), precision=HIGHEST)
    tiles = tiles + jnp.asarray(np.where(valid, 0.0, MASK_NEG), F32)
    return jnp.concatenate([tiles[:, :-1], tiles[:, 1:]], axis=-1)


def _na_attention(qkvz, rpb, q_g, k_g, n_ctx, grid_w, kh_max):
    bsz, t_all, e4 = qkvz.shape
    e = e4 // 4
    pairs = e // (2 * HEAD_DIM)
    rows = (t_all - n_ctx) // grid_w
    kh = min(kh_max, rows)
    assert kh % 2 == 0 and kh <= kh_max
    tbl = _na_bias_table(rpb, grid_w)
    ctx_tiles = n_ctx // grid_w
    rq = NA_ROWS_PER_STEP if (ctx_tiles % NA_ROWS_PER_STEP == 0 and rows % NA_ROWS_PER_STEP == 0) else 1
    lanes = 2 * HEAD_DIM
    g2 = lambda g: jnp.concatenate([g, g]).reshape(1, lanes).astype(F32)
    tok = lambda col0: pl.BlockSpec((1, rq * grid_w, lanes), lambda b, p, i: (b, i, col0 + p))
    seq = lambda col0: pl.BlockSpec((1, t_all, lanes), lambda b, p, i: (b, 0, col0 + p))
    return pl.pallas_call(
        functools.partial(_na_kernel, n_ctx=n_ctx, grid_w=grid_w, kh=kh, khm=kh_max, rows=rows,
                          scale=HEAD_DIM ** -0.5, rq=rq),
        grid=(bsz, pairs, t_all // (rq * grid_w)),
        in_specs=[tok(0), seq(pairs), seq(2 * pairs), tok(3 * pairs),
                  pl.BlockSpec((2,) + tbl.shape[1:], lambda b, p, i: (p, 0, 0, 0)),
                  pl.BlockSpec((1, lanes), lambda b, p, i: (0, 0)),
                  pl.BlockSpec((1, lanes), lambda b, p, i: (0, 0))],
        out_specs=tok(0),
        out_shape=jax.ShapeDtypeStruct((bsz, t_all, e), F32),
        scratch_shapes=[pltpu.VMEM((t_all, lanes), BF16), pltpu.VMEM((t_all, lanes), BF16)],
        compiler_params=_cparams("arbitrary", "arbitrary", "arbitrary"),
        name="na_attention",
    )(qkvz, qkvz, qkvz, qkvz, tbl, g2(q_g), g2(k_g))


def _na_layer(stream, norm_g, scale, shift, gate, n_ctx, tm, tile_args, grid_w, w_in, q_g, k_g, rpb, w_out):
    bsz, t_all, d = stream.shape
    m = bsz * t_all
    s2 = stream.reshape(m, d)
    qkvz = _norm_mm(s2, norm_g, scale, shift, w_in, tm, *tile_args).reshape(bsz, t_all, w_in.shape[1])
    kh_max = (rpb.shape[1] + 1) // 2
    o = _na_attention(qkvz, rpb, q_g, k_g, n_ctx, grid_w, kh_max)
    return _mm_residual(o.reshape(m, -1), w_out, s2, gate, tm, *tile_args).reshape(bsz, t_all, d)


def _s5_matrices(lam_re, lam_im, log_dt, b_re, b_im, c_re, c_im):
    nt = S5_CHUNK
    g, p, cg = b_re.shape
    tau = jnp.arange(nt + 1, dtype=F32)[:, None, None]
    i_idx = np.arange(nt)
    kbig, b_re_cols, b_im_cols, c_re_rows, c_im_rows, a_re, a_im = 0.0, [], [], [], [], [], []
    for s in range(2):
        lr, li = lam_re[s].astype(F32), lam_im[s].astype(F32)
        step = jnp.exp(log_dt[s].astype(F32))[:, None]
        mag = jnp.exp(lr * step)
        ar, ai = mag * jnp.cos(li * step), mag * jnp.sin(li * step)
        den = lr * lr + li * li
        qr = ((ar - 1.0) * lr + ai * li) / den
        qi = (ai * lr - (ar - 1.0) * li) / den
        bbr = qr[..., None] * b_re - qi[..., None] * b_im
        bbi = qr[..., None] * b_im + qi[..., None] * b_re
        pmag = jnp.exp(lr * step * tau)
        pr, pi = pmag * jnp.cos(li * step * tau), pmag * jnp.sin(li * step * tau)
        clr = c_re[None] * pr[:, :, None, :] - c_im[None] * pi[:, :, None, :]
        cli = c_re[None] * pi[:, :, None, :] + c_im[None] * pr[:, :, None, :]
        ker = (jnp.einsum("tgop,gpc->tgoc", clr[:nt], bbr, precision=HIGHEST)
               - jnp.einsum("tgop,gpc->tgoc", cli[:nt], bbi, precision=HIGHEST))
        lbr = pr[:, :, :, None] * bbr[None] - pi[:, :, :, None] * bbi[None]
        lbi = pr[:, :, :, None] * bbi[None] + pi[:, :, :, None] * bbr[None]
        lag = (i_idx[None, :] - i_idx[:, None]) if s == 0 else (i_idx[:, None] - i_idx[None, :])
        kb = jnp.where((lag >= 0)[:, :, None, None, None], ker[np.clip(lag, 0, nt - 1)], 0.0)
        kbig = kbig + kb.transpose(2, 0, 4, 1, 3).reshape(g, nt * cg, nt * cg)
        inj = (nt - 1 - i_idx) if s == 0 else i_idx
        flat_in = lambda a: a[inj].transpose(1, 0, 3, 2).reshape(g, nt * cg, p)
        b_re_cols.append(flat_in(lbr))
        b_im_cols.append(flat_in(lbi))
        out = (i_idx + 1) if s == 0 else (nt - i_idx)
        flat_out = lambda a: a[out].transpose(1, 3, 0, 2).reshape(g, p, nt * cg)
        c_re_rows.append(flat_out(clr))
        c_im_rows.append(flat_out(-cli))
        a_re.append(pr[nt])
        a_im.append(pi[nt])
    w_in = jnp.concatenate([kbig] + b_re_cols + b_im_cols, axis=2)
    w_out = jnp.concatenate(c_re_rows + c_im_rows, axis=1)
    coef = lambda a: jnp.concatenate(a, axis=-1).reshape(1, g * 2 * p)
    return w_in, w_out, coef(a_re), coef(a_im)


def _s5_core_kernel(u_ref, win_ref, wout_ref, are_ref, aim_ref, y_ref,
                    x_ref, yi_ref, bre_ref, bim_ref, xa_re, xa_im, xb_re, xb_im, *, nck, nck_ctx):
    nt, cg = S5_CHUNK, S5_GROUP
    gpb = u_ref.shape[1] // cg
    feat = nt * cg
    half = feat // 2
    pst = bre_ref.shape[1] // gpb
    lane_blk = lax.broadcasted_iota(jnp.int32, (1, u_ref.shape[1]), 1) // cg

    def regroup(piece_of, key_of):
        acc = None
        for n in range(gpb):
            src, shift = piece_of(n)
            rolled = pltpu.roll(src, shift, axis=1) if shift else src
            acc = rolled if acc is None else jnp.where(lane_blk == key_of(n), rolled, acc)
        return acc

    for hf in range(nt // gpb):
        for g8 in range(gpb):
            tile = regroup(lambda jj: (u_ref[pl.ds(gpb * hf + jj, nck, stride=nt), :], (cg * (jj - g8)) % (gpb * cg)),
                           lambda jj: jj)
            x_ref[:, g8 * feat + hf * half:g8 * feat + (hf + 1) * half] = tile.astype(BF16)

    for g8 in range(gpb):
        yb = jnp.dot(x_ref[:, g8 * feat:(g8 + 1) * feat], win_ref[g8], preferred_element_type=F32)
        yi_ref[:, g8 * feat:(g8 + 1) * feat] = yb[:, :feat]
        bre_ref[:, g8 * pst:(g8 + 1) * pst] = yb[:, feat:feat + pst]
        bim_ref[:, g8 * pst:(g8 + 1) * pst] = yb[:, feat + pst:feat + 2 * pst]

    dir0 = (lax.broadcasted_iota(jnp.int32, (1, bre_ref.shape[1]), 1) % pst) < pst // 2
    a_re = are_ref[...]
    a_im = aim_ref[...]

    def body(k, carry):
        xr, xi = carry
        rk = jnp.where(k < nck_ctx, nck_ctx - 1 - k, nck + nck_ctx - 1 - k)
        xa_re[pl.ds(k, 1), :] = xr
        xa_im[pl.ds(k, 1), :] = xi
        xb_re[pl.ds(rk, 1), :] = xr
        xb_im[pl.ds(rk, 1), :] = xi
        b_r = jnp.where(dir0, bre_ref[pl.ds(k, 1), :], bre_ref[pl.ds(rk, 1), :])
        b_i = jnp.where(dir0, bim_ref[pl.ds(k, 1), :], bim_ref[pl.ds(rk, 1), :])
        return a_re * xr - a_im * xi + b_r, a_re * xi + a_im * xr + b_i

    zero = jnp.zeros(a_re.shape, F32)
    lax.fori_loop(0, nck, body, (zero, zero))

    d0 = dir0[:, :pst]
    for g8 in range(gpb):
        blk = slice(g8 * pst, (g8 + 1) * pst)
        state = jnp.concatenate([jnp.where(d0, xa_re[:, blk], xb_re[:, blk]),
                                 jnp.where(d0, xa_im[:, blk], xb_im[:, blk])], axis=1).astype(BF16)
        yi_ref[:, g8 * feat:(g8 + 1) * feat] += jnp.dot(state, wout_ref[g8], preferred_element_type=F32)

    for i in range(nt):
        hf, ii = divmod(i, gpb)
        tile = regroup(lambda g8: (yi_ref[:, g8 * feat + hf * half:g8 * feat + (hf + 1) * half],
                                   (cg * (g8 - ii)) % (gpb * cg)),
                       lambda g8: g8)
        y_ref[pl.ds(i, nck, stride=nt), :] = tile


def _s5_core(uz, w_in, w_out, a_re, a_im, bsz, n_ctx, e):
    m = uz.shape[0]
    t_all = m // bsz
    nt, cg = S5_CHUNK, S5_GROUP
    lanes = 128
    gpb = lanes // cg
    nck, nck_ctx = t_all // nt, n_ctx // nt
    feat = nt * cg
    pst = w_out.shape[1] // 2
    assert w_in.shape[1:] == (feat, feat + 2 * pst) and w_out.shape[1:] == (2 * pst, feat)
    return pl.pallas_call(
        functools.partial(_s5_core_kernel, nck=nck, nck_ctx=nck_ctx),
        grid=(bsz, e // lanes),
        in_specs=[pl.BlockSpec((t_all, lanes), lambda b, q: (b, q)),
                  pl.BlockSpec((gpb,) + w_in.shape[1:], lambda b, q: (q, 0, 0)),
                  pl.BlockSpec((gpb,) + w_out.shape[1:], lambda b, q: (q, 0, 0)),
                  pl.BlockSpec((1, gpb * pst), lambda b, q: (0, q)),
                  pl.BlockSpec((1, gpb * pst), lambda b, q: (0, q))],
        out_specs=pl.BlockSpec((t_all, lanes), lambda b, q: (b, q)),
        out_shape=jax.ShapeDtypeStruct((m, e), F32),
        scratch_shapes=[pltpu.VMEM((nck, gpb * feat), BF16), pltpu.VMEM((nck, gpb * feat), F32)]
                       + [pltpu.VMEM((nck, gpb * pst), F32)] * 6,
        compiler_params=_cparams("arbitrary", "arbitrary"),
        name="s5_core",
    )(uz, w_in.astype(BF16), w_out.astype(BF16), a_re, a_im)


def _s5_out_kernel(y_ref, u_ref, z_ref, d_ref, wg_ref, bg_ref, w_ref, res_ref, gate_ref, o_ref):
    y = jax.nn.gelu(y_ref[...] + d_ref[...] * u_ref[...])
    y = y * jax.nn.sigmoid(jnp.dot(y.astype(BF16), wg_ref[...], preferred_element_type=F32) + bg_ref[...])
    z = z_ref[...]
    o = (y * (z * jax.nn.sigmoid(z))).astype(BF16)
    o_ref[...] = res_ref[...] + gate_ref[0] * jnp.dot(o, w_ref[...], preferred_element_type=F32)


def _s5_out(y, uz, d_skip, w_glu, b_glu, w_out, res2, gate, tm, tiles_per_b, ctx_tiles, n_batch):
    m, e = y.shape
    d = w_out.shape[1]
    vec = pl.BlockSpec((1, e), lambda i: (0, 0))
    return pl.pallas_call(
        _s5_out_kernel,
        grid=(m // tm,),
        in_specs=[pl.BlockSpec((tm, e), lambda i: (i, 0)),
                  pl.BlockSpec((tm, e), lambda i: (i, 0)),
                  pl.BlockSpec((tm, e), lambda i: (i, 1)),
                  vec, pl.BlockSpec((e, e), lambda i: (0, 0)), vec,
                  pl.BlockSpec((e, d), lambda i: (0, 0)),
                  pl.BlockSpec((tm, d), lambda i: (i, 0)),
                  pl.BlockSpec((1, 1, d), lambda i: (_mod_row(i, tiles_per_b, ctx_tiles, n_batch), 0, 0))],
        out_specs=pl.BlockSpec((tm, d), lambda i: (i, 0)),
        out_shape=jax.ShapeDtypeStruct((m, d), F32),
        compiler_params=_cparams("arbitrary"),
        name="s5_out",
    )(y, uz, uz, d_skip.reshape(1, e).astype(F32), w_glu.astype(BF16), b_glu.reshape(1, e).astype(F32),
      w_out.astype(BF16), res2, gate)


def _s5_layer(stream, norm_g, scale, shift, gate, n_ctx, tm, tile_args, w_in, lam_re, lam_im, log_dt, b_re, b_im,
              c_re, c_im, d_skip, w_glu, b_glu, w_out):
    bsz, t_all, d = stream.shape
    m = bsz * t_all
    e = w_in.shape[1] // 2
    uz = _norm_mm(stream.reshape(m, d), norm_g, scale, shift, w_in, tm, *tile_args)
    k_in, k_out, a_re, a_im = _s5_matrices(lam_re, lam_im, log_dt, b_re, b_im, c_re, c_im)
    y = _s5_core(uz, k_in, k_out, a_re, a_im, bsz, n_ctx, e)
    return _s5_out(y, uz, d_skip, w_glu, b_glu, w_out, stream.reshape(m, d), gate, tm,
                   *tile_args).reshape(bsz, t_all, d)


def kernel(x, c, ctx, c_ctx, norm_g, w_mod, b_mod, rwkv_mu, rwkv_w_rkvg, rwkv_w0, rwkv_w1, rwkv_w2, rwkv_a0, rwkv_a1, rwkv_a2, rwkv_k_k, rwkv_k_a, rwkv_r_k, rwkv_ln_w, rwkv_ln_b, rwkv_w_out, na_w_in, na_q_g, na_k_g, na_rpb, na_w_out, s5_w_in, s5_lam_re, s5_lam_im, s5_log_dt, s5_b_re, s5_b_im, s5_c_re, s5_c_im, s5_d, s5_w_glu, s5_b_glu, s5_w_out):
    bsz, n_lat, d = x.shape
    n_ctx = ctx.shape[1]
    t_all = n_ctx + n_lat
    depth = norm_g.shape[0]
    grid_w = 64
    tm = _token_tile(n_ctx, t_all)
    tile_args = (t_all // tm, n_ctx // tm, bsz)
    stream = jnp.concatenate([ctx, x], axis=1).astype(F32)
    rows = 8 * ((bsz + 1 + 7) // 8)
    cc = jnp.zeros((rows, d), F32).at[:bsz].set(c.astype(F32)).at[bsz].set(c_ctx.astype(F32))
    for i in range(depth):
        kind, j = i % 3, i // 3
        mod = _modulation(cc, w_mod[i].astype(F32), b_mod[i].astype(F32))[:bsz + 1]
        shift, scale, gate = (mod[:, k * d:(k + 1) * d].reshape(bsz + 1, 1, d) for k in range(3))
        if kind == 0:
            stream = _rwkv_layer(stream, norm_g[i], scale, shift, gate, n_ctx, rwkv_mu[j], rwkv_w_rkvg[j], rwkv_w0[j],
                                 rwkv_w1[j], rwkv_w2[j], rwkv_a0[j], rwkv_a1[j], rwkv_a2[j], rwkv_k_k[j],
                                 rwkv_k_a[j], rwkv_r_k[j], rwkv_ln_w[j], rwkv_ln_b[j], rwkv_w_out[j])
        elif kind == 1:
            stream = _na_layer(stream, norm_g[i], scale, shift, gate, n_ctx, tm, tile_args, grid_w, na_w_in[j],
                               na_q_g[j], na_k_g[j], na_rpb[j], na_w_out[j])
        else:
            stream = _s5_layer(stream, norm_g[i], scale, shift, gate, n_ctx, tm, tile_args, s5_w_in[j],
                               s5_lam_re[j], s5_lam_im[j], s5_log_dt[j], s5_b_re[j], s5_b_im[j], s5_c_re[j],
                               s5_c_im[j], s5_d[j], s5_w_glu[j], s5_b_glu[j], s5_w_out[j])
    return stream[:, n_ctx:].astype(x.dtype)
```

```python
import functools
import math

import numpy as np
import jax
import jax.numpy as jnp
from jax import lax
from jax.experimental import pallas as pl
from jax.experimental.pallas import tpu as pltpu

F32 = jnp.float32
BF16 = jnp.bfloat16
NORM_EPS = 1e-6
RWKV_GN_EPS = 64e-5
HEAD_DIM = 64
RWKV_CHUNK = 64
RWKV_HEADS_PER_STEP = 16
NA_ROWS_PER_STEP = 4
S5_CHUNK = 16
S5_GROUP = 16
MASK_NEG = -1e30
VMEM_LIMIT = 48 * 1024 * 1024
HIGHEST = lax.Precision.HIGHEST


def _cparams(*sem):
    return pltpu.CompilerParams(dimension_semantics=sem, vmem_limit_bytes=VMEM_LIMIT)


def _token_tile(n_ctx, n_all, largest=256):
    for t in (256, 128, 64):
        if t <= largest and n_ctx % t == 0 and n_all % t == 0:
            return t
    raise ValueError("context / sequence lengths must be multiples of 64")


def _mod_row(i, tiles_per_b, ctx_tiles, n_batch):
    return jnp.where(i % tiles_per_b < ctx_tiles, n_batch, i // tiles_per_b)


def _mod_kernel(c_ref, w_ref, b_ref, o_ref):
    c = c_ref[...]
    s = c * jax.nn.sigmoid(c)
    o_ref[...] = jnp.dot(s, w_ref[...], precision=HIGHEST, preferred_element_type=F32) + b_ref[...]


def _modulation(cc, wm, bm):
    rows, d = cc.shape
    n = wm.shape[1]
    tn = 512
    return pl.pallas_call(
        _mod_kernel,
        grid=(n // tn,),
        in_specs=[pl.BlockSpec((rows, d), lambda j: (0, 0)),
                  pl.BlockSpec((d, tn), lambda j: (0, j)),
                  pl.BlockSpec((1, tn), lambda j: (0, j))],
        out_specs=pl.BlockSpec((rows, tn), lambda j: (0, j)),
        out_shape=jax.ShapeDtypeStruct((rows, n), F32),
        compiler_params=_cparams("arbitrary"),
        name="modulation",
    )(cc, wm, bm.reshape(1, n))


def _norm_mm_kernel(x_ref, g_ref, sc_ref, sh_ref, w_ref, o_ref):
    x = x_ref[...]
    ms = jnp.mean(x * x, axis=-1, keepdims=True)
    y = x * lax.rsqrt(ms + NORM_EPS) * g_ref[...]
    h = (y * (1.0 + sc_ref[0]) + sh_ref[0]).astype(BF16)
    o_ref[...] = jnp.dot(h, w_ref[...], preferred_element_type=F32)


def _norm_mm(x2, g, scale, shift, w, tm, tiles_per_b, ctx_tiles, n_batch):
    m, d = x2.shape
    n = w.shape[1]
    row = lambda i: (_mod_row(i, tiles_per_b, ctx_tiles, n_batch), 0, 0)
    return pl.pallas_call(
        _norm_mm_kernel,
        grid=(m // tm,),
        in_specs=[pl.BlockSpec((tm, d), lambda i: (i, 0)),
                  pl.BlockSpec((1, d), lambda i: (0, 0)),
                  pl.BlockSpec((1, 1, d), row),
                  pl.BlockSpec((1, 1, d), row),
                  pl.BlockSpec((d, n), lambda i: (0, 0))],
        out_specs=pl.BlockSpec((tm, n), lambda i: (i, 0)),
        out_shape=jax.ShapeDtypeStruct((m, n), F32),
        compiler_params=_cparams("arbitrary"),
        name="norm_matmul",
    )(x2, g.reshape(1, d), scale, shift, w.astype(BF16))


def _mm_res_kernel(x_ref, w_ref, res_ref, gate_ref, o_ref):
    acc = jnp.dot(x_ref[...].astype(BF16), w_ref[...], preferred_element_type=F32)
    o_ref[...] = res_ref[...] + gate_ref[0] * acc


def _mm_residual(x2, w, res2, gate, tm, tiles_per_b, ctx_tiles, n_batch):
    m, k = x2.shape
    n = w.shape[1]
    return pl.pallas_call(
        _mm_res_kernel,
        grid=(m // tm,),
        in_specs=[pl.BlockSpec((tm, k), lambda i: (i, 0)),
                  pl.BlockSpec((k, n), lambda i: (0, 0)),
                  pl.BlockSpec((tm, n), lambda i: (i, 0)),
                  pl.BlockSpec((1, 1, n), lambda i: (_mod_row(i, tiles_per_b, ctx_tiles, n_batch), 0, 0))],
        out_specs=pl.BlockSpec((tm, n), lambda i: (i, 0)),
        out_shape=jax.ShapeDtypeStruct((m, n), F32),
        compiler_params=_cparams("arbitrary"),
        name="matmul_residual",
    )(x2, w.astype(BF16), res2, gate)


def _dot_nt(a, b):
    return lax.dot_general(a, b, (((1,), (1,)), ((), ())), preferred_element_type=F32)


def _dot_tn(a, b):
    return lax.dot_general(a, b, (((0,), (0,)), ((), ())), preferred_element_type=F32)


def _dot(a, b):
    return jnp.dot(a, b, preferred_element_type=F32)


def _head_sum(x, ones_bd):
    lanes = ones_bd.shape[0]
    return jnp.concatenate(
        [jnp.dot(x[:, p * lanes:(p + 1) * lanes].astype(BF16), ones_bd, preferred_element_type=F32)
         for p in range(x.shape[1] // lanes)], axis=1)


def _same_head(lanes):
    return (lax.broadcasted_iota(jnp.int32, (lanes, lanes), 0) // HEAD_DIM
            == lax.broadcasted_iota(jnp.int32, (lanes, lanes), 1) // HEAD_DIM)


def _rwkv_proj_kernel(x_ref, xp_ref, xn_ref, g_ref, sc_ref, sh_ref, mu_ref, w_ref, w1_ref, w2_ref, a1_ref, a2_ref,
                      w0_ref, a0_ref, kk_ref, ka_ref, rk_ref,
                      r_out, v_out, kk_out, g_out, bonus_out, lw_out, kd_out, bd_out, *, tm, tiles_per_b, ctx_tiles):
    i = pl.program_id(0)
    j = i % tiles_per_b
    gain = g_ref[...]
    sc = 1.0 + sc_ref[0]
    sh = sh_ref[0]

    def norm(x):
        ms = jnp.mean(x * x, axis=-1, keepdims=True)
        return x * lax.rsqrt(ms + NORM_EPS) * gain * sc + sh

    h = norm(x_ref[...])
    has_prev = jnp.logical_and(j != 0, j != ctx_tiles)
    has_next = jnp.logical_and(j != ctx_tiles - 1, j != tiles_per_b - 1)
    h_prev = jnp.where(has_prev, norm(xp_ref[...])[7:8], 0.0)
    h_next = jnp.where(has_next, norm(xn_ref[...])[0:1], 0.0)
    row = lax.broadcasted_iota(jnp.int32, h.shape, 0)
    prev = jnp.where(row == 0, h_prev, pltpu.roll(h, 1, axis=0))
    nxt = jnp.where(row == tm - 1, h_next, pltpu.roll(h, tm - 1, axis=0))
    xx = 0.5 * (prev + nxt) - h
    mix = lambda n: (h + xx * mu_ref[n:n + 1, :]).astype(BF16)

    r = jnp.dot(mix(0), w_ref[0], preferred_element_type=F32)
    k = jnp.dot(mix(1), w_ref[1], preferred_element_type=F32)
    v = jnp.dot(mix(2), w_ref[2], preferred_element_type=F32)
    g_out[...] = jnp.dot(mix(3), w_ref[3], preferred_element_type=F32)
    dec = jnp.dot(jnp.tanh(jnp.dot(mix(4), w1_ref[...], preferred_element_type=F32)).astype(BF16), w2_ref[...],
                  preferred_element_type=F32)
    icl = jnp.dot(jnp.dot(mix(5), a1_ref[...], preferred_element_type=F32).astype(BF16), a2_ref[...],
                  preferred_element_type=F32)
    e = r.shape[1]
    ones_bd = _same_head(2 * HEAD_DIM).astype(BF16)
    kkf = k * kk_ref[...]
    kk = kkf / jnp.maximum(jnp.sqrt(_head_sum(kkf * kkf, ones_bd)), 1e-12)
    r_out[...] = r
    v_out[...] = v
    kk_out[...] = kk
    bonus_out[...] = _head_sum(r * k * rk_ref[...], ones_bd) * v
    for s in range(2):
        lw_out[s] = -math.exp(-0.5) * jax.nn.sigmoid(w0_ref[s:s + 1, :] + dec[:, s * e:(s + 1) * e])
        a = jax.nn.sigmoid(a0_ref[s:s + 1, :] + icl[:, s * e:(s + 1) * e])
        kd_out[s] = k * (1.0 + (a - 1.0) * ka_ref[...])
        bd_out[s] = kk * a


def _rwkv_proj(stream2, norm_g, scale, shift, mu, w_rkvg, w0, w1, w2, a0, a1, a2, k_k, k_a, r_k, tm, tiles_per_b,
               ctx_tiles, n_batch):
    m, d = stream2.shape
    e = w_rkvg.shape[-1]
    lr = w1.shape[-1]
    nblk = m // 8
    zeros = jnp.zeros((lr, e), F32)
    cat = lambda w: jnp.concatenate([w[0], w[1]], axis=1).astype(BF16)
    bdiag = lambda w: jnp.concatenate([jnp.concatenate([w[0], zeros], axis=1),
                                       jnp.concatenate([zeros, w[1]], axis=1)], axis=0).astype(BF16)
    row = lambda i: (_mod_row(i, tiles_per_b, ctx_tiles, n_batch), 0, 0)
    full = lambda shape: pl.BlockSpec(shape, lambda i: (0,) * len(shape))
    tok = pl.BlockSpec((tm, e), lambda i: (i, 0))
    tok2 = pl.BlockSpec((2, tm, e), lambda i: (0, i, 0))
    vec = lambda a: a.reshape(1, e).astype(F32)
    return pl.pallas_call(
        functools.partial(_rwkv_proj_kernel, tm=tm, tiles_per_b=tiles_per_b, ctx_tiles=ctx_tiles),
        grid=(m // tm,),
        in_specs=[pl.BlockSpec((tm, d), lambda i: (i, 0)),
                  pl.BlockSpec((8, d), lambda i: (jnp.maximum(i * (tm // 8) - 1, 0), 0)),
                  pl.BlockSpec((8, d), lambda i: (jnp.minimum((i + 1) * (tm // 8), nblk - 1), 0)),
                  full((1, d)), pl.BlockSpec((1, 1, d), row), pl.BlockSpec((1, 1, d), row),
                  full(mu.shape), full(w_rkvg.shape), full((d, 2 * lr)), full((2 * lr, 2 * e)),
                  full((d, 2 * lr)), full((2 * lr, 2 * e)), full((2, e)), full((2, e)),
                  full((1, e)), full((1, e)), full((1, e))],
        out_specs=[tok, tok, tok, tok, tok, tok2, tok2, tok2],
        out_shape=[jax.ShapeDtypeStruct((m, e), F32)] * 5 + [jax.ShapeDtypeStruct((2, m, e), F32)] * 3,
        compiler_params=_cparams("arbitrary"),
        name="rwkv_proj",
    )(stream2, stream2, stream2, norm_g.reshape(1, d), scale, shift, mu.astype(F32), w_rkvg.astype(BF16),
      cat(w1), bdiag(w2), cat(a1), bdiag(a2), w0.astype(F32), a0.astype(F32), vec(k_k), vec(k_a), vec(r_k))


def _cumsum_rows(x, reverse):
    c = x.shape[0]
    row = lax.broadcasted_iota(jnp.int32, x.shape, 0)
    shift = 1
    while shift < c:
        if reverse:
            x = x + jnp.where(row < c - shift, pltpu.roll(x, c - shift, axis=0), 0.0)
        else:
            x = x + jnp.where(row >= shift, pltpu.roll(x, shift, axis=0), 0.0)
        shift *= 2
    return x


def _rwkv_chunk_streams(streams):
    c = streams[0][0].shape[0]
    hd = HEAD_DIM
    lanes = 2 * hd
    row = lax.broadcasted_iota(jnp.int32, (c, 2 * c), 0)
    col = lax.broadcasted_iota(jnp.int32, (c, 2 * c), 1) % c
    eye2 = (row == col).astype(F32)
    lane_head = lax.broadcasted_iota(jnp.int32, (1, lanes), 1) // hd
    first = lane_head == 0
    same = _same_head(lanes)
    pick = lambda a, p: a[:, p * lanes:(p + 1) * lanes]
    zero = jnp.zeros((), BF16)

    units, incl2, strict2 = [], {}, {}
    lhs, k_h, b_h, k_p, b_p, vb, lp_tot, s_bd = {}, {}, {}, {}, {}, {}, {}, {}
    for si, (r, v, kk, lw, kd, bd, states, reverse) in enumerate(streams):
        incl2[si] = (col >= row) if reverse else (col <= row)
        strict2[si] = (col > row) if reverse else (col < row)
        lp = _cumsum_rows(lw, reverse)
        tot = jnp.sum(lw, axis=0, keepdims=True)
        lhs_f = jnp.concatenate([kk * jnp.exp(lp - lw), r * jnp.exp(lp)], axis=0).astype(BF16)
        e_ninc = jnp.exp(-lp)
        e_rem = jnp.exp(tot - lp)
        full = dict(lhs=lhs_f, k_h=(kd * e_ninc).astype(BF16), b_h=(bd * e_ninc).astype(BF16),
                    k_p=(kd * e_rem).astype(BF16), b_p=(bd * e_rem).astype(BF16), vb=v.astype(BF16), tot=tot)
        for p in range(len(states)):
            u_ = (si, p)
            units.append(u_)
            lhs[u_], k_h[u_], b_h[u_] = pick(full["lhs"], p), pick(full["k_h"], p), pick(full["b_h"], p)
            k_p[u_], b_p[u_], vb[u_] = pick(full["k_p"], p), pick(full["b_p"], p), pick(full["vb"], p)
            lp_tot[u_], s_bd[u_] = pick(full["tot"], p), states[p]
    bd = lambda x: jnp.concatenate([jnp.where(first, x, zero), jnp.where(first, zero, x)], axis=0)

    a_k = {u_: _dot_nt(lhs[u_], bd(k_h[u_])) for u_ in units}
    a_b = {u_: _dot_nt(lhs[u_], bd(b_h[u_])) for u_ in units}
    ls = {u_: _dot_nt(lhs[u_], s_bd[u_].astype(BF16)) for u_ in units}
    av = {u_: ls[u_] + _dot(jnp.concatenate([jnp.where(strict2[u_[0]], a_k[u_][:c], 0.0),
                                             jnp.where(incl2[u_[0]], a_k[u_][c:], 0.0)], axis=0).astype(BF16),
                            bd(vb[u_])) for u_ in units}
    z = {u_: av[u_][:c] for u_ in units}
    y0 = {u_: av[u_][c:] for u_ in units}

    m_neg = {u_: jnp.where(strict2[u_[0]], -a_b[u_][:c], 0.0) for u_ in units}
    t_inv = {u_: eye2 + m_neg[u_] for u_ in units}
    pw = {u_: m_neg[u_].astype(BF16) for u_ in units}
    pw = {u_: _dot(pw[u_], bd(pw[u_])).astype(BF16) for u_ in units}
    for _ in range(int(math.log2(c)) - 2):
        sq = {u_: _dot(jnp.concatenate([pw[u_], t_inv[u_].astype(BF16)], axis=0), bd(pw[u_])) for u_ in units}
        t_inv = {u_: t_inv[u_] + sq[u_][c:] for u_ in units}
        pw = {u_: sq[u_][:c].astype(BF16) for u_ in units}
    t_inv = {u_: t_inv[u_] + _dot(t_inv[u_].astype(BF16), bd(pw[u_])) for u_ in units}

    u = {u_: _dot(t_inv[u_].astype(BF16), bd(z[u_].astype(BF16))) for u_ in units}
    y = {u_: y0[u_] - _dot(jnp.where(incl2[u_[0]], a_b[u_][c:], 0.0).astype(BF16), bd(u[u_].astype(BF16)))
         for u_ in units}
    s1 = {}
    for u_ in units:
        upd = _dot_tn(jnp.concatenate([vb[u_], -u[u_].astype(BF16)], axis=0),
                      jnp.concatenate([k_p[u_], b_p[u_]], axis=0))
        s1[u_] = s_bd[u_] * jnp.exp(lp_tot[u_]) + jnp.where(same, upd, 0.0)
    return [(jnp.concatenate([y[si, p] for p in range(len(st[6]))], axis=1),
             [s1[si, p] for p in range(len(st[6]))]) for si, st in enumerate(streams)]


def _rwkv_scan_kernel(rf_ref, vf_ref, kkf_ref, lwf_ref, kdf_ref, bdf_ref,
                      rb_ref, vb_ref, kkb_ref, lwb_ref, kdb_ref, bdb_ref, yf_ref, yb_ref, s_ref):
    @pl.when(pl.program_id(2) == 0)
    def _():
        s_ref[...] = jnp.zeros_like(s_ref)

    npairs = s_ref.shape[0] // 2
    fwd = (rf_ref[0], vf_ref[0], kkf_ref[0], lwf_ref[0, 0], kdf_ref[0, 0], bdf_ref[0, 0],
           [s_ref[p] for p in range(npairs)], False)
    bwd = (rb_ref[0], vb_ref[0], kkb_ref[0], lwb_ref[0, 0], kdb_ref[0, 0], bdb_ref[0, 0],
           [s_ref[npairs + p] for p in range(npairs)], True)
    (y_f, s_f), (y_b, s_b) = _rwkv_chunk_streams([fwd, bwd])
    yf_ref[0] = y_f
    yb_ref[0] = y_b
    for p in range(npairs):
        s_ref[p] = s_f[p]
        s_ref[npairs + p] = s_b[p]


def _rwkv_scan(r, v, kk, lw, kd, bd, n_ctx):
    bsz, t, e = r.shape
    c = RWKV_CHUNK
    width = RWKV_HEADS_PER_STEP * HEAD_DIM
    nc, nc_ctx = t // c, n_ctx // c
    rev = lambda ci: jnp.where(ci < nc_ctx, nc_ctx - 1 - ci, nc + nc_ctx - 1 - ci)
    tok_f = pl.BlockSpec((1, c, width), lambda b, hg, ci: (b, ci, hg))
    tok_b = pl.BlockSpec((1, c, width), lambda b, hg, ci: (b, rev(ci), hg))
    dir_f = pl.BlockSpec((1, 1, c, width), lambda b, hg, ci: (0, b, ci, hg))
    dir_b = pl.BlockSpec((1, 1, c, width), lambda b, hg, ci: (1, b, rev(ci), hg))
    return pl.pallas_call(
        _rwkv_scan_kernel,
        grid=(bsz, e // width, nc),
        in_specs=[tok_f, tok_f, tok_f, dir_f, dir_f, dir_f, tok_b, tok_b, tok_b, dir_b, dir_b, dir_b],
        out_specs=[tok_f, tok_b],
        out_shape=[jax.ShapeDtypeStruct((bsz, t, e), F32)] * 2,
        scratch_shapes=[pltpu.VMEM((2 * width // (2 * HEAD_DIM), 2 * HEAD_DIM, 2 * HEAD_DIM), F32)],
        compiler_params=_cparams("arbitrary", "arbitrary", "arbitrary"),
        name="rwkv_scan",
    )(r, v, kk, lw, kd, bd, r, v, kk, lw, kd, bd)


def _rwkv_out_kernel(yf_ref, yb_ref, bonus_ref, g_ref, lnw_ref, lnb_ref, w_ref, res_ref, gate_ref, o_ref):
    ones_bd = _same_head(2 * HEAD_DIM).astype(BF16)
    y = yf_ref[...] + yb_ref[...]
    mean = _head_sum(y, ones_bd) * (1.0 / HEAD_DIM)
    yc = y - mean
    var = _head_sum(yc * yc, ones_bd) * (1.0 / HEAD_DIM)
    yn = yc * lax.rsqrt(var + RWKV_GN_EPS) * lnw_ref[...] + lnb_ref[...]
    g = g_ref[...]
    o = ((yn + bonus_ref[...]) * (g * jax.nn.sigmoid(g))).astype(BF16)
    o_ref[...] = res_ref[...] + gate_ref[0] * jnp.dot(o, w_ref[...], preferred_element_type=F32)


def _rwkv_out(y_f, y_b, bonus, g, ln_w, ln_b, w_out, res2, gate, tm, tiles_per_b, ctx_tiles, n_batch):
    m, e = bonus.shape
    d = w_out.shape[1]
    tok = pl.BlockSpec((tm, e), lambda i: (i, 0))
    vec = pl.BlockSpec((1, e), lambda i: (0, 0))
    return pl.pallas_call(
        _rwkv_out_kernel,
        grid=(m // tm,),
        in_specs=[tok, tok, tok, tok, vec, vec,
                  pl.BlockSpec((e, d), lambda i: (0, 0)),
                  pl.BlockSpec((tm, d), lambda i: (i, 0)),
                  pl.BlockSpec((1, 1, d), lambda i: (_mod_row(i, tiles_per_b, ctx_tiles, n_batch), 0, 0))],
        out_specs=pl.BlockSpec((tm, d), lambda i: (i, 0)),
        out_shape=jax.ShapeDtypeStruct((m, d), F32),
        compiler_params=_cparams("arbitrary"),
        name="rwkv_out",
    )(y_f, y_b, bonus, g, ln_w.reshape(1, e).astype(F32), ln_b.reshape(1, e).astype(F32), w_out.astype(BF16),
      res2, gate)


def _rwkv_layer(stream, norm_g, scale, shift, gate, n_ctx, mu, w_rkvg, w0, w1, w2, a0, a1, a2, k_k, k_a, r_k,
                ln_w, ln_b, w_out):
    bsz, t_all, d = stream.shape
    e = w_rkvg.shape[-1]
    m = bsz * t_all
    tm = _token_tile(n_ctx, t_all, 256)
    tile_args = (t_all // tm, n_ctx // tm, bsz)
    s2 = stream.reshape(m, d)
    r, v, kk, g, bonus, lw, kd, bd = _rwkv_proj(s2, norm_g, scale, shift, mu, w_rkvg, w0, w1, w2, a0, a1, a2,
                                                k_k, k_a, r_k, tm, *tile_args)
    b3 = lambda a: a.reshape(bsz, t_all, e)
    b4 = lambda a: a.reshape(2, bsz, t_all, e)
    y_f, y_b = _rwkv_scan(b3(r), b3(v), b3(kk), b4(lw), b4(kd), b4(bd), n_ctx)
    return _rwkv_out(y_f.reshape(m, e), y_b.reshape(m, e), bonus, g, ln_w, ln_b, w_out, s2, gate, tm,
                     *tile_args).reshape(bsz, t_all, d)


def _na_kernel(q_ref, k_ref, v_ref, z_ref, bias_ref, qg_ref, kg_ref, o_ref, kn_ref, vb_ref, *,
               n_ctx, grid_w, kh, khm, rows, scale, rq):
    step = pl.program_id(2)
    ctx_steps = n_ctx // (grid_w * rq)
    hd = HEAD_DIM
    lanes = 2 * hd
    same_head = _same_head(lanes).astype(BF16)
    lane_head = lax.broadcasted_iota(jnp.int32, (1, lanes), 1) // hd
    first = lane_head == 0

    def head_rms(x, g):
        ss = jnp.dot((x * x).astype(BF16), same_head, preferred_element_type=F32)
        return x * lax.rsqrt(ss * (1.0 / hd) + NORM_EPS) * g

    @pl.when(step == 0)
    def _():
        kn_ref[...] = head_rms(k_ref[0], kg_ref[...]).astype(BF16)
        vb_ref[...] = v_ref[0].astype(BF16)

    qn = head_rms(q_ref[0], qg_ref[...]) * scale
    z = z_ref[0]
    zgate = z * jax.nn.sigmoid(z)
    k_ctx = kn_ref[0:n_ctx, :]
    v_ctx = vb_ref[0:n_ctx, :]
    subs = range(rq)
    q2 = {s: jnp.concatenate([jnp.where(lane_head == hh, qn[s * grid_w:(s + 1) * grid_w], 0.0).astype(BF16)
                              for hh in range(2)], axis=0) for s in subs}

    def attend(bands):
        s_c = {s: _dot_nt(q2[s], k_ctx) for s in subs}
        m = {s: jnp.max(s_c[s], axis=-1, keepdims=True) for s in subs}
        if bands is not None:
            s_n = {s: _dot_nt(q2[s], bands[s][0])
                      + jnp.concatenate([bands[s][2](hh) for hh in range(2)], axis=0) for s in subs}
            m = {s: jnp.maximum(m[s], jnp.max(s_n[s], axis=-1, keepdims=True)) for s in subs}
            p_n = {s: jnp.exp(s_n[s] - m[s]) for s in subs}
        p_c = {s: jnp.exp(s_c[s] - m[s]) for s in subs}
        den = {s: jnp.sum(p_c[s], axis=-1, keepdims=True) for s in subs}
        acc = {s: _dot(p_c[s].astype(BF16), v_ctx) for s in subs}
        if bands is not None:
            den = {s: den[s] + jnp.sum(p_n[s], axis=-1, keepdims=True) for s in subs}
            acc = {s: acc[s] + _dot(p_n[s].astype(BF16), bands[s][1]) for s in subs}
        o2 = {s: acc[s] / den[s] for s in subs}
        out = [jnp.where(first, o2[s][:grid_w], o2[s][grid_w:]) for s in subs]
        return jnp.concatenate(out, axis=0) * zgate

    @pl.when(step < ctx_steps)
    def _():
        o_ref[0] = attend(None)

    @pl.when(step >= ctx_steps)
    def _():
        bands = []
        for s in range(rq):
            i = (step - ctx_steps) * rq + s
            r0 = jnp.clip(i - kh // 2, 0, rows - kh)
            start = pl.multiple_of(n_ctx + r0 * grid_w, grid_w)
            base = khm - 1 - (i - r0)
            bias_of = functools.partial(
                lambda hh, base: jnp.concatenate([bias_ref[hh, base + 2 * q] for q in range(kh // 2)], axis=1),
                base=base)
            bands.append((kn_ref[pl.ds(start, kh * grid_w), :], vb_ref[pl.ds(start, kh * grid_w), :], bias_of))
        o_ref[0] = attend(bands)


def _na_bias_table(rpb, grid_w):
    kw = (rpb.shape[2] + 1) // 2
    j = np.arange(grid_w)[:, None]
    c = np.arange(grid_w)[None, :]
    c0 = np.clip(j - kw // 2, 0, grid_w - kw)
    valid = (c >= c0) & (c < c0 + kw)
    onehot = ((c - j + kw - 1)[None] == np.arange(2 * kw - 1)[:, None, None]) & valid[None]
    tiles = jnp.einsum("hab,bjc->hajc", rpb.astype(F32), jnp.asarray(onehot, F32), precision=HIGHEST)
    tiles = tiles + jnp.asarray(np.where(valid, 0.0, MASK_NEG), F32)
    return jnp.concatenate([tiles[:, :-1], tiles[:, 1:]], axis=-1)


def _na_attention(qkvz, rpb, q_g, k_g, n_ctx, grid_w, kh_max):
    bsz, t_all, e4 = qkvz.shape
    e = e4 // 4
    pairs = e // (2 * HEAD_DIM)
    rows = (t_all - n_ctx) // grid_w
    kh = min(kh_max, rows)
    assert kh % 2 == 0 and kh <= kh_max
    tbl = _na_bias_table(rpb, grid_w)
    ctx_tiles = n_ctx // grid_w
    rq = NA_ROWS_PER_STEP if (ctx_tiles % NA_ROWS_PER_STEP == 0 and rows % NA_ROWS_PER_STEP == 0) else 1
    lanes = 2 * HEAD_DIM
    g2 = lambda g: jnp.concatenate([g, g]).reshape(1, lanes).astype(F32)
    tok = lambda col0: pl.BlockSpec((1, rq * grid_w, lanes), lambda b, p, i: (b, i, col0 + p))
    seq = lambda col0: pl.BlockSpec((1, t_all, lanes), lambda b, p, i: (b, 0, col0 + p))
    return pl.pallas_call(
        functools.partial(_na_kernel, n_ctx=n_ctx, grid_w=grid_w, kh=kh, khm=kh_max, rows=rows,
                          scale=HEAD_DIM ** -0.5, rq=rq),
        grid=(bsz, pairs, t_all // (rq * grid_w)),
        in_specs=[tok(0), seq(pairs), seq(2 * pairs), tok(3 * pairs),
                  pl.BlockSpec((2,) + tbl.shape[1:], lambda b, p, i: (p, 0, 0, 0)),
                  pl.BlockSpec((1, lanes), lambda b, p, i: (0, 0)),
                  pl.BlockSpec((1, lanes), lambda b, p, i: (0, 0))],
        out_specs=tok(0),
        out_shape=jax.ShapeDtypeStruct((bsz, t_all, e), F32),
        scratch_shapes=[pltpu.VMEM((t_all, lanes), BF16), pltpu.VMEM((t_all, lanes), BF16)],
        compiler_params=_cparams("arbitrary", "arbitrary", "arbitrary"),
        name="na_attention",
    )(qkvz, qkvz, qkvz, qkvz, tbl, g2(q_g), g2(k_g))


def _na_layer(stream, norm_g, scale, shift, gate, n_ctx, tm, tile_args, grid_w, w_in, q_g, k_g, rpb, w_out):
    bsz, t_all, d = stream.shape
    m = bsz * t_all
    s2 = stream.reshape(m, d)
    qkvz = _norm_mm(s2, norm_g, scale, shift, w_in, tm, *tile_args).reshape(bsz, t_all, w_in.shape[1])
    kh_max = (rpb.shape[1] + 1) // 2
    o = _na_attention(qkvz, rpb, q_g, k_g, n_ctx, grid_w, kh_max)
    return _mm_residual(o.reshape(m, -1), w_out, s2, gate, tm, *tile_args).reshape(bsz, t_all, d)


def _s5_matrices(lam_re, lam_im, log_dt, b_re, b_im, c_re, c_im):
    nt = S5_CHUNK
    g, p, cg = b_re.shape
    tau = jnp.arange(nt + 1, dtype=F32)[:, None, None]
    i_idx = np.arange(nt)
    kers, sels, b_re_cols, b_im_cols, c_re_rows, c_im_rows, a_re, a_im = [], [], [], [], [], [], [], []
    for s in range(2):
        lr, li = lam_re[s].astype(F32), lam_im[s].astype(F32)
        step = jnp.exp(log_dt[s].astype(F32))[:, None]
        mag = jnp.exp(lr * step)
        ar, ai = mag * jnp.cos(li * step), mag * jnp.sin(li * step)
        den = lr * lr + li * li
        qr = ((ar - 1.0) * lr + ai * li) / den
        qi = (ai * lr - (ar - 1.0) * li) / den
        bbr = qr[..., None] * b_re - qi[..., None] * b_im
        bbi = qr[..., None] * b_im + qi[..., None] * b_re
        pmag = jnp.exp(lr * step * tau)
        pr, pi = pmag * jnp.cos(li * step * tau), pmag * jnp.sin(li * step * tau)
        clr = c_re[None] * pr[:, :, None, :] - c_im[None] * pi[:, :, None, :]
        cli = c_re[None] * pi[:, :, None, :] + c_im[None] * pr[:, :, None, :]
        ker = (jnp.einsum("tgop,gpc->tgoc", clr[:nt], bbr, precision=HIGHEST)
               - jnp.einsum("tgop,gpc->tgoc", cli[:nt], bbi, precision=HIGHEST))
        lbr = pr[:, :, :, None] * bbr[None] - pi[:, :, :, None] * bbi[None]
        lbi = pr[:, :, :, None] * bbi[None] + pi[:, :, :, None] * bbr[None]
        kers.append(ker)
        lag = (i_idx[None, :] - i_idx[:, None]) if s == 0 else (i_idx[:, None] - i_idx[None, :])
        sels.append((lag[None] == i_idx[:, None, None]).astype(np.float32))
        inj =(nt - 1 - i_idx) if s == 0 else i_idx
        flat_in = lambda a: a[inj].transpose(1, 0, 3, 2).reshape(g, nt * cg, p)
        b_re_cols.append(flat_in(lbr))
        b_im_cols.append(flat_in(lbi))
        out = (i_idx + 1) if s == 0 else (nt - i_idx)
        flat_out = lambda a: a[out].transpose(1, 3, 0, 2).reshape(g, p, nt * cg)
        c_re_rows.append(flat_out(clr))
        c_im_rows.append(flat_out(-cli))
        a_re.append(pr[nt])
        a_im.append(pi[nt])
    kbig = jnp.einsum("tji,tgoc->gjcio", jnp.asarray(np.concatenate(sels, axis=0)), jnp.concatenate(kers, axis=0),
                      precision=HIGHEST).reshape(g, nt * cg, nt * cg)
    w_in =jnp.concatenate([kbig] + b_re_cols + b_im_cols, axis=2)
    w_out = jnp.concatenate(c_re_rows + c_im_rows, axis=1)
    coef = lambda a: jnp.concatenate(a, axis=-1)
    return w_in, w_out, coef(a_re), coef(a_im)


def _s5_core_kernel(u_ref, win_ref, wout_ref, are_ref, aim_ref, y_ref,
                    x_ref, yi_ref, bre_ref, bim_ref, xa_re, xa_im, xb_re, xb_im, *, nck, nck_ctx):
    nt, cg = S5_CHUNK, S5_GROUP
    gpb = u_ref.shape[1] // cg
    feat = nt * cg
    half = feat // 2
    pst = bre_ref.shape[2]
    lane_blk = lax.broadcasted_iota(jnp.int32, (1, u_ref.shape[1]), 1) // cg

    def regroup(pieces, shift_of, key_of):
        acc = None
        for n, src in enumerate(pieces):
            shift = shift_of(n) % (gpb * cg)
            rolled = pltpu.roll(src, shift, axis=1) if shift else src
            acc = rolled if acc is None else jnp.where(lane_blk == key_of(n), rolled, acc)
        return acc

    for hf in range(nt // gpb):
        toks = [u_ref[pl.ds(gpb * hf + jj, nck, stride=nt), :].astype(BF16) for jj in range(gpb)]
        for g8 in range(gpb):
            x_ref[:, g8 * feat + hf * half:g8 * feat + (hf + 1) * half] = regroup(
                toks, lambda jj: cg * (jj - g8), lambda jj: jj)

    for g8 in range(gpb):
        yb = jnp.dot(x_ref[:, g8 * feat:(g8 + 1) * feat], win_ref[g8], preferred_element_type=F32)
        yi_ref[:, g8 * feat:(g8 + 1) * feat] = yb[:, :feat]
        bre_ref[:, g8, :] = yb[:, feat:feat + pst]
        bim_ref[:, g8, :] = yb[:, feat + pst:feat + 2 * pst]

    dir0 = lax.broadcasted_iota(jnp.int32, (1, pst), 1) < pst // 2
    a_re = are_ref[...]
    a_im = aim_ref[...]

    def body(k, carry):
        xr, xi = carry
        rk = jnp.where(k < nck_ctx, nck_ctx - 1 - k, nck + nck_ctx - 1 - k)
        xa_re[k] = xr
        xa_im[k] = xi
        xb_re[rk] = xr
        xb_im[rk] = xi
        b_r = jnp.where(dir0, bre_ref[k], bre_ref[rk])
        b_i = jnp.where(dir0, bim_ref[k], bim_ref[rk])
        return a_re * xr - a_im * xi + b_r, a_re * xi + a_im * xr + b_i

    zero = jnp.zeros(a_re.shape, F32)
    lax.fori_loop(0, nck, body, (zero, zero))

    for g8 in range(gpb):
        state = jnp.concatenate([jnp.where(dir0, xa_re[:, g8, :], xb_re[:, g8, :]),
                                 jnp.where(dir0, xa_im[:, g8, :], xb_im[:, g8, :])], axis=1).astype(BF16)
        yi_ref[:, g8 * feat:(g8 + 1) * feat] += jnp.dot(state, wout_ref[g8], preferred_element_type=F32)

    for hf in range(nt // gpb):
        grp = [yi_ref[:, g8 * feat + hf * half:g8 * feat + (hf + 1) * half] for g8 in range(gpb)]
        for ii in range(gpb):
            y_ref[pl.ds(gpb * hf + ii, nck, stride=nt), :] = regroup(grp, lambda g8: cg * (g8 - ii), lambda g8: g8)


def _s5_core(uz, w_in, w_out, a_re, a_im, bsz, n_ctx, e):
    m = uz.shape[0]
    t_all = m // bsz
    nt, cg = S5_CHUNK, S5_GROUP
    lanes = 128
    gpb = lanes // cg
    nck, nck_ctx = t_all // nt, n_ctx // nt
    feat = nt * cg
    pst = w_out.shape[1] // 2
    assert w_in.shape[1:] == (feat, feat + 2 * pst) and w_out.shape[1:] == (2 * pst, feat)
    return pl.pallas_call(
        functools.partial(_s5_core_kernel, nck=nck, nck_ctx=nck_ctx),
        grid=(bsz, e // lanes),
        in_specs=[pl.BlockSpec((t_all, lanes), lambda b, q: (b, q)),
                  pl.BlockSpec((gpb,) + w_in.shape[1:], lambda b, q: (q, 0, 0)),
                  pl.BlockSpec((gpb,) + w_out.shape[1:], lambda b, q: (q, 0, 0)),
                  pl.BlockSpec((gpb, pst), lambda b, q: (q, 0)),
                  pl.BlockSpec((gpb, pst), lambda b, q: (q, 0))],
        out_specs=pl.BlockSpec((t_all, lanes), lambda b, q: (b, q)),
        out_shape=jax.ShapeDtypeStruct((m, e), F32),
        scratch_shapes=[pltpu.VMEM((nck, gpb * feat), BF16), pltpu.VMEM((nck, gpb * feat), F32)]
                       + [pltpu.VMEM((nck, gpb, pst), F32)] * 6,
        compiler_params=_cparams("arbitrary", "arbitrary"),
        name="s5_core",
    )(uz, w_in.astype(BF16), w_out.astype(BF16), a_re, a_im)


def _s5_out_kernel(y_ref, u_ref, z_ref, d_ref, wg_ref, bg_ref, w_ref, res_ref, gate_ref, o_ref):
    y = jax.nn.gelu(y_ref[...] + d_ref[...] * u_ref[...])
    y = y * jax.nn.sigmoid(jnp.dot(y.astype(BF16), wg_ref[...], preferred_element_type=F32) + bg_ref[...])
    z = z_ref[...]
    o = (y * (z * jax.nn.sigmoid(z))).astype(BF16)
    o_ref[...] = res_ref[...] + gate_ref[0] * jnp.dot(o, w_ref[...], preferred_element_type=F32)


def _s5_out(y, uz, d_skip, w_glu, b_glu, w_out, res2, gate, tm, tiles_per_b, ctx_tiles, n_batch):
    m, e = y.shape
    d = w_out.shape[1]
    vec = pl.BlockSpec((1, e), lambda i: (0, 0))
    return pl.pallas_call(
        _s5_out_kernel,
        grid=(m // tm,),
        in_specs=[pl.BlockSpec((tm, e), lambda i: (i, 0)),
                  pl.BlockSpec((tm, e), lambda i: (i, 0)),
                  pl.BlockSpec((tm, e), lambda i: (i, 1)),
                  vec, pl.BlockSpec((e, e), lambda i: (0, 0)), vec,
                  pl.BlockSpec((e, d), lambda i: (0, 0)),
                  pl.BlockSpec((tm, d), lambda i: (i, 0)),
                  pl.BlockSpec((1, 1, d), lambda i: (_mod_row(i, tiles_per_b, ctx_tiles, n_batch), 0, 0))],
        out_specs=pl.BlockSpec((tm, d), lambda i: (i, 0)),
        out_shape=jax.ShapeDtypeStruct((m, d), F32),
        compiler_params=_cparams("arbitrary"),
        name="s5_out",
    )(y, uz, uz, d_skip.reshape(1, e).astype(F32), w_glu.astype(BF16), b_glu.reshape(1, e).astype(F32),
      w_out.astype(BF16), res2, gate)


def _s5_layer(stream, norm_g, scale, shift, gate, n_ctx, tm, tile_args, w_in, lam_re, lam_im, log_dt, b_re, b_im,
              c_re, c_im, d_skip, w_glu, b_glu, w_out):
    bsz, t_all, d = stream.shape
    m = bsz * t_all
    e = w_in.shape[1] // 2
    uz = _norm_mm(stream.reshape(m, d), norm_g, scale, shift, w_in, tm, *tile_args)
    k_in, k_out, a_re, a_im = _s5_matrices(lam_re, lam_im, log_dt, b_re, b_im, c_re, c_im)
    y = _s5_core(uz, k_in, k_out, a_re, a_im, bsz, n_ctx, e)
    return _s5_out(y, uz, d_skip, w_glu, b_glu, w_out, stream.reshape(m, d), gate, tm,
                   *tile_args).reshape(bsz, t_all, d)


def kernel(x, c, ctx, c_ctx, norm_g, w_mod, b_mod, rwkv_mu, rwkv_w_rkvg, rwkv_w0, rwkv_w1, rwkv_w2, rwkv_a0, rwkv_a1, rwkv_a2, rwkv_k_k, rwkv_k_a, rwkv_r_k, rwkv_ln_w, rwkv_ln_b, rwkv_w_out, na_w_in, na_q_g, na_k_g, na_rpb, na_w_out, s5_w_in, s5_lam_re, s5_lam_im, s5_log_dt, s5_b_re, s5_b_im, s5_c_re, s5_c_im, s5_d, s5_w_glu, s5_b_glu, s5_w_out):
    bsz, n_lat, d = x.shape
    n_ctx = ctx.shape[1]
    t_all = n_ctx + n_lat
    depth = norm_g.shape[0]
    grid_w = 64
    tm = _token_tile(n_ctx, t_all)
    tile_args = (t_all // tm, n_ctx // tm, bsz)
    stream = jnp.concatenate([ctx, x], axis=1).astype(F32)
    rows = 8 * ((bsz + 1 + 7) // 8)
    cc = jnp.zeros((rows, d), F32).at[:bsz].set(c.astype(F32)).at[bsz].set(c_ctx.astype(F32))
    for i in range(depth):
        kind, j = i % 3, i // 3
        mod = _modulation(cc, w_mod[i].astype(F32), b_mod[i].astype(F32))[:bsz + 1]
        shift, scale, gate = (mod[:, k * d:(k + 1) * d].reshape(bsz + 1, 1, d) for k in range(3))
        if kind == 0:
            stream = _rwkv_layer(stream, norm_g[i], scale, shift, gate, n_ctx, rwkv_mu[j], rwkv_w_rkvg[j], rwkv_w0[j],
                                 rwkv_w1[j], rwkv_w2[j], rwkv_a0[j], rwkv_a1[j], rwkv_a2[j], rwkv_k_k[j],
                                 rwkv_k_a[j], rwkv_r_k[j], rwkv_ln_w[j], rwkv_ln_b[j], rwkv_w_out[j])
        elif kind == 1:
            stream = _na_layer(stream, norm_g[i], scale, shift, gate, n_ctx, tm, tile_args, grid_w, na_w_in[j],
                               na_q_g[j], na_k_g[j], na_rpb[j], na_w_out[j])
        else:
            stream = _s5_layer(stream, norm_g[i], scale, shift, gate, n_ctx, tm, tile_args, s5_w_in[j],
                               s5_lam_re[j], s5_lam_im[j], s5_log_dt[j], s5_b_re[j], s5_b_im[j], s5_c_re[j],
                               s5_c_im[j], s5_d[j], s5_w_glu[j], s5_b_glu[j], s5_w_out[j])
    return stream[:, n_ctx:].astype(x.dtype)
```

```python
import functools
import math

import numpy as np
import jax
import jax.numpy as jnp
from jax import lax
from jax.experimental import pallas as pl
from jax.experimental.pallas import tpu as pltpu

F32 = jnp.float32
BF16 = jnp.bfloat16
NORM_EPS = 1e-6
RWKV_GN_EPS = 64e-5
HEAD_DIM = 64
RWKV_CHUNK = 64
RWKV_HEADS_PER_STEP = 16
NA_ROWS_PER_STEP = 4
S5_CHUNK = 16
S5_GROUP = 16
MASK_NEG = -1e30
VMEM_LIMIT = 48 * 1024 * 1024
HIGHEST = lax.Precision.HIGHEST


def _cparams(*sem):
    return pltpu.CompilerParams(dimension_semantics=sem, vmem_limit_bytes=VMEM_LIMIT)


def _token_tile(n_ctx, n_all, largest=256):
    for t in (256, 128, 64):
        if t <= largest and n_ctx % t == 0 and n_all % t == 0:
            return t
    raise ValueError("context / sequence lengths must be multiples of 64")


def _mod_row(i, tiles_per_b, ctx_tiles, n_batch):
    return jnp.where(i % tiles_per_b < ctx_tiles, n_batch, i // tiles_per_b)


def _mod_kernel(c_ref, w_ref, b_ref, o_ref):
    c = c_ref[...]
    s = c * jax.nn.sigmoid(c)
    o_ref[...] = jnp.dot(s, w_ref[...], precision=HIGHEST, preferred_element_type=F32) + b_ref[...]


def _modulation(cc, wm, bm):
    rows, d = cc.shape
    n = wm.shape[1]
    tn = 512
    return pl.pallas_call(
        _mod_kernel,
        grid=(n // tn,),
        in_specs=[pl.BlockSpec((rows, d), lambda j: (0, 0)),
                  pl.BlockSpec((d, tn), lambda j: (0, j)),
                  pl.BlockSpec((1, tn), lambda j: (0, j))],
        out_specs=pl.BlockSpec((rows, tn), lambda j: (0, j)),
        out_shape=jax.ShapeDtypeStruct((rows, n), F32),
        compiler_params=_cparams("arbitrary"),
        name="modulation",
    )(cc, wm, bm.reshape(1, n))


def _norm_mm_kernel(x_ref, g_ref, sc_ref, sh_ref, w_ref, o_ref):
    x = x_ref[...]
    ms = jnp.mean(x * x, axis=-1, keepdims=True)
    y = x * lax.rsqrt(ms + NORM_EPS) * g_ref[...]
    h = (y * (1.0 + sc_ref[0]) + sh_ref[0]).astype(BF16)
    o_ref[...] = jnp.dot(h, w_ref[...], preferred_element_type=F32)


def _norm_mm(x2, g, scale, shift, w, tm, tiles_per_b, ctx_tiles, n_batch):
    m, d = x2.shape
    n = w.shape[1]
    row = lambda i: (_mod_row(i, tiles_per_b, ctx_tiles, n_batch), 0, 0)
    return pl.pallas_call(
        _norm_mm_kernel,
        grid=(m // tm,),
        in_specs=[pl.BlockSpec((tm, d), lambda i: (i, 0)),
                  pl.BlockSpec((1, d), lambda i: (0, 0)),
                  pl.BlockSpec((1, 1, d), row),
                  pl.BlockSpec((1, 1, d), row),
                  pl.BlockSpec((d, n), lambda i: (0, 0))],
        out_specs=pl.BlockSpec((tm, n), lambda i: (i, 0)),
        out_shape=jax.ShapeDtypeStruct((m, n), F32),
        compiler_params=_cparams("arbitrary"),
        name="norm_matmul",
    )(x2, g.reshape(1, d), scale, shift, w.astype(BF16))


def _mm_res_kernel(x_ref, w_ref, res_ref, gate_ref, o_ref):
    acc = jnp.dot(x_ref[...].astype(BF16), w_ref[...], preferred_element_type=F32)
    o_ref[...] = res_ref[...] + gate_ref[0] * acc


def _out_tiling(m, tm, tiles_per_b, ctx_tiles, n_batch, latent_only):
    if not latent_only:
        return m // tm, (lambda i: i), m
    lat = tiles_per_b - ctx_tiles
    return n_batch * lat, (lambda i: (i // lat) * tiles_per_b + ctx_tiles + i % lat), n_batch * lat * tm


def _mm_residual(x2, w, res2, gate, tm, tiles_per_b, ctx_tiles, n_batch, latent_only=False):
    m, k = x2.shape
    n = w.shape[1]
    steps, tile, rows = _out_tiling(m, tm, tiles_per_b, ctx_tiles, n_batch, latent_only)
    return pl.pallas_call(
        _mm_res_kernel,
        grid=(steps,),
        in_specs=[pl.BlockSpec((tm, k), lambda i: (tile(i), 0)),
                  pl.BlockSpec((k, n), lambda i: (0, 0)),
                  pl.BlockSpec((tm, n), lambda i: (tile(i), 0)),
                  pl.BlockSpec((1, 1, n), lambda i: (_mod_row(tile(i), tiles_per_b, ctx_tiles, n_batch), 0, 0))],
        out_specs=pl.BlockSpec((tm, n), lambda i: (i, 0)),
        out_shape=jax.ShapeDtypeStruct((rows, n), F32),
        compiler_params=_cparams("arbitrary"),
        name="matmul_residual",
    )(x2, w.astype(BF16), res2, gate)


def _dot_nt(a, b):
    return lax.dot_general(a, b, (((1,), (1,)), ((), ())), preferred_element_type=F32)


def _dot_tn(a, b):
    return lax.dot_general(a, b, (((0,), (0,)), ((), ())), preferred_element_type=F32)


def _dot(a, b):
    return jnp.dot(a, b, preferred_element_type=F32)


def _head_sum(x, ones_bd):
    lanes = ones_bd.shape[0]
    return jnp.concatenate(
        [jnp.dot(x[:, p * lanes:(p + 1) * lanes].astype(BF16), ones_bd, preferred_element_type=F32)
         for p in range(x.shape[1] // lanes)], axis=1)


def _same_head(lanes):
    return (lax.broadcasted_iota(jnp.int32, (lanes, lanes), 0) // HEAD_DIM
            == lax.broadcasted_iota(jnp.int32, (lanes, lanes), 1) // HEAD_DIM)


def _rwkv_proj_kernel(x_ref, xp_ref, xn_ref, g_ref, sc_ref, sh_ref, mu_ref, w_ref, w1_ref, w2_ref, a1_ref, a2_ref,
                      w0_ref, a0_ref, kk_ref, ka_ref, rk_ref,
                      r_out, v_out, kk_out, g_out, bonus_out, lw_out, kd_out, bd_out, *, tm, tiles_per_b, ctx_tiles):
    i = pl.program_id(0)
    j = i % tiles_per_b
    gain = g_ref[...]
    sc = 1.0 + sc_ref[0]
    sh = sh_ref[0]

    def norm(x):
        ms = jnp.mean(x * x, axis=-1, keepdims=True)
        return x * lax.rsqrt(ms + NORM_EPS) * gain * sc + sh

    h = norm(x_ref[...])
    has_prev = jnp.logical_and(j != 0, j != ctx_tiles)
    has_next = jnp.logical_and(j != ctx_tiles - 1, j != tiles_per_b - 1)
    h_prev = jnp.where(has_prev, norm(xp_ref[...])[7:8], 0.0)
    h_next = jnp.where(has_next, norm(xn_ref[...])[0:1], 0.0)
    row = lax.broadcasted_iota(jnp.int32, h.shape, 0)
    prev = jnp.where(row == 0, h_prev, pltpu.roll(h, 1, axis=0))
    nxt = jnp.where(row == tm - 1, h_next, pltpu.roll(h, tm - 1, axis=0))
    xx = 0.5 * (prev + nxt) - h
    mix = lambda n: (h + xx * mu_ref[n:n + 1, :]).astype(BF16)

    r = jnp.dot(mix(0), w_ref[0], preferred_element_type=F32)
    k = jnp.dot(mix(1), w_ref[1], preferred_element_type=F32)
    v = jnp.dot(mix(2), w_ref[2], preferred_element_type=F32)
    g_out[...] = jnp.dot(mix(3), w_ref[3], preferred_element_type=F32)
    dec = jnp.dot(jnp.tanh(jnp.dot(mix(4), w1_ref[...], preferred_element_type=F32)).astype(BF16), w2_ref[...],
                  preferred_element_type=F32)
    icl = jnp.dot(jnp.dot(mix(5), a1_ref[...], preferred_element_type=F32).astype(BF16), a2_ref[...],
                  preferred_element_type=F32)
    e = r.shape[1]
    ones_bd = _same_head(2 * HEAD_DIM).astype(BF16)
    kkf = k * kk_ref[...]
    kk = kkf / jnp.maximum(jnp.sqrt(_head_sum(kkf * kkf, ones_bd)), 1e-12)
    r_out[...] = r
    v_out[...] = v
    kk_out[...] = kk
    bonus_out[...] = _head_sum(r * k * rk_ref[...], ones_bd) * v
    for s in range(2):
        lw_out[s] = -math.exp(-0.5) * jax.nn.sigmoid(w0_ref[s:s + 1, :] + dec[:, s * e:(s + 1) * e])
        a = jax.nn.sigmoid(a0_ref[s:s + 1, :] + icl[:, s * e:(s + 1) * e])
        kd_out[s] = k * (1.0 + (a - 1.0) * ka_ref[...])
        bd_out[s] = kk * a


def _rwkv_proj(stream2, norm_g, scale, shift, mu, w_rkvg, w0, w1, w2, a0, a1, a2, k_k, k_a, r_k, tm, tiles_per_b,
               ctx_tiles, n_batch):
    m, d = stream2.shape
    e = w_rkvg.shape[-1]
    lr = w1.shape[-1]
    nblk = m // 8
    zeros = jnp.zeros((lr, e), F32)
    cat = lambda w: jnp.concatenate([w[0], w[1]], axis=1).astype(BF16)
    bdiag = lambda w: jnp.concatenate([jnp.concatenate([w[0], zeros], axis=1),
                                       jnp.concatenate([zeros, w[1]], axis=1)], axis=0).astype(BF16)
    row = lambda i: (_mod_row(i, tiles_per_b, ctx_tiles, n_batch), 0, 0)
    full = lambda shape: pl.BlockSpec(shape, lambda i: (0,) * len(shape))
    tok = pl.BlockSpec((tm, e), lambda i: (i, 0))
    tok2 = pl.BlockSpec((2, tm, e), lambda i: (0, i, 0))
    vec = lambda a: a.reshape(1, e).astype(F32)
    return pl.pallas_call(
        functools.partial(_rwkv_proj_kernel, tm=tm, tiles_per_b=tiles_per_b, ctx_tiles=ctx_tiles),
        grid=(m // tm,),
        in_specs=[pl.BlockSpec((tm, d), lambda i: (i, 0)),
                  pl.BlockSpec((8, d), lambda i: (jnp.maximum(i * (tm // 8) - 1, 0), 0)),
                  pl.BlockSpec((8, d), lambda i: (jnp.minimum((i + 1) * (tm // 8), nblk - 1), 0)),
                  full((1, d)), pl.BlockSpec((1, 1, d), row), pl.BlockSpec((1, 1, d), row),
                  full(mu.shape), full(w_rkvg.shape), full((d, 2 * lr)), full((2 * lr, 2 * e)),
                  full((d, 2 * lr)), full((2 * lr, 2 * e)), full((2, e)), full((2, e)),
                  full((1, e)), full((1, e)), full((1, e))],
        out_specs=[tok, tok, tok, tok, tok, tok2, tok2, tok2],
        out_shape=[jax.ShapeDtypeStruct((m, e), F32)] * 5 + [jax.ShapeDtypeStruct((2, m, e), F32)] * 3,
        compiler_params=_cparams("arbitrary"),
        name="rwkv_proj",
    )(stream2, stream2, stream2, norm_g.reshape(1, d), scale, shift, mu.astype(F32), w_rkvg.astype(BF16),
      cat(w1), bdiag(w2), cat(a1), bdiag(a2), w0.astype(F32), a0.astype(F32), vec(k_k), vec(k_a), vec(r_k))


def _cumsum_rows(x, reverse):
    c = x.shape[0]
    row = lax.broadcasted_iota(jnp.int32, x.shape, 0)
    shift = 1
    while shift < c:
        if reverse:
            x = x + jnp.where(row < c - shift, pltpu.roll(x, c - shift, axis=0), 0.0)
        else:
            x = x + jnp.where(row >= shift, pltpu.roll(x, shift, axis=0), 0.0)
        shift *= 2
    return x


def _rwkv_chunk_streams(streams):
    c = streams[0][0].shape[0]
    hd = HEAD_DIM
    lanes = 2 * hd
    row = lax.broadcasted_iota(jnp.int32, (c, 2 * c), 0)
    col = lax.broadcasted_iota(jnp.int32, (c, 2 * c), 1) % c
    eye2 = (row == col).astype(F32)
    lane_head = lax.broadcasted_iota(jnp.int32, (1, lanes), 1) // hd
    first = lane_head == 0
    same = _same_head(lanes)
    pick = lambda a, p: a[:, p * lanes:(p + 1) * lanes]
    zero = jnp.zeros((), BF16)

    units, incl2, strict2 = [], {}, {}
    lhs, k_h, b_h, k_p, b_p, vb, lp_tot, s_bd = {}, {}, {}, {}, {}, {}, {}, {}
    for si, (r, v, kk, lw, kd, bd, states, reverse) in enumerate(streams):
        incl2[si] = (col >= row) if reverse else (col <= row)
        strict2[si] = (col > row) if reverse else (col < row)
        lp = _cumsum_rows(lw, reverse)
        tot = jnp.sum(lw, axis=0, keepdims=True)
        lhs_f = jnp.concatenate([kk * jnp.exp(lp - lw), r * jnp.exp(lp)], axis=0).astype(BF16)
        e_ninc = jnp.exp(-lp)
        e_rem = jnp.exp(tot - lp)
        full = dict(lhs=lhs_f, k_h=(kd * e_ninc).astype(BF16), b_h=(bd * e_ninc).astype(BF16),
                    k_p=(kd * e_rem).astype(BF16), b_p=(bd * e_rem).astype(BF16), vb=v.astype(BF16), tot=tot)
        for p in range(len(states)):
            u_ = (si, p)
            units.append(u_)
            lhs[u_], k_h[u_], b_h[u_] = pick(full["lhs"], p), pick(full["k_h"], p), pick(full["b_h"], p)
            k_p[u_], b_p[u_], vb[u_] = pick(full["k_p"], p), pick(full["b_p"], p), pick(full["vb"], p)
            lp_tot[u_], s_bd[u_] = pick(full["tot"], p), states[p]
    bd = lambda x: jnp.concatenate([jnp.where(first, x, zero), jnp.where(first, zero, x)], axis=0)

    a_k = {u_: _dot_nt(lhs[u_], bd(k_h[u_])) for u_ in units}
    a_b = {u_: _dot_nt(lhs[u_], bd(b_h[u_])) for u_ in units}
    ls = {u_: _dot_nt(lhs[u_], s_bd[u_].astype(BF16)) for u_ in units}
    av = {u_: ls[u_] + _dot(jnp.concatenate([jnp.where(strict2[u_[0]], a_k[u_][:c], 0.0),
                                             jnp.where(incl2[u_[0]], a_k[u_][c:], 0.0)], axis=0).astype(BF16),
                            bd(vb[u_])) for u_ in units}
    z = {u_: av[u_][:c] for u_ in units}
    y0 = {u_: av[u_][c:] for u_ in units}

    m_neg = {u_: jnp.where(strict2[u_[0]], -a_b[u_][:c], 0.0) for u_ in units}
    t_inv = {u_: eye2 + m_neg[u_] for u_ in units}
    pw = {u_: m_neg[u_].astype(BF16) for u_ in units}
    pw = {u_: _dot(pw[u_], bd(pw[u_])).astype(BF16) for u_ in units}
    for _ in range(int(math.log2(c)) - 2):
        sq = {u_: _dot(jnp.concatenate([pw[u_], t_inv[u_].astype(BF16)], axis=0), bd(pw[u_])) for u_ in units}
        t_inv = {u_: t_inv[u_] + sq[u_][c:] for u_ in units}
        pw = {u_: sq[u_][:c].astype(BF16) for u_ in units}
    t_inv = {u_: t_inv[u_] + _dot(t_inv[u_].astype(BF16), bd(pw[u_])) for u_ in units}

    u = {u_: _dot(t_inv[u_].astype(BF16), bd(z[u_].astype(BF16))) for u_ in units}
    y = {u_: y0[u_] - _dot(jnp.where(incl2[u_[0]], a_b[u_][c:], 0.0).astype(BF16), bd(u[u_].astype(BF16)))
         for u_ in units}
    s1 = {}
    for u_ in units:
        upd = _dot_tn(jnp.concatenate([vb[u_], -u[u_].astype(BF16)], axis=0),
                      jnp.concatenate([k_p[u_], b_p[u_]], axis=0))
        s1[u_] = s_bd[u_] * jnp.exp(lp_tot[u_]) + jnp.where(same, upd, 0.0)
    return [(jnp.concatenate([y[si, p] for p in range(len(st[6]))], axis=1),
             [s1[si, p] for p in range(len(st[6]))]) for si, st in enumerate(streams)]


def _rwkv_scan_kernel(rf_ref, vf_ref, kkf_ref, lwf_ref, kdf_ref, bdf_ref,
                      rb_ref, vb_ref, kkb_ref, lwb_ref, kdb_ref, bdb_ref, yf_ref, yb_ref, s_ref):
    @pl.when(pl.program_id(2) == 0)
    def _():
        s_ref[...] = jnp.zeros_like(s_ref)

    npairs = s_ref.shape[0] // 2
    fwd = (rf_ref[0], vf_ref[0], kkf_ref[0], lwf_ref[0, 0], kdf_ref[0, 0], bdf_ref[0, 0],
           [s_ref[p] for p in range(npairs)], False)
    bwd = (rb_ref[0], vb_ref[0], kkb_ref[0], lwb_ref[0, 0], kdb_ref[0, 0], bdb_ref[0, 0],
           [s_ref[npairs + p] for p in range(npairs)], True)
    (y_f, s_f), (y_b, s_b) = _rwkv_chunk_streams([fwd, bwd])
    yf_ref[0] = y_f
    yb_ref[0] = y_b
    for p in range(npairs):
        s_ref[p] = s_f[p]
        s_ref[npairs + p] = s_b[p]


def _rwkv_scan(r, v, kk, lw, kd, bd, n_ctx):
    bsz, t, e = r.shape
    c = RWKV_CHUNK
    width = RWKV_HEADS_PER_STEP * HEAD_DIM
    nc, nc_ctx = t // c, n_ctx // c
    rev = lambda ci: jnp.where(ci < nc_ctx, nc_ctx - 1 - ci, nc + nc_ctx - 1 - ci)
    tok_f = pl.BlockSpec((1, c, width), lambda b, hg, ci: (b, ci, hg))
    tok_b = pl.BlockSpec((1, c, width), lambda b, hg, ci: (b, rev(ci), hg))
    dir_f = pl.BlockSpec((1, 1, c, width), lambda b, hg, ci: (0, b, ci, hg))
    dir_b = pl.BlockSpec((1, 1, c, width), lambda b, hg, ci: (1, b, rev(ci), hg))
    return pl.pallas_call(
        _rwkv_scan_kernel,
        grid=(bsz, e // width, nc),
        in_specs=[tok_f, tok_f, tok_f, dir_f, dir_f, dir_f, tok_b, tok_b, tok_b, dir_b, dir_b, dir_b],
        out_specs=[tok_f, tok_b],
        out_shape=[jax.ShapeDtypeStruct((bsz, t, e), F32)] * 2,
        scratch_shapes=[pltpu.VMEM((2 * width // (2 * HEAD_DIM), 2 * HEAD_DIM, 2 * HEAD_DIM), F32)],
        compiler_params=_cparams("arbitrary", "arbitrary", "arbitrary"),
        name="rwkv_scan",
    )(r, v, kk, lw, kd, bd, r, v, kk, lw, kd, bd)


def _rwkv_out_kernel(yf_ref, yb_ref, bonus_ref, g_ref, lnw_ref, lnb_ref, w_ref, res_ref, gate_ref, o_ref):
    ones_bd = _same_head(2 * HEAD_DIM).astype(BF16)
    y = yf_ref[...] + yb_ref[...]
    mean = _head_sum(y, ones_bd) * (1.0 / HEAD_DIM)
    yc = y - mean
    var = _head_sum(yc * yc, ones_bd) * (1.0 / HEAD_DIM)
    yn = yc * lax.rsqrt(var + RWKV_GN_EPS) * lnw_ref[...] + lnb_ref[...]
    g = g_ref[...]
    o = ((yn + bonus_ref[...]) * (g * jax.nn.sigmoid(g))).astype(BF16)
    o_ref[...] = res_ref[...] + gate_ref[0] * jnp.dot(o, w_ref[...], preferred_element_type=F32)


def _rwkv_out(y_f, y_b, bonus, g, ln_w, ln_b, w_out, res2, gate, tm, tiles_per_b, ctx_tiles, n_batch,
              latent_only=False):
    m, e = bonus.shape
    d = w_out.shape[1]
    steps, tile, rows = _out_tiling(m, tm, tiles_per_b, ctx_tiles, n_batch, latent_only)
    tok = pl.BlockSpec((tm, e), lambda i: (tile(i), 0))
    vec = pl.BlockSpec((1, e), lambda i: (0, 0))
    return pl.pallas_call(
        _rwkv_out_kernel,
        grid=(steps,),
        in_specs=[tok, tok, tok, tok, vec, vec,
                  pl.BlockSpec((e, d), lambda i: (0, 0)),
                  pl.BlockSpec((tm, d), lambda i: (tile(i), 0)),
                  pl.BlockSpec((1, 1, d), lambda i: (_mod_row(tile(i), tiles_per_b, ctx_tiles, n_batch), 0, 0))],
        out_specs=pl.BlockSpec((tm, d), lambda i: (i, 0)),
        out_shape=jax.ShapeDtypeStruct((rows, d), F32),
        compiler_params=_cparams("arbitrary"),
        name="rwkv_out",
    )(y_f, y_b, bonus, g, ln_w.reshape(1, e).astype(F32), ln_b.reshape(1, e).astype(F32), w_out.astype(BF16),
      res2, gate)


def _rwkv_layer(stream, norm_g, scale, shift, gate, n_ctx, mu, w_rkvg, w0, w1, w2, a0, a1, a2, k_k, k_a, r_k,
                ln_w, ln_b, w_out, last):
    bsz, t_all, d = stream.shape
    e = w_rkvg.shape[-1]
    m = bsz * t_all
    tm = _token_tile(n_ctx, t_all, 256)
    tile_args = (t_all // tm, n_ctx // tm, bsz)
    s2 = stream.reshape(m, d)
    r, v, kk, g, bonus, lw, kd, bd = _rwkv_proj(s2, norm_g, scale, shift, mu, w_rkvg, w0, w1, w2, a0, a1, a2,
                                                k_k, k_a, r_k, tm, *tile_args)
    b3 = lambda a: a.reshape(bsz, t_all, e)
    b4 = lambda a: a.reshape(2, bsz, t_all, e)
    y_f, y_b = _rwkv_scan(b3(r), b3(v), b3(kk), b4(lw), b4(kd), b4(bd), n_ctx)
    return _rwkv_out(y_f.reshape(m, e), y_b.reshape(m, e), bonus, g, ln_w, ln_b, w_out, s2, gate, tm,
                     *tile_args, latent_only=last).reshape(bsz, -1, d)


def _na_kernel(q_ref, k_ref, v_ref, z_ref, bias_ref, qg_ref, kg_ref, o_ref, kn_ref, vb_ref, *,
               n_ctx, grid_w, kh, khm, rows, scale, rq):
    step = pl.program_id(2)
    ctx_steps = n_ctx // (grid_w * rq)
    hd = HEAD_DIM
    lanes = 2 * hd
    same_head = _same_head(lanes).astype(BF16)
    lane_head = lax.broadcasted_iota(jnp.int32, (1, lanes), 1) // hd
    first = lane_head == 0

    def head_rms(x, g):
        ss = jnp.dot((x * x).astype(BF16), same_head, preferred_element_type=F32)
        return x * lax.rsqrt(ss * (1.0 / hd) + NORM_EPS) * g

    @pl.when(step == 0)
    def _():
        kn_ref[...] = head_rms(k_ref[0], kg_ref[...]).astype(BF16)
        vb_ref[...] = v_ref[0].astype(BF16)

    qn = head_rms(q_ref[0], qg_ref[...]) * scale
    z = z_ref[0]
    zgate = z * jax.nn.sigmoid(z)
    k_ctx = kn_ref[0:n_ctx, :]
    v_ctx = vb_ref[0:n_ctx, :]
    subs = range(rq)
    q2 = {s: jnp.concatenate([jnp.where(lane_head == hh, qn[s * grid_w:(s + 1) * grid_w], 0.0).astype(BF16)
                              for hh in range(2)], axis=0) for s in subs}

    def attend(bands):
        s_c = {s: _dot_nt(q2[s], k_ctx) for s in subs}
        m = {s: jnp.max(s_c[s], axis=-1, keepdims=True) for s in subs}
        if bands is not None:
            s_n = {s: _dot_nt(q2[s], bands[s][0])
                      + jnp.concatenate([bands[s][2](hh) for hh in range(2)], axis=0) for s in subs}
            m = {s: jnp.maximum(m[s], jnp.max(s_n[s], axis=-1, keepdims=True)) for s in subs}
            p_n = {s: jnp.exp(s_n[s] - m[s]) for s in subs}
        p_c = {s: jnp.exp(s_c[s] - m[s]) for s in subs}
        den = {s: jnp.sum(p_c[s], axis=-1, keepdims=True) for s in subs}
        acc = {s: _dot(p_c[s].astype(BF16), v_ctx) for s in subs}
        if bands is not None:
            den = {s: den[s] + jnp.sum(p_n[s], axis=-1, keepdims=True) for s in subs}
            acc = {s: acc[s] + _dot(p_n[s].astype(BF16), bands[s][1]) for s in subs}
        o2 = {s: acc[s] / den[s] for s in subs}
        out = [jnp.where(first, o2[s][:grid_w], o2[s][grid_w:]) for s in subs]
        return jnp.concatenate(out, axis=0) * zgate

    @pl.when(step < ctx_steps)
    def _():
        o_ref[0] = attend(None)

    @pl.when(step >= ctx_steps)
    def _():
        bands = []
        for s in range(rq):
            i = (step - ctx_steps) * rq + s
            r0 = jnp.clip(i - kh // 2, 0, rows - kh)
            start = pl.multiple_of(n_ctx + r0 * grid_w, grid_w)
            base = khm - 1 - (i - r0)
            bias_of = functools.partial(
                lambda hh, base: jnp.concatenate([bias_ref[hh, base + 2 * q] for q in range(kh // 2)], axis=1),
                base=base)
            bands.append((kn_ref[pl.ds(start, kh * grid_w), :], vb_ref[pl.ds(start, kh * grid_w), :], bias_of))
        o_ref[0] = attend(bands)


def _na_bias_table(rpb, grid_w):
    kw = (rpb.shape[2] + 1) // 2
    j = np.arange(grid_w)[:, None]
    c = np.arange(grid_w)[None, :]
    c0 = np.clip(j - kw // 2, 0, grid_w - kw)
    valid = (c >= c0) & (c < c0 + kw)
    onehot = ((c - j + kw - 1)[None] == np.arange(2 * kw - 1)[:, None, None]) & valid[None]
    tiles = jnp.einsum("hab,bjc->hajc", rpb.astype(F32), jnp.asarray(onehot, F32), precision=HIGHEST)
    tiles = tiles + jnp.asarray(np.where(valid, 0.0, MASK_NEG), F32)
    return jnp.concatenate([tiles[:, :-1], tiles[:, 1:]], axis=-1)


def _na_attention(qkvz, rpb, q_g, k_g, n_ctx, grid_w, kh_max):
    bsz, t_all, e4 = qkvz.shape
    e = e4 // 4
    pairs = e // (2 * HEAD_DIM)
    rows = (t_all - n_ctx) // grid_w
    kh = min(kh_max, rows)
    assert kh % 2 == 0 and kh <= kh_max
    tbl = _na_bias_table(rpb, grid_w)
    ctx_tiles = n_ctx // grid_w
    rq = NA_ROWS_PER_STEP if (ctx_tiles % NA_ROWS_PER_STEP == 0 and rows % NA_ROWS_PER_STEP == 0) else 1
    lanes = 2 * HEAD_DIM
    g2 = lambda g: jnp.concatenate([g, g]).reshape(1, lanes).astype(F32)
    tok = lambda col0: pl.BlockSpec((1, rq * grid_w, lanes), lambda b, p, i: (b, i, col0 + p))
    seq = lambda col0: pl.BlockSpec((1, t_all, lanes), lambda b, p, i: (b, 0, col0 + p))
    return pl.pallas_call(
        functools.partial(_na_kernel, n_ctx=n_ctx, grid_w=grid_w, kh=kh, khm=kh_max, rows=rows,
                          scale=HEAD_DIM ** -0.5, rq=rq),
        grid=(bsz, pairs, t_all // (rq * grid_w)),
        in_specs=[tok(0), seq(pairs), seq(2 * pairs), tok(3 * pairs),
                  pl.BlockSpec((2,) + tbl.shape[1:], lambda b, p, i: (p, 0, 0, 0)),
                  pl.BlockSpec((1, lanes), lambda b, p, i: (0, 0)),
                  pl.BlockSpec((1, lanes), lambda b, p, i: (0, 0))],
        out_specs=tok(0),
        out_shape=jax.ShapeDtypeStruct((bsz, t_all, e), F32),
        scratch_shapes=[pltpu.VMEM((t_all, lanes), BF16), pltpu.VMEM((t_all, lanes), BF16)],
        compiler_params=_cparams("arbitrary", "arbitrary", "arbitrary"),
        name="na_attention",
    )(qkvz, qkvz, qkvz, qkvz, tbl, g2(q_g), g2(k_g))


def _na_layer(stream, norm_g, scale, shift, gate, n_ctx, tm, tile_args, grid_w, w_in, q_g, k_g, rpb, w_out, last):
    bsz, t_all, d = stream.shape
    m = bsz * t_all
    s2 = stream.reshape(m, d)
    qkvz = _norm_mm(s2, norm_g, scale, shift, w_in, tm, *tile_args).reshape(bsz, t_all, w_in.shape[1])
    kh_max = (rpb.shape[1] + 1) // 2
    o = _na_attention(qkvz, rpb, q_g, k_g, n_ctx, grid_w, kh_max)
    return _mm_residual(o.reshape(m, -1), w_out, s2, gate, tm, *tile_args, latent_only=last).reshape(bsz, -1, d)


def _s5_matrices(lam_re, lam_im, log_dt, b_re, b_im, c_re, c_im):
    nt = S5_CHUNK
    g, p, cg = b_re.shape
    tau = jnp.arange(nt + 1, dtype=F32)[:, None, None]
    i_idx = np.arange(nt)
    kers, sels, b_re_cols, b_im_cols, c_re_rows, c_im_rows, a_re, a_im = [], [], [], [], [], [], [], []
    for s in range(2):
        lr, li = lam_re[s].astype(F32), lam_im[s].astype(F32)
        step = jnp.exp(log_dt[s].astype(F32))[:, None]
        mag = jnp.exp(lr * step)
        ar, ai = mag * jnp.cos(li * step), mag * jnp.sin(li * step)
        den = lr * lr + li * li
        qr = ((ar - 1.0) * lr + ai * li) / den
        qi = (ai * lr - (ar - 1.0) * li) / den
        bbr = qr[..., None] * b_re - qi[..., None] * b_im
        bbi = qr[..., None] * b_im + qi[..., None] * b_re
        pmag = jnp.exp(lr * step * tau)
        pr, pi = pmag * jnp.cos(li * step * tau), pmag * jnp.sin(li * step * tau)
        clr = c_re[None] * pr[:, :, None, :] - c_im[None] * pi[:, :, None, :]
        cli = c_re[None] * pi[:, :, None, :] + c_im[None] * pr[:, :, None, :]
        ker = (jnp.einsum("tgop,gpc->tgoc", clr[:nt], bbr, precision=HIGHEST)
               - jnp.einsum("tgop,gpc->tgoc", cli[:nt], bbi, precision=HIGHEST))
        kers.append(ker)
        lag = (i_idx[None, :] - i_idx[:, None]) if s == 0 else (i_idx[:, None] - i_idx[None, :])
        sels.append((lag[None] == i_idx[:, None, None]).astype(np.float32))
        inj = (nt - 1 - i_idx) if s == 0 else i_idx
        prj, pij = (a[inj].transpose(1, 0, 2)[:, :, None, :] for a in (pr, pi))
        bbr_t, bbi_t = (a.transpose(0, 2, 1)[:, None] for a in (bbr, bbi))
        b_re_cols.append((prj * bbr_t - pij * bbi_t).reshape(g, nt * cg, p))
        b_im_cols.append((prj * bbi_t + pij * bbr_t).reshape(g, nt * cg, p))
        out = (i_idx + 1) if s == 0 else (nt - i_idx)
        pro, pio = (a[out].transpose(1, 2, 0)[:, :, :, None] for a in (pr, pi))
        cre_t, cim_t = (a.transpose(0, 2, 1)[:, :, None, :] for a in (c_re, c_im))
        c_re_rows.append((cre_t * pro - cim_t * pio).reshape(g, p, nt * cg))
        c_im_rows.append((-(cre_t * pio + cim_t * pro)).reshape(g, p, nt * cg))
        a_re.append(pr[nt])
        a_im.append(pi[nt])
    kbig = jnp.einsum("tji,tgoc->gjcio", jnp.asarray(np.concatenate(sels, axis=0)), jnp.concatenate(kers, axis=0),
                      precision=HIGHEST).reshape(g, nt * cg, nt * cg)
    w_in =jnp.concatenate([kbig] + b_re_cols + b_im_cols, axis=2)
    w_out = jnp.concatenate(c_re_rows + c_im_rows, axis=1)
    coef = lambda a: jnp.concatenate(a, axis=-1)
    return w_in, w_out, coef(a_re), coef(a_im)


def _s5_core_kernel(u_ref, win_ref, wout_ref, are_ref, aim_ref, y_ref,
                    x_ref, yi_ref, bre_ref, bim_ref, xa_re, xa_im, xb_re, xb_im, *, nck, nck_ctx):
    nt, cg = S5_CHUNK, S5_GROUP
    gpb = u_ref.shape[1] // cg
    feat = nt * cg
    half = feat // 2
    pst = bre_ref.shape[2]
    lane_blk = lax.broadcasted_iota(jnp.int32, (1, u_ref.shape[1]), 1) // cg

    def regroup(pieces, shift_of, key_of):
        acc = None
        for n, src in enumerate(pieces):
            shift = shift_of(n) % (gpb * cg)
            rolled = pltpu.roll(src, shift, axis=1) if shift else src
            acc = rolled if acc is None else jnp.where(lane_blk == key_of(n), rolled, acc)
        return acc

    for hf in range(nt // gpb):
        toks = [u_ref[pl.ds(gpb * hf + jj, nck, stride=nt), :].astype(BF16) for jj in range(gpb)]
        for g8 in range(gpb):
            x_ref[:, g8 * feat + hf * half:g8 * feat + (hf + 1) * half] = regroup(
                toks, lambda jj: cg * (jj - g8), lambda jj: jj)

    for g8 in range(gpb):
        yb = jnp.dot(x_ref[:, g8 * feat:(g8 + 1) * feat], win_ref[g8], preferred_element_type=F32)
        yi_ref[:, g8 * feat:(g8 + 1) * feat] = yb[:, :feat]
        bre_ref[:, g8, :] = yb[:, feat:feat + pst]
        bim_ref[:, g8, :] = yb[:, feat + pst:feat + 2 * pst]

    dir0 = lax.broadcasted_iota(jnp.int32, (1, pst), 1) < pst // 2
    a_re = are_ref[...]
    a_im = aim_ref[...]

    def body(k, carry):
        xr, xi = carry
        rk = jnp.where(k < nck_ctx, nck_ctx - 1 - k, nck + nck_ctx - 1 - k)
        xa_re[k] = xr
        xa_im[k] = xi
        xb_re[rk] = xr
        xb_im[rk] = xi
        b_r = jnp.where(dir0, bre_ref[k], bre_ref[rk])
        b_i = jnp.where(dir0, bim_ref[k], bim_ref[rk])
        return a_re * xr - a_im * xi + b_r, a_re * xi + a_im * xr + b_i

    zero = jnp.zeros(a_re.shape, F32)
    lax.fori_loop(0, nck, body, (zero, zero))

    for g8 in range(gpb):
        state = jnp.concatenate([jnp.where(dir0, xa_re[:, g8, :], xb_re[:, g8, :]),
                                 jnp.where(dir0, xa_im[:, g8, :], xb_im[:, g8, :])], axis=1).astype(BF16)
        yi_ref[:, g8 * feat:(g8 + 1) * feat] += jnp.dot(state, wout_ref[g8], preferred_element_type=F32)

    for hf in range(nt // gpb):
        grp = [yi_ref[:, g8 * feat + hf * half:g8 * feat + (hf + 1) * half] for g8 in range(gpb)]
        for ii in range(gpb):
            y_ref[pl.ds(gpb * hf + ii, nck, stride=nt), :] = regroup(grp, lambda g8: cg * (g8 - ii), lambda g8: g8)


def _s5_core(uz, w_in, w_out, a_re, a_im, bsz, n_ctx, e):
    m = uz.shape[0]
    t_all = m // bsz
    nt, cg = S5_CHUNK, S5_GROUP
    lanes = 128
    gpb = lanes // cg
    nck, nck_ctx = t_all // nt, n_ctx // nt
    feat = nt * cg
    pst = w_out.shape[1] // 2
    assert w_in.shape[1:] == (feat, feat + 2 * pst) and w_out.shape[1:] == (2 * pst, feat)
    return pl.pallas_call(
        functools.partial(_s5_core_kernel, nck=nck, nck_ctx=nck_ctx),
        grid=(bsz, e // lanes),
        in_specs=[pl.BlockSpec((t_all, lanes), lambda b, q: (b, q)),
                  pl.BlockSpec((gpb,) + w_in.shape[1:], lambda b, q: (q, 0, 0)),
                  pl.BlockSpec((gpb,) + w_out.shape[1:], lambda b, q: (q, 0, 0)),
                  pl.BlockSpec((gpb, pst), lambda b, q: (q, 0)),
                  pl.BlockSpec((gpb, pst), lambda b, q: (q, 0))],
        out_specs=pl.BlockSpec((t_all, lanes), lambda b, q: (b, q)),
        out_shape=jax.ShapeDtypeStruct((m, e), F32),
        scratch_shapes=[pltpu.VMEM((nck, gpb * feat), BF16), pltpu.VMEM((nck, gpb * feat), F32)]
                       + [pltpu.VMEM((nck, gpb, pst), F32)] * 6,
        compiler_params=_cparams("arbitrary", "arbitrary"),
        name="s5_core",
    )(uz, w_in.astype(BF16), w_out.astype(BF16), a_re, a_im)


def _s5_out_kernel(y_ref, u_ref, z_ref, d_ref, wg_ref, bg_ref, w_ref, res_ref, gate_ref, o_ref):
    y = jax.nn.gelu(y_ref[...] + d_ref[...] * u_ref[...])
    y = y * jax.nn.sigmoid(jnp.dot(y.astype(BF16), wg_ref[...], preferred_element_type=F32) + bg_ref[...])
    z = z_ref[...]
    o = (y * (z * jax.nn.sigmoid(z))).astype(BF16)
    o_ref[...] = res_ref[...] + gate_ref[0] * jnp.dot(o, w_ref[...], preferred_element_type=F32)


def _s5_out(y, uz, d_skip, w_glu, b_glu, w_out, res2, gate, tm, tiles_per_b, ctx_tiles, n_batch, latent_only=False):
    m, e = y.shape
    d = w_out.shape[1]
    steps, tile, rows = _out_tiling(m, tm, tiles_per_b, ctx_tiles, n_batch, latent_only)
    vec = pl.BlockSpec((1, e), lambda i: (0, 0))
    return pl.pallas_call(
        _s5_out_kernel,
        grid=(steps,),
        in_specs=[pl.BlockSpec((tm, e), lambda i: (tile(i), 0)),
                  pl.BlockSpec((tm, e), lambda i: (tile(i), 0)),
                  pl.BlockSpec((tm, e), lambda i: (tile(i), 1)),
                  vec, pl.BlockSpec((e, e), lambda i: (0, 0)), vec,
                  pl.BlockSpec((e, d), lambda i: (0, 0)),
                  pl.BlockSpec((tm, d), lambda i: (tile(i), 0)),
                  pl.BlockSpec((1, 1, d), lambda i: (_mod_row(tile(i), tiles_per_b, ctx_tiles, n_batch), 0, 0))],
        out_specs=pl.BlockSpec((tm, d), lambda i: (i, 0)),
        out_shape=jax.ShapeDtypeStruct((rows, d), F32),
        compiler_params=_cparams("arbitrary"),
        name="s5_out",
    )(y, uz, uz, d_skip.reshape(1, e).astype(F32), w_glu.astype(BF16), b_glu.reshape(1, e).astype(F32),
      w_out.astype(BF16), res2, gate)


def _s5_layer(stream, norm_g, scale, shift, gate, n_ctx, tm, tile_args, w_in, lam_re, lam_im, log_dt, b_re, b_im,
              c_re, c_im, d_skip, w_glu, b_glu, w_out, last):
    bsz, t_all, d = stream.shape
    m = bsz * t_all
    e = w_in.shape[1] // 2
    uz = _norm_mm(stream.reshape(m, d), norm_g, scale, shift, w_in, tm, *tile_args)
    k_in, k_out, a_re, a_im = _s5_matrices(lam_re, lam_im, log_dt, b_re, b_im, c_re, c_im)
    y = _s5_core(uz, k_in, k_out, a_re, a_im, bsz, n_ctx, e)
    return _s5_out(y, uz, d_skip, w_glu, b_glu, w_out, stream.reshape(m, d), gate, tm,
                   *tile_args, latent_only=last).reshape(bsz, -1, d)


def kernel(x, c, ctx, c_ctx, norm_g, w_mod, b_mod, rwkv_mu, rwkv_w_rkvg, rwkv_w0, rwkv_w1, rwkv_w2, rwkv_a0, rwkv_a1, rwkv_a2, rwkv_k_k, rwkv_k_a, rwkv_r_k, rwkv_ln_w, rwkv_ln_b, rwkv_w_out, na_w_in, na_q_g, na_k_g, na_rpb, na_w_out, s5_w_in, s5_lam_re, s5_lam_im, s5_log_dt, s5_b_re, s5_b_im, s5_c_re, s5_c_im, s5_d, s5_w_glu, s5_b_glu, s5_w_out):
    bsz, n_lat, d = x.shape
    n_ctx = ctx.shape[1]
    t_all = n_ctx + n_lat
    depth = norm_g.shape[0]
    grid_w = 64
    tm = _token_tile(n_ctx, t_all)
    tile_args = (t_all // tm, n_ctx // tm, bsz)
    stream = jnp.concatenate([ctx, x], axis=1).astype(F32)
    rows = 8 * ((bsz + 1 + 7) // 8)
    cc = jnp.zeros((rows, d), F32).at[:bsz].set(c.astype(F32)).at[bsz].set(c_ctx.astype(F32))
    for i in range(depth):
        kind, j = i % 3, i // 3
        mod = _modulation(cc, w_mod[i].astype(F32), b_mod[i].astype(F32))[:bsz + 1]
        shift, scale, gate = (mod[:, k * d:(k + 1) * d].reshape(bsz + 1, 1, d) for k in range(3))
        last = i == depth - 1
        if kind == 0:
            stream = _rwkv_layer(stream, norm_g[i], scale, shift, gate, n_ctx, rwkv_mu[j], rwkv_w_rkvg[j], rwkv_w0[j],
                                 rwkv_w1[j], rwkv_w2[j], rwkv_a0[j], rwkv_a1[j], rwkv_a2[j], rwkv_k_k[j],
                                 rwkv_k_a[j], rwkv_r_k[j], rwkv_ln_w[j], rwkv_ln_b[j], rwkv_w_out[j], last)
        elif kind == 1:
            stream = _na_layer(stream, norm_g[i], scale, shift, gate, n_ctx, tm, tile_args, grid_w, na_w_in[j],
                               na_q_g[j], na_k_g[j], na_rpb[j], na_w_out[j], last)
        else:
            stream = _s5_layer(stream, norm_g[i], scale, shift, gate, n_ctx, tm, tile_args, s5_w_in[j],
                               s5_lam_re[j], s5_lam_im[j], s5_log_dt[j], s5_b_re[j], s5_b_im[j], s5_c_re[j],
                               s5_c_im[j], s5_d[j], s5_w_glu[j], s5_b_glu[j], s5_w_out[j], last)
    return stream.astype(x.dtype)
```

```python
import functools
import math

import numpy as np
import jax
import jax.numpy as jnp
from jax import lax
from jax.experimental import pallas as pl
from jax.experimental.pallas import tpu as pltpu

F32 = jnp.float32
BF16 = jnp.bfloat16
NORM_EPS = 1e-6
RWKV_GN_EPS = 64e-5
HEAD_DIM = 64
RWKV_CHUNK = 64
RWKV_HEADS_PER_STEP = 16
NA_ROWS_PER_STEP = 4
S5_CHUNK = 16
S5_GROUP = 16
MASK_NEG = -1e30
VMEM_LIMIT = 48 * 1024 * 1024
HIGHEST = lax.Precision.HIGHEST


def _cparams(*sem):
    return pltpu.CompilerParams(dimension_semantics=sem, vmem_limit_bytes=VMEM_LIMIT)


def _token_tile(n_ctx, n_all, largest=256):
    for t in (256, 128, 64):
        if t <= largest and n_ctx % t == 0 and n_all % t == 0:
            return t
    raise ValueError("context / sequence lengths must be multiples of 64")


def _sigmoid(x):
    return 0.5 * jnp.tanh(0.5 * x) + 0.5


def _mod_row(i, tiles_per_b, ctx_tiles, n_batch):
    return jnp.where(i % tiles_per_b < ctx_tiles, n_batch, i // tiles_per_b)


def _mod_kernel(c_ref, w_ref, b_ref, o_ref):
    c = c_ref[...]
    s = c * jax.nn.sigmoid(c)
    o_ref[0] = jnp.dot(s, w_ref[0], precision=HIGHEST, preferred_element_type=F32) + b_ref[0]


def _modulation(cc, w_mod, b_mod):
    rows, d = cc.shape
    depth, _, n = w_mod.shape
    tn = 512
    return pl.pallas_call(
        _mod_kernel,
        grid=(depth, n // tn),
        in_specs=[pl.BlockSpec((rows, d), lambda l, j: (0, 0)),
                  pl.BlockSpec((1, d, tn), lambda l, j: (l, 0, j)),
                  pl.BlockSpec((1, 1, tn), lambda l, j: (l, 0, j))],
        out_specs=pl.BlockSpec((1, rows, tn), lambda l, j: (l, 0, j)),
        out_shape=jax.ShapeDtypeStruct((depth, rows, n), F32),
        compiler_params=_cparams("arbitrary", "arbitrary"),
        name="modulation",
    )(cc, w_mod.astype(F32), b_mod.reshape(depth, 1, n).astype(F32))


def _norm_mm_kernel(x_ref, g_ref, sc_ref, sh_ref, w_ref, o_ref):
    x = x_ref[...]
    ms = jnp.mean(x * x, axis=-1, keepdims=True)
    y = x * lax.rsqrt(ms + NORM_EPS) * g_ref[...]
    h = (y * (1.0 + sc_ref[0]) + sh_ref[0]).astype(BF16)
    o_ref[...] = jnp.dot(h, w_ref[...], preferred_element_type=F32)


def _norm_mm(x2, g, scale, shift, w, tm, tiles_per_b, ctx_tiles, n_batch):
    m, d = x2.shape
    n = w.shape[1]
    row = lambda i: (_mod_row(i, tiles_per_b, ctx_tiles, n_batch), 0, 0)
    return pl.pallas_call(
        _norm_mm_kernel,
        grid=(m // tm,),
        in_specs=[pl.BlockSpec((tm, d), lambda i: (i, 0)),
                  pl.BlockSpec((1, d), lambda i: (0, 0)),
                  pl.BlockSpec((1, 1, d), row),
                  pl.BlockSpec((1, 1, d), row),
                  pl.BlockSpec((d, n), lambda i: (0, 0))],
        out_specs=pl.BlockSpec((tm, n), lambda i: (i, 0)),
        out_shape=jax.ShapeDtypeStruct((m, n), F32),
        compiler_params=_cparams("arbitrary"),
        name="norm_matmul",
    )(x2, g.reshape(1, d), scale, shift, w.astype(BF16))


def _mm_res_kernel(x_ref, w_ref, res_ref, gate_ref, o_ref):
    acc = jnp.dot(x_ref[...].astype(BF16), w_ref[...], preferred_element_type=F32)
    o_ref[...] = res_ref[...] + gate_ref[0] * acc


def _out_tiling(m, tm, tiles_per_b, ctx_tiles, n_batch, latent_only):
    if not latent_only:
        return m // tm, (lambda i: i), m
    lat = tiles_per_b - ctx_tiles
    return n_batch * lat, (lambda i: (i // lat) * tiles_per_b + ctx_tiles + i % lat), n_batch * lat * tm


def _mm_residual(x2, w, res2, gate, tm, tiles_per_b, ctx_tiles, n_batch, latent_only=False):
    m, k = x2.shape
    n = w.shape[1]
    steps, tile, rows = _out_tiling(m, tm, tiles_per_b, ctx_tiles, n_batch, latent_only)
    return pl.pallas_call(
        _mm_res_kernel,
        grid=(steps,),
        in_specs=[pl.BlockSpec((tm, k), lambda i: (tile(i), 0)),
                  pl.BlockSpec((k, n), lambda i: (0, 0)),
                  pl.BlockSpec((tm, n), lambda i: (tile(i), 0)),
                  pl.BlockSpec((1, 1, n), lambda i: (_mod_row(tile(i), tiles_per_b, ctx_tiles, n_batch), 0, 0))],
        out_specs=pl.BlockSpec((tm, n), lambda i: (i, 0)),
        out_shape=jax.ShapeDtypeStruct((rows, n), F32),
        compiler_params=_cparams("arbitrary"),
        name="matmul_residual",
    )(x2, w.astype(BF16), res2, gate)


def _dot_nt(a, b):
    return lax.dot_general(a, b, (((1,), (1,)), ((), ())), preferred_element_type=F32)


def _dot_tn(a, b):
    return lax.dot_general(a, b, (((0,), (0,)), ((), ())), preferred_element_type=F32)


def _dot(a, b):
    return jnp.dot(a, b, preferred_element_type=F32)


def _head_sum(x, ones_bd):
    lanes = ones_bd.shape[0]
    return jnp.concatenate(
        [jnp.dot(x[:, p * lanes:(p + 1) * lanes].astype(BF16), ones_bd, preferred_element_type=F32)
         for p in range(x.shape[1] // lanes)], axis=1)


def _same_head(lanes):
    return (lax.broadcasted_iota(jnp.int32, (lanes, lanes), 0) // HEAD_DIM
            == lax.broadcasted_iota(jnp.int32, (lanes, lanes), 1) // HEAD_DIM)


def _rwkv_proj_kernel(x_ref, xp_ref, xn_ref, g_ref, sc_ref, sh_ref, mu_ref, w_ref, w1_ref, w2_ref, a1_ref, a2_ref,
                      w0_ref, a0_ref, kk_ref, ka_ref, rk_ref,
                      r_out, v_out, kk_out, g_out, bonus_out, lw_out, kd_out, bd_out, *, tm, tiles_per_b, ctx_tiles):
    i = pl.program_id(0)
    j = i % tiles_per_b
    gain = g_ref[...]
    sc = 1.0 + sc_ref[0]
    sh = sh_ref[0]

    def norm(x):
        ms = jnp.mean(x * x, axis=-1, keepdims=True)
        return x * lax.rsqrt(ms + NORM_EPS) * gain * sc + sh

    h = norm(x_ref[...])
    has_prev = jnp.logical_and(j != 0, j != ctx_tiles)
    has_next = jnp.logical_and(j != ctx_tiles - 1, j != tiles_per_b - 1)
    h_prev = jnp.where(has_prev, norm(xp_ref[...])[7:8], 0.0)
    h_next = jnp.where(has_next, norm(xn_ref[...])[0:1], 0.0)
    row = lax.broadcasted_iota(jnp.int32, h.shape, 0)
    prev = jnp.where(row == 0, h_prev, pltpu.roll(h, 1, axis=0))
    nxt = jnp.where(row == tm - 1, h_next, pltpu.roll(h, tm - 1, axis=0))
    xx = 0.5 * (prev + nxt) - h
    mix = lambda n: (h + xx * mu_ref[n:n + 1, :]).astype(BF16)

    r = jnp.dot(mix(0), w_ref[0], preferred_element_type=F32)
    k = jnp.dot(mix(1), w_ref[1], preferred_element_type=F32)
    v = jnp.dot(mix(2), w_ref[2], preferred_element_type=F32)
    g_out[...] = jnp.dot(mix(3), w_ref[3], preferred_element_type=F32)
    dec = jnp.dot(jnp.tanh(jnp.dot(mix(4), w1_ref[...], preferred_element_type=F32)).astype(BF16), w2_ref[...],
                  preferred_element_type=F32)
    icl = jnp.dot(jnp.dot(mix(5), a1_ref[...], preferred_element_type=F32).astype(BF16), a2_ref[...],
                  preferred_element_type=F32)
    e = r.shape[1]
    ones_bd = _same_head(2 * HEAD_DIM).astype(BF16)
    kkf = k * kk_ref[...]
    kk = kkf * lax.rsqrt(jnp.maximum(_head_sum(kkf * kkf, ones_bd), 1e-24))
    r_out[...] = r
    v_out[...] = v
    kk_out[...] = kk
    bonus_out[...] = _head_sum(r * k * rk_ref[...], ones_bd) * v
    for s in range(2):
        lw_out[s] = -math.exp(-0.5) * _sigmoid(w0_ref[s:s + 1, :] + dec[:, s * e:(s + 1) * e])
        a = _sigmoid(a0_ref[s:s + 1, :] + icl[:, s * e:(s + 1) * e])
        kd_out[s] = k * (1.0 + (a - 1.0) * ka_ref[...])
        bd_out[s] = kk * a


def _rwkv_proj(stream2, norm_g, scale, shift, mu, w_rkvg, w0, w1, w2, a0, a1, a2, k_k, k_a, r_k, tm, tiles_per_b,
               ctx_tiles, n_batch):
    m, d = stream2.shape
    e = w_rkvg.shape[-1]
    lr = w1.shape[-1]
    nblk = m // 8
    zeros = jnp.zeros((lr, e), F32)
    cat = lambda w: jnp.concatenate([w[0], w[1]], axis=1).astype(BF16)
    bdiag = lambda w: jnp.concatenate([jnp.concatenate([w[0], zeros], axis=1),
                                       jnp.concatenate([zeros, w[1]], axis=1)], axis=0).astype(BF16)
    row = lambda i: (_mod_row(i, tiles_per_b, ctx_tiles, n_batch), 0, 0)
    full = lambda shape: pl.BlockSpec(shape, lambda i: (0,) * len(shape))
    tok = pl.BlockSpec((tm, e), lambda i: (i, 0))
    tok2 = pl.BlockSpec((2, tm, e), lambda i: (0, i, 0))
    vec = lambda a: a.reshape(1, e).astype(F32)
    return pl.pallas_call(
        functools.partial(_rwkv_proj_kernel, tm=tm, tiles_per_b=tiles_per_b, ctx_tiles=ctx_tiles),
        grid=(m // tm,),
        in_specs=[pl.BlockSpec((tm, d), lambda i: (i, 0)),
                  pl.BlockSpec((8, d), lambda i: (jnp.maximum(i * (tm // 8) - 1, 0), 0)),
                  pl.BlockSpec((8, d), lambda i: (jnp.minimum((i + 1) * (tm // 8), nblk - 1), 0)),
                  full((1, d)), pl.BlockSpec((1, 1, d), row), pl.BlockSpec((1, 1, d), row),
                  full(mu.shape), full(w_rkvg.shape), full((d, 2 * lr)), full((2 * lr, 2 * e)),
                  full((d, 2 * lr)), full((2 * lr, 2 * e)), full((2, e)), full((2, e)),
                  full((1, e)), full((1, e)), full((1, e))],
        out_specs=[tok, tok, tok, tok, tok, tok2, tok2, tok2],
        out_shape=[jax.ShapeDtypeStruct((m, e), F32)] * 5 + [jax.ShapeDtypeStruct((2, m, e), F32)] * 3,
        compiler_params=_cparams("arbitrary"),
        name="rwkv_proj",
    )(stream2, stream2, stream2, norm_g.reshape(1, d), scale, shift, mu.astype(F32), w_rkvg.astype(BF16),
      cat(w1), bdiag(w2), cat(a1), bdiag(a2), w0.astype(F32), a0.astype(F32), vec(k_k), vec(k_a), vec(r_k))


def _cumsum_rows(x, reverse):
    c = x.shape[0]
    row = lax.broadcasted_iota(jnp.int32, x.shape, 0)
    shift = 1
    while shift < c:
        if reverse:
            x = x + jnp.where(row < c - shift, pltpu.roll(x, c - shift, axis=0), 0.0)
        else:
            x = x + jnp.where(row >= shift, pltpu.roll(x, shift, axis=0), 0.0)
        shift *= 2
    return x


def _rwkv_chunk_streams(streams):
    c = streams[0][0].shape[0]
    hd = HEAD_DIM
    lanes = 2 * hd
    row = lax.broadcasted_iota(jnp.int32, (c, 2 * c), 0)
    col = lax.broadcasted_iota(jnp.int32, (c, 2 * c), 1) % c
    eye2 = (row == col).astype(F32)
    lane_head = lax.broadcasted_iota(jnp.int32, (1, lanes), 1) // hd
    first = lane_head == 0
    same = _same_head(lanes)
    pick = lambda a, p: a[:, p * lanes:(p + 1) * lanes]
    zero = jnp.zeros((), BF16)

    units, incl2, strict2 = [], {}, {}
    lhs, k_h, b_h, k_p, b_p, vb, lp_tot, s_bd = {}, {}, {}, {}, {}, {}, {}, {}
    for si, (r, v, kk, lw, kd, bd, states, reverse) in enumerate(streams):
        incl2[si] = (col >= row) if reverse else (col <= row)
        strict2[si] = (col > row) if reverse else (col < row)
        lp = _cumsum_rows(lw, reverse)
        tot = jnp.sum(lw, axis=0, keepdims=True)
        lhs_f = jnp.concatenate([kk * jnp.exp(lp - lw), r * jnp.exp(lp)], axis=0).astype(BF16)
        e_ninc = jnp.exp(-lp)
        e_rem = jnp.exp(tot - lp)
        full = dict(lhs=lhs_f, k_h=(kd * e_ninc).astype(BF16), b_h=(bd * e_ninc).astype(BF16),
                    k_p=(kd * e_rem).astype(BF16), b_p=(bd * e_rem).astype(BF16), vb=v.astype(BF16), tot=tot)
        for p in range(len(states)):
            u_ = (si, p)
            units.append(u_)
            lhs[u_], k_h[u_], b_h[u_] = pick(full["lhs"], p), pick(full["k_h"], p), pick(full["b_h"], p)
            k_p[u_], b_p[u_], vb[u_] = pick(full["k_p"], p), pick(full["b_p"], p), pick(full["vb"], p)
            lp_tot[u_], s_bd[u_] = pick(full["tot"], p), states[p]
    bd = lambda x: jnp.concatenate([jnp.where(first, x, zero), jnp.where(first, zero, x)], axis=0)

    a_k = {u_: _dot_nt(lhs[u_], bd(k_h[u_])) for u_ in units}
    a_b = {u_: _dot_nt(lhs[u_], bd(b_h[u_])) for u_ in units}
    ls = {u_: _dot_nt(lhs[u_], s_bd[u_].astype(BF16)) for u_ in units}
    av = {u_: ls[u_] + _dot(jnp.concatenate([jnp.where(strict2[u_[0]], a_k[u_][:c], 0.0),
                                             jnp.where(incl2[u_[0]], a_k[u_][c:], 0.0)], axis=0).astype(BF16),
                            bd(vb[u_])) for u_ in units}
    z = {u_: av[u_][:c] for u_ in units}
    y0 = {u_: av[u_][c:] for u_ in units}

    m_neg = {u_: jnp.where(strict2[u_[0]], -a_b[u_][:c], 0.0) for u_ in units}
    t_inv = {u_: eye2 + m_neg[u_] for u_ in units}
    pw = {u_: m_neg[u_].astype(BF16) for u_ in units}
    pw = {u_: _dot(pw[u_], bd(pw[u_])).astype(BF16) for u_ in units}
    for _ in range(int(math.log2(c)) - 2):
        sq = {u_: _dot(jnp.concatenate([pw[u_], t_inv[u_].astype(BF16)], axis=0), bd(pw[u_])) for u_ in units}
        t_inv = {u_: t_inv[u_] + sq[u_][c:] for u_ in units}
        pw = {u_: sq[u_][:c].astype(BF16) for u_ in units}
    t_inv = {u_: t_inv[u_] + _dot(t_inv[u_].astype(BF16), bd(pw[u_])) for u_ in units}

    u = {u_: _dot(t_inv[u_].astype(BF16), bd(z[u_].astype(BF16))) for u_ in units}
    y = {u_: y0[u_] - _dot(jnp.where(incl2[u_[0]], a_b[u_][c:], 0.0).astype(BF16), bd(u[u_].astype(BF16)))
         for u_ in units}
    s1 = {}
    for u_ in units:
        upd = _dot_tn(jnp.concatenate([vb[u_], -u[u_].astype(BF16)], axis=0),
                      jnp.concatenate([k_p[u_], b_p[u_]], axis=0))
        s1[u_] = s_bd[u_] * jnp.exp(lp_tot[u_]) + jnp.where(same, upd, 0.0)
    return [(jnp.concatenate([y[si, p] for p in range(len(st[6]))], axis=1),
             [s1[si, p] for p in range(len(st[6]))]) for si, st in enumerate(streams)]


def _rwkv_scan_kernel(rf_ref, vf_ref, kkf_ref, lwf_ref, kdf_ref, bdf_ref,
                      rb_ref, vb_ref, kkb_ref, lwb_ref, kdb_ref, bdb_ref, yf_ref, yb_ref, s_ref):
    @pl.when(pl.program_id(2) == 0)
    def _():
        s_ref[...] = jnp.zeros_like(s_ref)

    npairs = s_ref.shape[0] // 2
    fwd = (rf_ref[0], vf_ref[0], kkf_ref[0], lwf_ref[0, 0], kdf_ref[0, 0], bdf_ref[0, 0],
           [s_ref[p] for p in range(npairs)], False)
    bwd = (rb_ref[0], vb_ref[0], kkb_ref[0], lwb_ref[0, 0], kdb_ref[0, 0], bdb_ref[0, 0],
           [s_ref[npairs + p] for p in range(npairs)], True)
    (y_f, s_f), (y_b, s_b) = _rwkv_chunk_streams([fwd, bwd])
    yf_ref[0] = y_f
    yb_ref[0] = y_b
    for p in range(npairs):
        s_ref[p] = s_f[p]
        s_ref[npairs + p] = s_b[p]


def _rwkv_scan(r, v, kk, lw, kd, bd, n_ctx):
    bsz, t, e = r.shape
    c = RWKV_CHUNK
    width = RWKV_HEADS_PER_STEP * HEAD_DIM
    nc, nc_ctx = t // c, n_ctx // c
    rev = lambda ci: jnp.where(ci < nc_ctx, nc_ctx - 1 - ci, nc + nc_ctx - 1 - ci)
    tok_f = pl.BlockSpec((1, c, width), lambda b, hg, ci: (b, ci, hg))
    tok_b = pl.BlockSpec((1, c, width), lambda b, hg, ci: (b, rev(ci), hg))
    dir_f = pl.BlockSpec((1, 1, c, width), lambda b, hg, ci: (0, b, ci, hg))
    dir_b = pl.BlockSpec((1, 1, c, width), lambda b, hg, ci: (1, b, rev(ci), hg))
    return pl.pallas_call(
        _rwkv_scan_kernel,
        grid=(bsz, e // width, nc),
        in_specs=[tok_f, tok_f, tok_f, dir_f, dir_f, dir_f, tok_b, tok_b, tok_b, dir_b, dir_b, dir_b],
        out_specs=[tok_f, tok_b],
        out_shape=[jax.ShapeDtypeStruct((bsz, t, e), F32)] * 2,
        scratch_shapes=[pltpu.VMEM((2 * width // (2 * HEAD_DIM), 2 * HEAD_DIM, 2 * HEAD_DIM), F32)],
        compiler_params=_cparams("arbitrary", "arbitrary", "arbitrary"),
        name="rwkv_scan",
    )(r, v, kk, lw, kd, bd, r, v, kk, lw, kd, bd)


def _rwkv_out_kernel(yf_ref, yb_ref, bonus_ref, g_ref, lnw_ref, lnb_ref, w_ref, res_ref, gate_ref, o_ref):
    ones_bd = _same_head(2 * HEAD_DIM).astype(BF16)
    y = yf_ref[...] + yb_ref[...]
    mean = _head_sum(y, ones_bd) * (1.0 / HEAD_DIM)
    yc = y - mean
    var = _head_sum(yc * yc, ones_bd) * (1.0 / HEAD_DIM)
    yn = yc * lax.rsqrt(var + RWKV_GN_EPS) * lnw_ref[...] + lnb_ref[...]
    g = g_ref[...]
    o = ((yn + bonus_ref[...]) * (g * _sigmoid(g))).astype(BF16)
    o_ref[...] = res_ref[...] + gate_ref[0] * jnp.dot(o, w_ref[...], preferred_element_type=F32)


def _rwkv_out(y_f, y_b, bonus, g, ln_w, ln_b, w_out, res2, gate, tm, tiles_per_b, ctx_tiles, n_batch,
              latent_only=False):
    m, e = bonus.shape
    d = w_out.shape[1]
    steps, tile, rows = _out_tiling(m, tm, tiles_per_b, ctx_tiles, n_batch, latent_only)
    tok = pl.BlockSpec((tm, e), lambda i: (tile(i), 0))
    vec = pl.BlockSpec((1, e), lambda i: (0, 0))
    return pl.pallas_call(
        _rwkv_out_kernel,
        grid=(steps,),
        in_specs=[tok, tok, tok, tok, vec, vec,
                  pl.BlockSpec((e, d), lambda i: (0, 0)),
                  pl.BlockSpec((tm, d), lambda i: (tile(i), 0)),
                  pl.BlockSpec((1, 1, d), lambda i: (_mod_row(tile(i), tiles_per_b, ctx_tiles, n_batch), 0, 0))],
        out_specs=pl.BlockSpec((tm, d), lambda i: (i, 0)),
        out_shape=jax.ShapeDtypeStruct((rows, d), F32),
        compiler_params=_cparams("arbitrary"),
        name="rwkv_out",
    )(y_f, y_b, bonus, g, ln_w.reshape(1, e).astype(F32), ln_b.reshape(1, e).astype(F32), w_out.astype(BF16),
      res2, gate)


def _rwkv_layer(stream, norm_g, scale, shift, gate, n_ctx, mu, w_rkvg, w0, w1, w2, a0, a1, a2, k_k, k_a, r_k,
                ln_w, ln_b, w_out, last):
    bsz, t_all, d = stream.shape
    e = w_rkvg.shape[-1]
    m = bsz * t_all
    tm = _token_tile(n_ctx, t_all, 256)
    tile_args = (t_all // tm, n_ctx // tm, bsz)
    s2 = stream.reshape(m, d)
    r, v, kk, g, bonus, lw, kd, bd = _rwkv_proj(s2, norm_g, scale, shift, mu, w_rkvg, w0, w1, w2, a0, a1, a2,
                                                k_k, k_a, r_k, tm, *tile_args)
    b3 = lambda a: a.reshape(bsz, t_all, e)
    b4 = lambda a: a.reshape(2, bsz, t_all, e)
    y_f, y_b = _rwkv_scan(b3(r), b3(v), b3(kk), b4(lw), b4(kd), b4(bd), n_ctx)
    return _rwkv_out(y_f.reshape(m, e), y_b.reshape(m, e), bonus, g, ln_w, ln_b, w_out, s2, gate, tm,
                     *tile_args, latent_only=last).reshape(bsz, -1, d)


def _na_kernel(q_ref, k_ref, v_ref, z_ref, bias_ref, qg_ref, kg_ref, o_ref, kn_ref, vb_ref, *,
               n_ctx, grid_w, kh, khm, rows, scale, rq):
    step = pl.program_id(2)
    ctx_steps = n_ctx // (grid_w * rq)
    hd = HEAD_DIM
    lanes = 2 * hd
    same_head = _same_head(lanes).astype(BF16)
    lane_head = lax.broadcasted_iota(jnp.int32, (1, lanes), 1) // hd
    first = lane_head == 0

    def head_rms(x, g):
        ss = jnp.dot((x * x).astype(BF16), same_head, preferred_element_type=F32)
        return x * lax.rsqrt(ss * (1.0 / hd) + NORM_EPS) * g

    @pl.when(step == 0)
    def _():
        kn_ref[...] = head_rms(k_ref[0], kg_ref[...]).astype(BF16)
        vb_ref[...] = v_ref[0].astype(BF16)

    qn = head_rms(q_ref[0], qg_ref[...]) * scale
    z = z_ref[0]
    zgate = z * _sigmoid(z)
    k_ctx = kn_ref[0:n_ctx, :]
    v_ctx = vb_ref[0:n_ctx, :]
    subs = range(rq)
    q2 = {s: jnp.concatenate([jnp.where(lane_head == hh, qn[s * grid_w:(s + 1) * grid_w], 0.0).astype(BF16)
                              for hh in range(2)], axis=0) for s in subs}

    def attend(bands):
        s_c = {s: _dot_nt(q2[s], k_ctx) for s in subs}
        m = {s: jnp.max(s_c[s], axis=-1, keepdims=True) for s in subs}
        if bands is not None:
            s_n = {s: _dot_nt(q2[s], bands[s][0])
                      + jnp.concatenate([bands[s][2](hh) for hh in range(2)], axis=0) for s in subs}
            m = {s: jnp.maximum(m[s], jnp.max(s_n[s], axis=-1, keepdims=True)) for s in subs}
            p_n = {s: jnp.exp(s_n[s] - m[s]) for s in subs}
        p_c = {s: jnp.exp(s_c[s] - m[s]) for s in subs}
        den = {s: jnp.sum(p_c[s], axis=-1, keepdims=True) for s in subs}
        acc = {s: _dot(p_c[s].astype(BF16), v_ctx) for s in subs}
        if bands is not None:
            den = {s: den[s] + jnp.sum(p_n[s], axis=-1, keepdims=True) for s in subs}
            acc = {s: acc[s] + _dot(p_n[s].astype(BF16), bands[s][1]) for s in subs}
        o2 = {s: acc[s] / den[s] for s in subs}
        out = [jnp.where(first, o2[s][:grid_w], o2[s][grid_w:]) for s in subs]
        return jnp.concatenate(out, axis=0) * zgate

    @pl.when(step < ctx_steps)
    def _():
        o_ref[0] = attend(None)

    @pl.when(step >= ctx_steps)
    def _():
        bands = []
        for s in range(rq):
            i = (step - ctx_steps) * rq + s
            r0 = jnp.clip(i - kh // 2, 0, rows - kh)
            start = pl.multiple_of(n_ctx + r0 * grid_w, grid_w)
            base = khm - 1 - (i - r0)
            bias_of = functools.partial(
                lambda hh, base: jnp.concatenate([bias_ref[hh, base + 2 * q] for q in range(kh // 2)], axis=1),
                base=base)
            bands.append((kn_ref[pl.ds(start, kh * grid_w), :], vb_ref[pl.ds(start, kh * grid_w), :], bias_of))
        o_ref[0] = attend(bands)


def _na_bias_table(rpb, grid_w):
    kw = (rpb.shape[2] + 1) // 2
    j = np.arange(grid_w)[:, None]
    c = np.arange(grid_w)[None, :]
    c0 = np.clip(j - kw // 2, 0, grid_w - kw)
    valid = (c >= c0) & (c < c0 + kw)
    onehot = ((c - j + kw - 1)[None] == np.arange(2 * kw - 1)[:, None, None]) & valid[None]
    tiles = jnp.einsum("hab,bjc->hajc", rpb.astype(F32), jnp.asarray(onehot, F32), precision=HIGHEST)
    tiles = tiles + jnp.asarray(np.where(valid, 0.0, MASK_NEG), F32)
    return jnp.concatenate([tiles[:, :-1], tiles[:, 1:]], axis=-1)


def _na_attention(qkvz, rpb, q_g, k_g, n_ctx, grid_w, kh_max):
    bsz, t_all, e4 = qkvz.shape
    e = e4 // 4
    pairs = e // (2 * HEAD_DIM)
    rows = (t_all - n_ctx) // grid_w
    kh = min(kh_max, rows)
    assert kh % 2 == 0 and kh <= kh_max
    tbl = _na_bias_table(rpb, grid_w)
    ctx_tiles = n_ctx // grid_w
    rq = NA_ROWS_PER_STEP if (ctx_tiles % NA_ROWS_PER_STEP == 0 and rows % NA_ROWS_PER_STEP == 0) else 1
    lanes = 2 * HEAD_DIM
    g2 = lambda g: jnp.concatenate([g, g]).reshape(1, lanes).astype(F32)
    tok = lambda col0: pl.BlockSpec((1, rq * grid_w, lanes), lambda b, p, i: (b, i, col0 + p))
    seq = lambda col0: pl.BlockSpec((1, t_all, lanes), lambda b, p, i: (b, 0, col0 + p))
    return pl.pallas_call(
        functools.partial(_na_kernel, n_ctx=n_ctx, grid_w=grid_w, kh=kh, khm=kh_max, rows=rows,
                          scale=HEAD_DIM ** -0.5, rq=rq),
        grid=(bsz, pairs, t_all // (rq * grid_w)),
        in_specs=[tok(0), seq(pairs), seq(2 * pairs), tok(3 * pairs),
                  pl.BlockSpec((2,) + tbl.shape[1:], lambda b, p, i: (p, 0, 0, 0)),
                  pl.BlockSpec((1, lanes), lambda b, p, i: (0, 0)),
                  pl.BlockSpec((1, lanes), lambda b, p, i: (0, 0))],
        out_specs=tok(0),
        out_shape=jax.ShapeDtypeStruct((bsz, t_all, e), F32),
        scratch_shapes=[pltpu.VMEM((t_all, lanes), BF16), pltpu.VMEM((t_all, lanes), BF16)],
        compiler_params=_cparams("arbitrary", "arbitrary", "arbitrary"),
        name="na_attention",
    )(qkvz, qkvz, qkvz, qkvz, tbl, g2(q_g), g2(k_g))


def _na_layer(stream, norm_g, scale, shift, gate, n_ctx, tm, tile_args, grid_w, w_in, q_g, k_g, rpb, w_out, last):
    bsz, t_all, d = stream.shape
    m = bsz * t_all
    s2 = stream.reshape(m, d)
    qkvz = _norm_mm(s2, norm_g, scale, shift, w_in, tm, *tile_args).reshape(bsz, t_all, w_in.shape[1])
    kh_max = (rpb.shape[1] + 1) // 2
    o = _na_attention(qkvz, rpb, q_g, k_g, n_ctx, grid_w, kh_max)
    return _mm_residual(o.reshape(m, -1), w_out, s2, gate, tm, *tile_args, latent_only=last).reshape(bsz, -1, d)


def _s5_matrices(lam_re, lam_im, log_dt, b_re, b_im, c_re, c_im):
    nt = S5_CHUNK
    g, p, cg = b_re.shape
    tau = jnp.arange(nt + 1, dtype=F32)[:, None, None]
    i_idx = np.arange(nt)
    kers, sels, b_re_cols, b_im_cols, c_re_rows, c_im_rows, a_re, a_im = [], [], [], [], [], [], [], []
    for s in range(2):
        lr, li = lam_re[s].astype(F32), lam_im[s].astype(F32)
        step = jnp.exp(log_dt[s].astype(F32))[:, None]
        mag = jnp.exp(lr * step)
        ar, ai = mag * jnp.cos(li * step), mag * jnp.sin(li * step)
        den = lr * lr + li * li
        qr = ((ar - 1.0) * lr + ai * li) / den
        qi = (ai * lr - (ar - 1.0) * li) / den
        bbr = qr[..., None] * b_re - qi[..., None] * b_im
        bbi = qr[..., None] * b_im + qi[..., None] * b_re
        pmag = jnp.exp(lr * step * tau)
        pr, pi = pmag * jnp.cos(li * step * tau), pmag * jnp.sin(li * step * tau)
        clr = c_re[None] * pr[:, :, None, :] - c_im[None] * pi[:, :, None, :]
        cli = c_re[None] * pi[:, :, None, :] + c_im[None] * pr[:, :, None, :]
        ker = jnp.sum(clr[:nt, :, :, None, :] * bbr.transpose(0, 2, 1)[None, :, None]
                      - cli[:nt, :, :, None, :] * bbi.transpose(0, 2, 1)[None, :, None], axis=-1)
        kers.append(ker)
        lag = (i_idx[None, :] - i_idx[:, None]) if s == 0 else (i_idx[:, None] - i_idx[None, :])
        sels.append((lag[None] == i_idx[:, None, None]).astype(np.float32))
        inj = (nt - 1 - i_idx) if s == 0 else i_idx
        prj, pij = (a[inj].transpose(1, 0, 2)[:, :, None, :] for a in (pr, pi))
        bbr_t, bbi_t = (a.transpose(0, 2, 1)[:, None] for a in (bbr, bbi))
        b_re_cols.append((prj * bbr_t - pij * bbi_t).reshape(g, nt * cg, p))
        b_im_cols.append((prj * bbi_t + pij * bbr_t).reshape(g, nt * cg, p))
        out = (i_idx + 1) if s == 0 else (nt - i_idx)
        pro, pio = (a[out].transpose(1, 2, 0)[:, :, :, None] for a in (pr, pi))
        cre_t, cim_t = (a.transpose(0, 2, 1)[:, :, None, :] for a in (c_re, c_im))
        c_re_rows.append((cre_t * pro - cim_t * pio).reshape(g, p, nt * cg))
        c_im_rows.append((-(cre_t * pio + cim_t * pro)).reshape(g, p, nt * cg))
        a_re.append(pr[nt])
        a_im.append(pi[nt])
    kbig = jnp.einsum("tji,tgoc->gjcio", jnp.asarray(np.concatenate(sels, axis=0)), jnp.concatenate(kers, axis=0),
                      precision=HIGHEST).reshape(g, nt * cg, nt * cg)
    w_inj = jnp.concatenate(b_re_cols + b_im_cols, axis=2)
    w_out = jnp.concatenate(c_re_rows + c_im_rows, axis=1)
    coef = lambda a: jnp.concatenate(a, axis=-1)
    return kbig, w_inj, w_out, coef(a_re), coef(a_im)


def _s5_core_kernel(u_ref, wk_ref, winj_ref, wout_ref, are_ref, aim_ref, y_ref,
                    x_ref, yi_ref, bre_ref, bim_ref, xa_re, xa_im, xb_re, xb_im, *, nck, nck_ctx):
    nt, cg = S5_CHUNK, S5_GROUP
    gpb = u_ref.shape[1] // cg
    feat = nt * cg
    half = feat // 2
    pst = bre_ref.shape[2]
    lane_blk = lax.broadcasted_iota(jnp.int32, (1, u_ref.shape[1]), 1) // cg

    def regroup(pieces, shift_of, key_of):
        acc = None
        for n, src in enumerate(pieces):
            shift = shift_of(n) % (gpb * cg)
            rolled = pltpu.roll(src, shift, axis=1) if shift else src
            acc = rolled if acc is None else jnp.where(lane_blk == key_of(n), rolled, acc)
        return acc

    for hf in range(nt // gpb):
        toks = [u_ref[pl.ds(gpb * hf + jj, nck, stride=nt), :].astype(BF16) for jj in range(gpb)]
        for g8 in range(gpb):
            x_ref[:, g8 * feat + hf * half:g8 * feat + (hf + 1) * half] = regroup(
                toks, lambda jj: cg * (jj - g8), lambda jj: jj)

    for g8 in range(gpb):
        xg = x_ref[:, g8 * feat:(g8 + 1) * feat]
        yi_ref[:, g8 * feat:(g8 + 1) * feat] = jnp.dot(xg, wk_ref[g8], preferred_element_type=F32)
        inj = jnp.dot(xg, winj_ref[g8], preferred_element_type=F32)
        bre_ref[:, g8, :] = inj[:, :pst]
        bim_ref[:, g8, :] = inj[:, pst:]

    dir0 = lax.broadcasted_iota(jnp.int32, (1, pst), 1) < pst // 2
    a_re = are_ref[...]
    a_im = aim_ref[...]

    def body(k, carry):
        xr, xi = carry
        rk = jnp.where(k < nck_ctx, nck_ctx - 1 - k, nck + nck_ctx - 1 - k)
        xa_re[k] = xr
        xa_im[k] = xi
        xb_re[rk] = xr
        xb_im[rk] = xi
        b_r = jnp.where(dir0, bre_ref[k], bre_ref[rk])
        b_i = jnp.where(dir0, bim_ref[k], bim_ref[rk])
        return a_re * xr - a_im * xi + b_r, a_re * xi + a_im * xr + b_i

    zero = jnp.zeros(a_re.shape, F32)
    lax.fori_loop(0, nck, body, (zero, zero))

    for g8 in range(gpb):
        state = jnp.concatenate([jnp.where(dir0, xa_re[:, g8, :], xb_re[:, g8, :]),
                                 jnp.where(dir0, xa_im[:, g8, :], xb_im[:, g8, :])], axis=1).astype(BF16)
        yi_ref[:, g8 * feat:(g8 + 1) * feat] += jnp.dot(state, wout_ref[g8], preferred_element_type=F32)

    for hf in range(nt // gpb):
        grp = [yi_ref[:, g8 * feat + hf * half:g8 * feat + (hf + 1) * half] for g8 in range(gpb)]
        for ii in range(gpb):
            y_ref[pl.ds(gpb * hf + ii, nck, stride=nt), :] = regroup(grp, lambda g8: cg * (g8 - ii), lambda g8: g8)


def _s5_core(uz, w_k, w_inj, w_out, a_re, a_im, bsz, n_ctx, e):
    m = uz.shape[0]
    t_all = m // bsz
    nt, cg = S5_CHUNK, S5_GROUP
    lanes = 128
    gpb = lanes // cg
    nck, nck_ctx = t_all // nt, n_ctx // nt
    feat = nt * cg
    pst = w_out.shape[1] // 2
    assert w_k.shape[1:] == (feat, feat) and w_inj.shape[1:] == (feat, 2 * pst) and w_out.shape[1:] == (2 * pst, feat)
    return pl.pallas_call(
        functools.partial(_s5_core_kernel, nck=nck, nck_ctx=nck_ctx),
        grid=(bsz, e // lanes),
        in_specs=[pl.BlockSpec((t_all, lanes), lambda b, q: (b, q)),
                  pl.BlockSpec((gpb,) + w_k.shape[1:], lambda b, q: (q, 0, 0)),
                  pl.BlockSpec((gpb,) + w_inj.shape[1:], lambda b, q: (q, 0, 0)),
                  pl.BlockSpec((gpb,) + w_out.shape[1:], lambda b, q: (q, 0, 0)),
                  pl.BlockSpec((gpb, pst), lambda b, q: (q, 0)),
                  pl.BlockSpec((gpb, pst), lambda b, q: (q, 0))],
        out_specs=pl.BlockSpec((t_all, lanes), lambda b, q: (b, q)),
        out_shape=jax.ShapeDtypeStruct((m, e), F32),
        scratch_shapes=[pltpu.VMEM((nck, gpb * feat), BF16), pltpu.VMEM((nck, gpb * feat), F32)]
                       + [pltpu.VMEM((nck, gpb, pst), F32)] * 6,
        compiler_params=_cparams("arbitrary", "arbitrary"),
        name="s5_core",
    )(uz, w_k.astype(BF16), w_inj.astype(BF16), w_out.astype(BF16), a_re, a_im)


def _s5_out_kernel(y_ref, u_ref, z_ref, d_ref, wg_ref, bg_ref, w_ref, res_ref, gate_ref, o_ref):
    y = jax.nn.gelu(y_ref[...] + d_ref[...] * u_ref[...])
    y = y * _sigmoid(jnp.dot(y.astype(BF16), wg_ref[...], preferred_element_type=F32) + bg_ref[...])
    z = z_ref[...]
    o = (y * (z * _sigmoid(z))).astype(BF16)
    o_ref[...] = res_ref[...] + gate_ref[0] * jnp.dot(o, w_ref[...], preferred_element_type=F32)


def _s5_out(y, uz, d_skip, w_glu, b_glu, w_out, res2, gate, tm, tiles_per_b, ctx_tiles, n_batch, latent_only=False):
    m, e = y.shape
    d = w_out.shape[1]
    steps, tile, rows = _out_tiling(m, tm, tiles_per_b, ctx_tiles, n_batch, latent_only)
    vec = pl.BlockSpec((1, e), lambda i: (0, 0))
    return pl.pallas_call(
        _s5_out_kernel,
        grid=(steps,),
        in_specs=[pl.BlockSpec((tm, e), lambda i: (tile(i), 0)),
                  pl.BlockSpec((tm, e), lambda i: (tile(i), 0)),
                  pl.BlockSpec((tm, e), lambda i: (tile(i), 1)),
                  vec, pl.BlockSpec((e, e), lambda i: (0, 0)), vec,
                  pl.BlockSpec((e, d), lambda i: (0, 0)),
                  pl.BlockSpec((tm, d), lambda i: (tile(i), 0)),
                  pl.BlockSpec((1, 1, d), lambda i: (_mod_row(tile(i), tiles_per_b, ctx_tiles, n_batch), 0, 0))],
        out_specs=pl.BlockSpec((tm, d), lambda i: (i, 0)),
        out_shape=jax.ShapeDtypeStruct((rows, d), F32),
        compiler_params=_cparams("arbitrary"),
        name="s5_out",
    )(y, uz, uz, d_skip.reshape(1, e).astype(F32), w_glu.astype(BF16), b_glu.reshape(1, e).astype(F32),
      w_out.astype(BF16), res2, gate)


def _s5_layer(stream, norm_g, scale, shift, gate, n_ctx, tm, tile_args, w_in, lam_re, lam_im, log_dt, b_re, b_im,
              c_re, c_im, d_skip, w_glu, b_glu, w_out, last):
    bsz, t_all, d = stream.shape
    m = bsz * t_all
    e = w_in.shape[1] // 2
    uz = _norm_mm(stream.reshape(m, d), norm_g, scale, shift, w_in, tm, *tile_args)
    k_big, k_inj, k_out, a_re, a_im = _s5_matrices(lam_re, lam_im, log_dt, b_re, b_im, c_re, c_im)
    y = _s5_core(uz, k_big, k_inj, k_out, a_re, a_im, bsz, n_ctx, e)
    return _s5_out(y, uz, d_skip, w_glu, b_glu, w_out, stream.reshape(m, d), gate, tm,
                   *tile_args, latent_only=last).reshape(bsz, -1, d)


def kernel(x, c, ctx, c_ctx, norm_g, w_mod, b_mod, rwkv_mu, rwkv_w_rkvg, rwkv_w0, rwkv_w1, rwkv_w2, rwkv_a0, rwkv_a1, rwkv_a2, rwkv_k_k, rwkv_k_a, rwkv_r_k, rwkv_ln_w, rwkv_ln_b, rwkv_w_out, na_w_in, na_q_g, na_k_g, na_rpb, na_w_out, s5_w_in, s5_lam_re, s5_lam_im, s5_log_dt, s5_b_re, s5_b_im, s5_c_re, s5_c_im, s5_d, s5_w_glu, s5_b_glu, s5_w_out):
    bsz, n_lat, d = x.shape
    n_ctx = ctx.shape[1]
    t_all = n_ctx + n_lat
    depth = norm_g.shape[0]
    grid_w = 64
    tm = _token_tile(n_ctx, t_all)
    tile_args = (t_all // tm, n_ctx // tm, bsz)
    stream = jnp.concatenate([ctx, x], axis=1).astype(F32)
    rows = 8 * ((bsz + 1 + 7) // 8)
    cc = jnp.zeros((rows, d), F32).at[:bsz].set(c.astype(F32)).at[bsz].set(c_ctx.astype(F32))
    mod_all = _modulation(cc, w_mod, b_mod)
    for i in range(depth):
        kind, j = i % 3, i // 3
        mod = mod_all[i, :bsz + 1]
        shift, scale, gate = (mod[:, k * d:(k + 1) * d].reshape(bsz + 1, 1, d) for k in range(3))
        last = i == depth - 1
        if kind == 0:
            stream = _rwkv_layer(stream, norm_g[i], scale, shift, gate, n_ctx, rwkv_mu[j], rwkv_w_rkvg[j], rwkv_w0[j],
                                 rwkv_w1[j], rwkv_w2[j], rwkv_a0[j], rwkv_a1[j], rwkv_a2[j], rwkv_k_k[j],
                                 rwkv_k_a[j], rwkv_r_k[j], rwkv_ln_w[j], rwkv_ln_b[j], rwkv_w_out[j], last)
        elif kind == 1:
            stream = _na_layer(stream, norm_g[i], scale, shift, gate, n_ctx, tm, tile_args, grid_w, na_w_in[j],
                               na_q_g[j], na_k_g[j], na_rpb[j], na_w_out[j], last)
        else:
            stream = _s5_layer(stream, norm_g[i], scale, shift, gate, n_ctx, tm, tile_args, s5_w_in[j],
                               s5_lam_re[j], s5_lam_im[j], s5_log_dt[j], s5_b_re[j], s5_b_im[j], s5_c_re[j],
                               s5_c_im[j], s5_d[j], s5_w_glu[j], s5_b_glu[j], s5_w_out[j], last)
    return stream.astype(x.dtype)
```

```python
import functools
import math

import numpy as np
import jax
import jax.numpy as jnp
from jax import lax
from jax.experimental import pallas as pl
from jax.experimental.pallas import tpu as pltpu

F32 = jnp.float32
BF16 = jnp.bfloat16
NORM_EPS = 1e-6
RWKV_GN_EPS = 64e-5
HEAD_DIM = 64
RWKV_CHUNK = 64
NA_ROWS_PER_STEP = 4
S5_CHUNK = 16
S5_GROUP = 16
MASK_NEG = -1e30
VMEM_LIMIT = 48 * 1024 * 1024
HIGHEST = lax.Precision.HIGHEST


def _cparams(*sem):
    return pltpu.CompilerParams(dimension_semantics=sem, vmem_limit_bytes=VMEM_LIMIT)


def _token_tile(n_ctx, n_all, largest=256):
    for t in (256, 128, 64):
        if t <= largest and n_ctx % t == 0 and n_all % t == 0:
            return t
    raise ValueError("context / sequence lengths must be multiples of 64")


def _sigmoid(x):
    return 0.5 * jnp.tanh(0.5 * x) + 0.5


def _mod_row(i, tiles_per_b, ctx_tiles, n_batch):
    return jnp.where(i % tiles_per_b < ctx_tiles, n_batch, i // tiles_per_b)


def _mod_kernel(c_ref, w_ref, b_ref, o_ref):
    c = c_ref[...]
    s = c * jax.nn.sigmoid(c)
    o_ref[0] = jnp.dot(s, w_ref[0], precision=HIGHEST, preferred_element_type=F32) + b_ref[0]


def _modulation(cc, w_mod, b_mod):
    rows, d = cc.shape
    depth, _, n = w_mod.shape
    tn = 512
    return pl.pallas_call(
        _mod_kernel,
        grid=(depth, n // tn),
        in_specs=[pl.BlockSpec((rows, d), lambda l, j: (0, 0)),
                  pl.BlockSpec((1, d, tn), lambda l, j: (l, 0, j)),
                  pl.BlockSpec((1, 1, tn), lambda l, j: (l, 0, j))],
        out_specs=pl.BlockSpec((1, rows, tn), lambda l, j: (l, 0, j)),
        out_shape=jax.ShapeDtypeStruct((depth, rows, n), F32),
        compiler_params=_cparams("arbitrary", "arbitrary"),
        name="modulation",
    )(cc, w_mod.astype(F32), b_mod.reshape(depth, 1, n).astype(F32))


def _norm_mm_kernel(x_ref, g_ref, sc_ref, sh_ref, w_ref, o_ref):
    x = x_ref[...]
    ms = jnp.mean(x * x, axis=-1, keepdims=True)
    y = x * lax.rsqrt(ms + NORM_EPS) * g_ref[...]
    h = (y * (1.0 + sc_ref[0]) + sh_ref[0]).astype(BF16)
    o_ref[...] = jnp.dot(h, w_ref[...], preferred_element_type=F32)


def _norm_mm(x2, g, scale, shift, w, tm, tiles_per_b, ctx_tiles, n_batch):
    m, d = x2.shape
    n = w.shape[1]
    row = lambda i: (_mod_row(i, tiles_per_b, ctx_tiles, n_batch), 0, 0)
    return pl.pallas_call(
        _norm_mm_kernel,
        grid=(m // tm,),
        in_specs=[pl.BlockSpec((tm, d), lambda i: (i, 0)),
                  pl.BlockSpec((1, d), lambda i: (0, 0)),
                  pl.BlockSpec((1, 1, d), row),
                  pl.BlockSpec((1, 1, d), row),
                  pl.BlockSpec((d, n), lambda i: (0, 0))],
        out_specs=pl.BlockSpec((tm, n), lambda i: (i, 0)),
        out_shape=jax.ShapeDtypeStruct((m, n), F32),
        compiler_params=_cparams("arbitrary"),
        name="norm_matmul",
    )(x2, g.reshape(1, d), scale, shift, w.astype(BF16))


def _mm_res_kernel(x_ref, w_ref, res_ref, gate_ref, o_ref):
    acc = jnp.dot(x_ref[...].astype(BF16), w_ref[...], preferred_element_type=F32)
    o_ref[...] = res_ref[...] + gate_ref[0] * acc


def _out_tiling(m, tm, tiles_per_b, ctx_tiles, n_batch, latent_only):
    if not latent_only:
        return m // tm, (lambda i: i), m
    lat = tiles_per_b - ctx_tiles
    return n_batch * lat, (lambda i: (i // lat) * tiles_per_b + ctx_tiles + i % lat), n_batch * lat * tm


def _mm_residual(x2, w, res2, gate, tm, tiles_per_b, ctx_tiles, n_batch, latent_only=False):
    m, k = x2.shape
    n = w.shape[1]
    steps, tile, rows = _out_tiling(m, tm, tiles_per_b, ctx_tiles, n_batch, latent_only)
    return pl.pallas_call(
        _mm_res_kernel,
        grid=(steps,),
        in_specs=[pl.BlockSpec((tm, k), lambda i: (tile(i), 0)),
                  pl.BlockSpec((k, n), lambda i: (0, 0)),
                  pl.BlockSpec((tm, n), lambda i: (tile(i), 0)),
                  pl.BlockSpec((1, 1, n), lambda i: (_mod_row(tile(i), tiles_per_b, ctx_tiles, n_batch), 0, 0))],
        out_specs=pl.BlockSpec((tm, n), lambda i: (i, 0)),
        out_shape=jax.ShapeDtypeStruct((rows, n), F32),
        compiler_params=_cparams("arbitrary"),
        name="matmul_residual",
    )(x2, w.astype(BF16), res2, gate)


def _dot_nt(a, b):
    return lax.dot_general(a, b, (((1,), (1,)), ((), ())), preferred_element_type=F32)


def _dot_tn(a, b):
    return lax.dot_general(a, b, (((0,), (0,)), ((), ())), preferred_element_type=F32)


def _dot(a, b):
    return jnp.dot(a, b, preferred_element_type=F32)


def _head_sum(x, ones_bd):
    lanes = ones_bd.shape[0]
    return jnp.concatenate(
        [jnp.dot(x[:, p * lanes:(p + 1) * lanes].astype(BF16), ones_bd, preferred_element_type=F32)
         for p in range(x.shape[1] // lanes)], axis=1)


def _same_head(lanes):
    return (lax.broadcasted_iota(jnp.int32, (lanes, lanes), 0) // HEAD_DIM
            == lax.broadcasted_iota(jnp.int32, (lanes, lanes), 1) // HEAD_DIM)


def _rwkv_proj_kernel(x_ref, xp_ref, xn_ref, g_ref, sc_ref, sh_ref, mu_ref, w_ref, w1_ref, w2_ref, a1_ref, a2_ref,
                      w0_ref, a0_ref, kk_ref, ka_ref, rk_ref,
                      rvk_out, g_out, bonus_out, dirs_out, *, tm, tiles_per_b, ctx_tiles):
    i = pl.program_id(0)
    j = i % tiles_per_b
    gain = g_ref[...]
    sc = 1.0 + sc_ref[0]
    sh = sh_ref[0]

    def norm(x):
        ms = jnp.mean(x * x, axis=-1, keepdims=True)
        return x * lax.rsqrt(ms + NORM_EPS) * gain * sc + sh

    h = norm(x_ref[...])
    has_prev = jnp.logical_and(j != 0, j != ctx_tiles)
    has_next = jnp.logical_and(j != ctx_tiles - 1, j != tiles_per_b - 1)
    h_prev = jnp.where(has_prev, norm(xp_ref[...])[7:8], 0.0)
    h_next = jnp.where(has_next, norm(xn_ref[...])[0:1], 0.0)
    row = lax.broadcasted_iota(jnp.int32, h.shape, 0)
    prev = jnp.where(row == 0, h_prev, pltpu.roll(h, 1, axis=0))
    nxt = jnp.where(row == tm - 1, h_next, pltpu.roll(h, tm - 1, axis=0))
    xx = 0.5 * (prev + nxt) - h
    mix = lambda n: (h + xx * mu_ref[n:n + 1, :]).astype(BF16)

    r = jnp.dot(mix(0), w_ref[0], preferred_element_type=F32)
    k = jnp.dot(mix(1), w_ref[1], preferred_element_type=F32)
    v = jnp.dot(mix(2), w_ref[2], preferred_element_type=F32)
    g_out[...] = jnp.dot(mix(3), w_ref[3], preferred_element_type=F32)
    dec = jnp.dot(jnp.tanh(jnp.dot(mix(4), w1_ref[...], preferred_element_type=F32)).astype(BF16), w2_ref[...],
                  preferred_element_type=F32)
    icl = jnp.dot(jnp.dot(mix(5), a1_ref[...], preferred_element_type=F32).astype(BF16), a2_ref[...],
                  preferred_element_type=F32)
    e = r.shape[1]
    ones_bd = _same_head(2 * HEAD_DIM).astype(BF16)
    kkf = k * kk_ref[...]
    kk = kkf * lax.rsqrt(jnp.maximum(_head_sum(kkf * kkf, ones_bd), 1e-24))
    rvk_out[:, 0:e] = r
    rvk_out[:, e:2 * e] = v
    rvk_out[:, 2 * e:3 * e] = kk
    bonus_out[...] = _head_sum(r * k * rk_ref[...], ones_bd) * v
    for s in range(2):
        dirs_out[s, :, 0:e] = -math.exp(-0.5) * _sigmoid(w0_ref[s:s + 1, :] + dec[:, s * e:(s + 1) * e])
        a = _sigmoid(a0_ref[s:s + 1, :] + icl[:, s * e:(s + 1) * e])
        dirs_out[s, :, e:2 * e] = k * (1.0 + (a - 1.0) * ka_ref[...])
        dirs_out[s, :, 2 * e:3 * e] = kk * a


def _rwkv_proj(stream2, norm_g, scale, shift, mu, w_rkvg, w0, w1, w2, a0, a1, a2, k_k, k_a, r_k, tm, tiles_per_b,
               ctx_tiles, n_batch):
    m, d = stream2.shape
    e = w_rkvg.shape[-1]
    lr = w1.shape[-1]
    nblk = m // 8
    zeros = jnp.zeros((lr, e), F32)
    cat = lambda w: jnp.concatenate([w[0], w[1]], axis=1).astype(BF16)
    bdiag = lambda w: jnp.concatenate([jnp.concatenate([w[0], zeros], axis=1),
                                       jnp.concatenate([zeros, w[1]], axis=1)], axis=0).astype(BF16)
    row = lambda i: (_mod_row(i, tiles_per_b, ctx_tiles, n_batch), 0, 0)
    full = lambda shape: pl.BlockSpec(shape, lambda i: (0,) * len(shape))
    tok = pl.BlockSpec((tm, e), lambda i: (i, 0))
    vec = lambda a: a.reshape(1, e).astype(F32)
    return pl.pallas_call(
        functools.partial(_rwkv_proj_kernel, tm=tm, tiles_per_b=tiles_per_b, ctx_tiles=ctx_tiles),
        grid=(m // tm,),
        in_specs=[pl.BlockSpec((tm, d), lambda i: (i, 0)),
                  pl.BlockSpec((8, d), lambda i: (jnp.maximum(i * (tm // 8) - 1, 0), 0)),
                  pl.BlockSpec((8, d), lambda i: (jnp.minimum((i + 1) * (tm // 8), nblk - 1), 0)),
                  full((1, d)), pl.BlockSpec((1, 1, d), row), pl.BlockSpec((1, 1, d), row),
                  full(mu.shape), full(w_rkvg.shape), full((d, 2 * lr)), full((2 * lr, 2 * e)),
                  full((d, 2 * lr)), full((2 * lr, 2 * e)), full((2, e)), full((2, e)),
                  full((1, e)), full((1, e)), full((1, e))],
        out_specs=[pl.BlockSpec((tm, 3 * e), lambda i: (i, 0)), tok, tok,
                   pl.BlockSpec((2, tm, 3 * e), lambda i: (0, i, 0))],
        out_shape=[jax.ShapeDtypeStruct((m, 3 * e), F32), jax.ShapeDtypeStruct((m, e), F32),
                   jax.ShapeDtypeStruct((m, e), F32), jax.ShapeDtypeStruct((2, m, 3 * e), F32)],
        compiler_params=_cparams("arbitrary"),
        name="rwkv_proj",
    )(stream2, stream2, stream2, norm_g.reshape(1, d), scale, shift, mu.astype(F32), w_rkvg.astype(BF16),
      cat(w1), bdiag(w2), cat(a1), bdiag(a2), w0.astype(F32), a0.astype(F32), vec(k_k), vec(k_a), vec(r_k))


def _cumsum_rows(x, reverse):
    c = x.shape[0]
    row = lax.broadcasted_iota(jnp.int32, x.shape, 0)
    shift = 1
    while shift < c:
        if reverse:
            x = x + jnp.where(row < c - shift, pltpu.roll(x, c - shift, axis=0), 0.0)
        else:
            x = x + jnp.where(row >= shift, pltpu.roll(x, shift, axis=0), 0.0)
        shift *= 2
    return x


def _rwkv_chunk_streams(streams):
    c = streams[0][0].shape[0]
    hd = HEAD_DIM
    lanes = 2 * hd
    row = lax.broadcasted_iota(jnp.int32, (c, 2 * c), 0)
    col = lax.broadcasted_iota(jnp.int32, (c, 2 * c), 1) % c
    eye2 = (row == col).astype(F32)
    lane_head = lax.broadcasted_iota(jnp.int32, (1, lanes), 1) // hd
    first = lane_head == 0
    same = _same_head(lanes)
    pick = lambda a, p: a[:, p * lanes:(p + 1) * lanes]
    zero = jnp.zeros((), BF16)

    units, incl2, strict2 = [], {}, {}
    lhs, k_h, b_h, k_p, b_p, vb, lp_tot, s_bd = {}, {}, {}, {}, {}, {}, {}, {}
    for si, (r, v, kk, lw, kd, bd, states, reverse) in enumerate(streams):
        incl2[si] = (col >= row) if reverse else (col <= row)
        strict2[si] = (col > row) if reverse else (col < row)
        lp = _cumsum_rows(lw, reverse)
        tot = jnp.sum(lw, axis=0, keepdims=True)
        lhs_f = jnp.concatenate([kk * jnp.exp(lp - lw), r * jnp.exp(lp)], axis=0).astype(BF16)
        e_ninc = jnp.exp(-lp)
        e_rem = jnp.exp(tot - lp)
        full = dict(lhs=lhs_f, k_h=(kd * e_ninc).astype(BF16), b_h=(bd * e_ninc).astype(BF16),
                    k_p=(kd * e_rem).astype(BF16), b_p=(bd * e_rem).astype(BF16), vb=v.astype(BF16), tot=tot)
        for p in range(len(states)):
            u_ = (si, p)
            units.append(u_)
            lhs[u_], k_h[u_], b_h[u_] = pick(full["lhs"], p), pick(full["k_h"], p), pick(full["b_h"], p)
            k_p[u_], b_p[u_], vb[u_] = pick(full["k_p"], p), pick(full["b_p"], p), pick(full["vb"], p)
            lp_tot[u_], s_bd[u_] = pick(full["tot"], p), states[p]
    bd = lambda x: jnp.concatenate([jnp.where(first, x, zero), jnp.where(first, zero, x)], axis=0)

    a_k = {u_: _dot_nt(lhs[u_], bd(k_h[u_])) for u_ in units}
    a_b = {u_: _dot_nt(lhs[u_], bd(b_h[u_])) for u_ in units}
    ls = {u_: _dot_nt(lhs[u_], s_bd[u_].astype(BF16)) for u_ in units}
    av = {u_: ls[u_] + _dot(jnp.concatenate([jnp.where(strict2[u_[0]], a_k[u_][:c], 0.0),
                                             jnp.where(incl2[u_[0]], a_k[u_][c:], 0.0)], axis=0).astype(BF16),
                            bd(vb[u_])) for u_ in units}
    z = {u_: av[u_][:c] for u_ in units}
    y0 = {u_: av[u_][c:] for u_ in units}

    m_neg = {u_: jnp.where(strict2[u_[0]], -a_b[u_][:c], 0.0) for u_ in units}
    t_inv = {u_: eye2 + m_neg[u_] for u_ in units}
    pw = {u_: m_neg[u_].astype(BF16) for u_ in units}
    pw = {u_: _dot(pw[u_], bd(pw[u_])).astype(BF16) for u_ in units}
    for _ in range(int(math.log2(c)) - 2):
        sq = {u_: _dot(jnp.concatenate([pw[u_], t_inv[u_].astype(BF16)], axis=0), bd(pw[u_])) for u_ in units}
        t_inv = {u_: t_inv[u_] + sq[u_][c:] for u_ in units}
        pw = {u_: sq[u_][:c].astype(BF16) for u_ in units}
    t_inv = {u_: t_inv[u_] + _dot(t_inv[u_].astype(BF16), bd(pw[u_])) for u_ in units}

    u = {u_: _dot(t_inv[u_].astype(BF16), bd(z[u_].astype(BF16))) for u_ in units}
    y = {u_: y0[u_] - _dot(jnp.where(incl2[u_[0]], a_b[u_][c:], 0.0).astype(BF16), bd(u[u_].astype(BF16)))
         for u_ in units}
    s1 = {}
    for u_ in units:
        upd = _dot_tn(jnp.concatenate([vb[u_], -u[u_].astype(BF16)], axis=0),
                      jnp.concatenate([k_p[u_], b_p[u_]], axis=0))
        s1[u_] = s_bd[u_] * jnp.exp(lp_tot[u_]) + jnp.where(same, upd, 0.0)
    return [(jnp.concatenate([y[si, p] for p in range(len(st[6]))], axis=1),
             [s1[si, p] for p in range(len(st[6]))]) for si, st in enumerate(streams)]


def _rwkv_scan_kernel(rvkf_ref, dirf_ref, rvkb_ref, dirb_ref, yf_ref, yb_ref, s_ref):
    @pl.when(pl.program_id(1) == 0)
    def _():
        s_ref[...] = jnp.zeros_like(s_ref)

    npairs = s_ref.shape[0] // 2
    e = yf_ref.shape[2]
    third = lambda a: (a[:, 0:e], a[:, e:2 * e], a[:, 2 * e:3 * e])
    fwd = third(rvkf_ref[0]) + third(dirf_ref[0, 0]) + ([s_ref[p] for p in range(npairs)], False)
    bwd = third(rvkb_ref[0]) + third(dirb_ref[0, 0]) + ([s_ref[npairs + p] for p in range(npairs)], True)
    (y_f, s_f), (y_b, s_b) = _rwkv_chunk_streams([fwd, bwd])
    yf_ref[0] = y_f
    yb_ref[0] = y_b
    for p in range(npairs):
        s_ref[p] = s_f[p]
        s_ref[npairs + p] = s_b[p]


def _rwkv_scan(rvk, dirs, n_ctx):
    bsz, t, e3 = rvk.shape
    e = e3 // 3
    c = RWKV_CHUNK
    nc, nc_ctx = t // c, n_ctx // c
    rev = lambda ci: jnp.where(ci < nc_ctx, nc_ctx - 1 - ci, nc + nc_ctx - 1 - ci)
    return pl.pallas_call(
        _rwkv_scan_kernel,
        grid=(bsz, nc),
        in_specs=[pl.BlockSpec((1, c, e3), lambda b, ci: (b, ci, 0)),
                  pl.BlockSpec((1, 1, c, e3), lambda b, ci: (0, b, ci, 0)),
                  pl.BlockSpec((1, c, e3), lambda b, ci: (b, rev(ci), 0)),
                  pl.BlockSpec((1, 1, c, e3), lambda b, ci: (1, b, rev(ci), 0))],
        out_specs=[pl.BlockSpec((1, c, e), lambda b, ci: (b, ci, 0)),
                   pl.BlockSpec((1, c, e), lambda b, ci: (b, rev(ci), 0))],
        out_shape=[jax.ShapeDtypeStruct((bsz, t, e), F32)] * 2,
        scratch_shapes=[pltpu.VMEM((2 * e // (2 * HEAD_DIM), 2 * HEAD_DIM, 2 * HEAD_DIM), F32)],
        compiler_params=_cparams("arbitrary", "arbitrary"),
        name="rwkv_scan",
    )(rvk, dirs, rvk, dirs)


def _rwkv_out_kernel(yf_ref, yb_ref, bonus_ref, g_ref, lnw_ref, lnb_ref, w_ref, res_ref, gate_ref, o_ref):
    ones_bd = _same_head(2 * HEAD_DIM).astype(BF16)
    y = yf_ref[...] + yb_ref[...]
    mean = _head_sum(y, ones_bd) * (1.0 / HEAD_DIM)
    yc = y - mean
    var = _head_sum(yc * yc, ones_bd) * (1.0 / HEAD_DIM)
    yn = yc * lax.rsqrt(var + RWKV_GN_EPS) * lnw_ref[...] + lnb_ref[...]
    g = g_ref[...]
    o = ((yn + bonus_ref[...]) * (g * _sigmoid(g))).astype(BF16)
    o_ref[...] = res_ref[...] + gate_ref[0] * jnp.dot(o, w_ref[...], preferred_element_type=F32)


def _rwkv_out(y_f, y_b, bonus, g, ln_w, ln_b, w_out, res2, gate, tm, tiles_per_b, ctx_tiles, n_batch,
              latent_only=False):
    m, e = bonus.shape
    d = w_out.shape[1]
    steps, tile, rows = _out_tiling(m, tm, tiles_per_b, ctx_tiles, n_batch, latent_only)
    tok = pl.BlockSpec((tm, e), lambda i: (tile(i), 0))
    vec = pl.BlockSpec((1, e), lambda i: (0, 0))
    return pl.pallas_call(
        _rwkv_out_kernel,
        grid=(steps,),
        in_specs=[tok, tok, tok, tok, vec, vec,
                  pl.BlockSpec((e, d), lambda i: (0, 0)),
                  pl.BlockSpec((tm, d), lambda i: (tile(i), 0)),
                  pl.BlockSpec((1, 1, d), lambda i: (_mod_row(tile(i), tiles_per_b, ctx_tiles, n_batch), 0, 0))],
        out_specs=pl.BlockSpec((tm, d), lambda i: (i, 0)),
        out_shape=jax.ShapeDtypeStruct((rows, d), F32),
        compiler_params=_cparams("arbitrary"),
        name="rwkv_out",
    )(y_f, y_b, bonus, g, ln_w.reshape(1, e).astype(F32), ln_b.reshape(1, e).astype(F32), w_out.astype(BF16),
      res2, gate)


def _rwkv_layer(stream, norm_g, scale, shift, gate, n_ctx, mu, w_rkvg, w0, w1, w2, a0, a1, a2, k_k, k_a, r_k,
                ln_w, ln_b, w_out, last):
    bsz, t_all, d = stream.shape
    e = w_rkvg.shape[-1]
    m = bsz * t_all
    tm = _token_tile(n_ctx, t_all, 256)
    tile_args = (t_all // tm, n_ctx // tm, bsz)
    s2 = stream.reshape(m, d)
    rvk, g, bonus, dirs = _rwkv_proj(s2, norm_g, scale, shift, mu, w_rkvg, w0, w1, w2, a0, a1, a2,
                                     k_k, k_a, r_k, tm, *tile_args)
    y_f, y_b = _rwkv_scan(rvk.reshape(bsz, t_all, 3 * e), dirs.reshape(2, bsz, t_all, 3 * e), n_ctx)
    return _rwkv_out(y_f.reshape(m, e), y_b.reshape(m, e), bonus, g, ln_w, ln_b, w_out, s2, gate, tm,
                     *tile_args, latent_only=last).reshape(bsz, -1, d)


def _na_kernel(q_ref, k_ref, v_ref, z_ref, bias_ref, qg_ref, kg_ref, o_ref, kn_ref, vb_ref, *,
               n_ctx, grid_w, kh, khm, rows, scale, rq):
    step = pl.program_id(2)
    ctx_steps = n_ctx // (grid_w * rq)
    hd = HEAD_DIM
    lanes = 2 * hd
    same_head = _same_head(lanes).astype(BF16)
    lane_head = lax.broadcasted_iota(jnp.int32, (1, lanes), 1) // hd
    first = lane_head == 0

    def head_rms(x, g):
        ss = jnp.dot((x * x).astype(BF16), same_head, preferred_element_type=F32)
        return x * lax.rsqrt(ss * (1.0 / hd) + NORM_EPS) * g

    @pl.when(step == 0)
    def _():
        kn_ref[...] = head_rms(k_ref[0], kg_ref[...]).astype(BF16)
        vb_ref[...] = v_ref[0].astype(BF16)

    qn = head_rms(q_ref[0], qg_ref[...]) * scale
    z = z_ref[0]
    zgate = z * _sigmoid(z)
    k_ctx = kn_ref[0:n_ctx, :]
    v_ctx = vb_ref[0:n_ctx, :]
    subs = range(rq)
    q2 = {s: jnp.concatenate([jnp.where(lane_head == hh, qn[s * grid_w:(s + 1) * grid_w], 0.0).astype(BF16)
                              for hh in range(2)], axis=0) for s in subs}

    def attend(bands):
        s_c = {s: _dot_nt(q2[s], k_ctx) for s in subs}
        m = {s: jnp.max(s_c[s], axis=-1, keepdims=True) for s in subs}
        if bands is not None:
            s_n = {s: _dot_nt(q2[s], bands[s][0])
                      + jnp.concatenate([bands[s][2](hh) for hh in range(2)], axis=0) for s in subs}
            m = {s: jnp.maximum(m[s], jnp.max(s_n[s], axis=-1, keepdims=True)) for s in subs}
            p_n = {s: jnp.exp(s_n[s] - m[s]) for s in subs}
        p_c = {s: jnp.exp(s_c[s] - m[s]) for s in subs}
        den = {s: jnp.sum(p_c[s], axis=-1, keepdims=True) for s in subs}
        acc = {s: _dot(p_c[s].astype(BF16), v_ctx) for s in subs}
        if bands is not None:
            den = {s: den[s] + jnp.sum(p_n[s], axis=-1, keepdims=True) for s in subs}
            acc = {s: acc[s] + _dot(p_n[s].astype(BF16), bands[s][1]) for s in subs}
        o2 = {s: acc[s] / den[s] for s in subs}
        out = [jnp.where(first, o2[s][:grid_w], o2[s][grid_w:]) for s in subs]
        return jnp.concatenate(out, axis=0) * zgate

    @pl.when(step < ctx_steps)
    def _():
        o_ref[0] = attend(None)

    @pl.when(step >= ctx_steps)
    def _():
        bands = []
        for s in range(rq):
            i = (step - ctx_steps) * rq + s
            r0 = jnp.clip(i - kh // 2, 0, rows - kh)
            start = pl.multiple_of(n_ctx + r0 * grid_w, grid_w)
            base = khm - 1 - (i - r0)
            bias_of = functools.partial(
                lambda hh, base: jnp.concatenate([bias_ref[hh, base + 2 * q] for q in range(kh // 2)], axis=1),
                base=base)
            bands.append((kn_ref[pl.ds(start, kh * grid_w), :], vb_ref[pl.ds(start, kh * grid_w), :], bias_of))
        o_ref[0] = attend(bands)


def _na_bias_table(rpb, grid_w):
    kw = (rpb.shape[2] + 1) // 2
    j = np.arange(grid_w)[:, None]
    c = np.arange(grid_w)[None, :]
    c0 = np.clip(j - kw // 2, 0, grid_w - kw)
    valid = (c >= c0) & (c < c0 + kw)
    onehot = ((c - j + kw - 1)[None] == np.arange(2 * kw - 1)[:, None, None]) & valid[None]
    tiles = jnp.einsum("hab,bjc->hajc", rpb.astype(F32), jnp.asarray(onehot, F32), precision=HIGHEST)
    tiles = tiles + jnp.asarray(np.where(valid, 0.0, MASK_NEG), F32)
    return jnp.concatenate([tiles[:, :-1], tiles[:, 1:]], axis=-1)


def _na_attention(qkvz, rpb, q_g, k_g, n_ctx, grid_w, kh_max):
    bsz, t_all, e4 = qkvz.shape
    e = e4 // 4
    pairs = e // (2 * HEAD_DIM)
    rows = (t_all - n_ctx) // grid_w
    kh = min(kh_max, rows)
    assert kh % 2 == 0 and kh <= kh_max
    tbl = _na_bias_table(rpb, grid_w)
    ctx_tiles = n_ctx // grid_w
    rq = NA_ROWS_PER_STEP if (ctx_tiles % NA_ROWS_PER_STEP == 0 and rows % NA_ROWS_PER_STEP == 0) else 1
    lanes = 2 * HEAD_DIM
    g2 = lambda g: jnp.concatenate([g, g]).reshape(1, lanes).astype(F32)
    tok = lambda col0: pl.BlockSpec((1, rq * grid_w, lanes), lambda b, p, i: (b, i, col0 + p))
    seq = lambda col0: pl.BlockSpec((1, t_all, lanes), lambda b, p, i: (b, 0, col0 + p))
    return pl.pallas_call(
        functools.partial(_na_kernel, n_ctx=n_ctx, grid_w=grid_w, kh=kh, khm=kh_max, rows=rows,
                          scale=HEAD_DIM ** -0.5, rq=rq),
        grid=(bsz, pairs, t_all // (rq * grid_w)),
        in_specs=[tok(0), seq(pairs), seq(2 * pairs), tok(3 * pairs),
                  pl.BlockSpec((2,) + tbl.shape[1:], lambda b, p, i: (p, 0, 0, 0)),
                  pl.BlockSpec((1, lanes), lambda b, p, i: (0, 0)),
                  pl.BlockSpec((1, lanes), lambda b, p, i: (0, 0))],
        out_specs=tok(0),
        out_shape=jax.ShapeDtypeStruct((bsz, t_all, e), F32),
        scratch_shapes=[pltpu.VMEM((t_all, lanes), BF16), pltpu.VMEM((t_all, lanes), BF16)],
        compiler_params=_cparams("arbitrary", "arbitrary", "arbitrary"),
        name="na_attention",
    )(qkvz, qkvz, qkvz, qkvz, tbl, g2(q_g), g2(k_g))


def _na_layer(stream, norm_g, scale, shift, gate, n_ctx, tm, tile_args, grid_w, w_in, q_g, k_g, rpb, w_out, last):
    bsz, t_all, d = stream.shape
    m = bsz * t_all
    s2 = stream.reshape(m, d)
    qkvz = _norm_mm(s2, norm_g, scale, shift, w_in, tm, *tile_args).reshape(bsz, t_all, w_in.shape[1])
    kh_max = (rpb.shape[1] + 1) // 2
    o = _na_attention(qkvz, rpb, q_g, k_g, n_ctx, grid_w, kh_max)
    return _mm_residual(o.reshape(m, -1), w_out, s2, gate, tm, *tile_args, latent_only=last).reshape(bsz, -1, d)


def _s5_matrices(lam_re, lam_im, log_dt, b_re, b_im, c_re, c_im):
    nt = S5_CHUNK
    g, p, cg = b_re.shape
    tau = jnp.arange(nt + 1, dtype=F32)[:, None, None]
    i_idx = np.arange(nt)
    kers, sels, b_re_cols, b_im_cols, c_re_rows, c_im_rows, a_re, a_im = [], [], [], [], [], [], [], []
    for s in range(2):
        lr, li = lam_re[s].astype(F32), lam_im[s].astype(F32)
        step = jnp.exp(log_dt[s].astype(F32))[:, None]
        mag = jnp.exp(lr * step)
        ar, ai = mag * jnp.cos(li * step), mag * jnp.sin(li * step)
        den = lr * lr + li * li
        qr = ((ar - 1.0) * lr + ai * li) / den
        qi = (ai * lr - (ar - 1.0) * li) / den
        bbr = qr[..., None] * b_re - qi[..., None] * b_im
        bbi = qr[..., None] * b_im + qi[..., None] * b_re
        pmag = jnp.exp(lr * step * tau)
        pr, pi = pmag * jnp.cos(li * step * tau), pmag * jnp.sin(li * step * tau)
        clr = c_re[None] * pr[:, :, None, :] - c_im[None] * pi[:, :, None, :]
        cli = c_re[None] * pi[:, :, None, :] + c_im[None] * pr[:, :, None, :]
        ker = jnp.sum(clr[:nt, :, :, None, :] * bbr.transpose(0, 2, 1)[None, :, None]
                      - cli[:nt, :, :, None, :] * bbi.transpose(0, 2, 1)[None, :, None], axis=-1)
        kers.append(ker)
        lag = (i_idx[None, :] - i_idx[:, None]) if s == 0 else (i_idx[:, None] - i_idx[None, :])
        sels.append((lag[None] == i_idx[:, None, None]).astype(np.float32))
        inj = (nt - 1 - i_idx) if s == 0 else i_idx
        prj, pij = (a[inj].transpose(1, 0, 2)[:, :, None, :] for a in (pr, pi))
        bbr_t, bbi_t = (a.transpose(0, 2, 1)[:, None] for a in (bbr, bbi))
        b_re_cols.append((prj * bbr_t - pij * bbi_t).reshape(g, nt * cg, p))
        b_im_cols.append((prj * bbi_t + pij * bbr_t).reshape(g, nt * cg, p))
        out = (i_idx + 1) if s == 0 else (nt - i_idx)
        pro, pio = (a[out].transpose(1, 2, 0)[:, :, :, None] for a in (pr, pi))
        cre_t, cim_t = (a.transpose(0, 2, 1)[:, :, None, :] for a in (c_re, c_im))
        c_re_rows.append((cre_t * pro - cim_t * pio).reshape(g, p, nt * cg))
        c_im_rows.append((-(cre_t * pio + cim_t * pro)).reshape(g, p, nt * cg))
        a_re.append(pr[nt])
        a_im.append(pi[nt])
    kbig = jnp.einsum("tji,tgoc->gjcio", jnp.asarray(np.concatenate(sels, axis=0)), jnp.concatenate(kers, axis=0),
                      precision=HIGHEST).reshape(g, nt * cg, nt * cg)
    w_inj = jnp.concatenate(b_re_cols + b_im_cols, axis=2)
    w_out = jnp.concatenate(c_re_rows + c_im_rows, axis=1)
    coef = lambda a: jnp.concatenate(a, axis=-1)
    return kbig, w_inj, w_out, coef(a_re), coef(a_im)


def _s5_core_kernel(u_ref, wk_ref, winj_ref, wout_ref, are_ref, aim_ref, y_ref,
                    x_ref, yi_ref, bre_ref, bim_ref, xa_re, xa_im, xb_re, xb_im, *, nck, nck_ctx):
    nt, cg = S5_CHUNK, S5_GROUP
    gpb = u_ref.shape[1] // cg
    feat = nt * cg
    half = feat // 2
    pst = bre_ref.shape[2]
    lane_blk = lax.broadcasted_iota(jnp.int32, (1, u_ref.shape[1]), 1) // cg

    def regroup(pieces, shift_of, key_of):
        acc = None
        for n, src in enumerate(pieces):
            shift = shift_of(n) % (gpb * cg)
            rolled = pltpu.roll(src, shift, axis=1) if shift else src
            acc = rolled if acc is None else jnp.where(lane_blk == key_of(n), rolled, acc)
        return acc

    for hf in range(nt // gpb):
        toks = [u_ref[pl.ds(gpb * hf + jj, nck, stride=nt), :].astype(BF16) for jj in range(gpb)]
        for g8 in range(gpb):
            x_ref[:, g8 * feat + hf * half:g8 * feat + (hf + 1) * half] = regroup(
                toks, lambda jj: cg * (jj - g8), lambda jj: jj)

    for g8 in range(gpb):
        xg = x_ref[:, g8 * feat:(g8 + 1) * feat]
        yi_ref[:, g8 * feat:(g8 + 1) * feat] = jnp.dot(xg, wk_ref[g8], preferred_element_type=F32)
        inj = jnp.dot(xg, winj_ref[g8], preferred_element_type=F32)
        bre_ref[:, g8, :] = inj[:, :pst]
        bim_ref[:, g8, :] = inj[:, pst:]

    dir0 = lax.broadcasted_iota(jnp.int32, (1, pst), 1) < pst // 2
    a_re = are_ref[...]
    a_im = aim_ref[...]

    def body(k, carry):
        xr, xi = carry
        rk = jnp.where(k < nck_ctx, nck_ctx - 1 - k, nck + nck_ctx - 1 - k)
        xa_re[k] = xr
        xa_im[k] = xi
        xb_re[rk] = xr
        xb_im[rk] = xi
        b_r = jnp.where(dir0, bre_ref[k], bre_ref[rk])
        b_i = jnp.where(dir0, bim_ref[k], bim_ref[rk])
        return a_re * xr - a_im * xi + b_r, a_re * xi + a_im * xr + b_i

    zero = jnp.zeros(a_re.shape, F32)
    lax.fori_loop(0, nck, body, (zero, zero))

    for g8 in range(gpb):
        state = jnp.concatenate([jnp.where(dir0, xa_re[:, g8, :], xb_re[:, g8, :]),
                                 jnp.where(dir0, xa_im[:, g8, :], xb_im[:, g8, :])], axis=1).astype(BF16)
        yi_ref[:, g8 * feat:(g8 + 1) * feat] += jnp.dot(state, wout_ref[g8], preferred_element_type=F32)

    for hf in range(nt // gpb):
        grp = [yi_ref[:, g8 * feat + hf * half:g8 * feat + (hf + 1) * half] for g8 in range(gpb)]
        for ii in range(gpb):
            y_ref[pl.ds(gpb * hf + ii, nck, stride=nt), :] = regroup(grp, lambda g8: cg * (g8 - ii), lambda g8: g8)


def _s5_core(uz, w_k, w_inj, w_out, a_re, a_im, bsz, n_ctx, e):
    m = uz.shape[0]
    t_all = m // bsz
    nt, cg = S5_CHUNK, S5_GROUP
    lanes = 128
    gpb = lanes // cg
    nck, nck_ctx = t_all // nt, n_ctx // nt
    feat = nt * cg
    pst = w_out.shape[1] // 2
    assert w_k.shape[1:] == (feat, feat) and w_inj.shape[1:] == (feat, 2 * pst) and w_out.shape[1:] == (2 * pst, feat)
    return pl.pallas_call(
        functools.partial(_s5_core_kernel, nck=nck, nck_ctx=nck_ctx),
        grid=(bsz, e // lanes),
        in_specs=[pl.BlockSpec((t_all, lanes), lambda b, q: (b, q)),
                  pl.BlockSpec((gpb,) + w_k.shape[1:], lambda b, q: (q, 0, 0)),
                  pl.BlockSpec((gpb,) + w_inj.shape[1:], lambda b, q: (q, 0, 0)),
                  pl.BlockSpec((gpb,) + w_out.shape[1:], lambda b, q: (q, 0, 0)),
                  pl.BlockSpec((gpb, pst), lambda b, q: (q, 0)),
                  pl.BlockSpec((gpb, pst), lambda b, q: (q, 0))],
        out_specs=pl.BlockSpec((t_all, lanes), lambda b, q: (b, q)),
        out_shape=jax.ShapeDtypeStruct((m, e), F32),
        scratch_shapes=[pltpu.VMEM((nck, gpb * feat), BF16), pltpu.VMEM((nck, gpb * feat), F32)]
                       + [pltpu.VMEM((nck, gpb, pst), F32)] * 6,
        compiler_params=_cparams("arbitrary", "arbitrary"),
        name="s5_core",
    )(uz, w_k.astype(BF16), w_inj.astype(BF16), w_out.astype(BF16), a_re, a_im)


def _s5_out_kernel(y_ref, u_ref, z_ref, d_ref, wg_ref, bg_ref, w_ref, res_ref, gate_ref, o_ref):
    y = jax.nn.gelu(y_ref[...] + d_ref[...] * u_ref[...])
    y = y * _sigmoid(jnp.dot(y.astype(BF16), wg_ref[...], preferred_element_type=F32) + bg_ref[...])
    z = z_ref[...]
    o = (y * (z * _sigmoid(z))).astype(BF16)
    o_ref[...] = res_ref[...] + gate_ref[0] * jnp.dot(o, w_ref[...], preferred_element_type=F32)


def _s5_out(y, uz, d_skip, w_glu, b_glu, w_out, res2, gate, tm, tiles_per_b, ctx_tiles, n_batch, latent_only=False):
    m, e = y.shape
    d = w_out.shape[1]
    steps, tile, rows = _out_tiling(m, tm, tiles_per_b, ctx_tiles, n_batch, latent_only)
    vec = pl.BlockSpec((1, e), lambda i: (0, 0))
    return pl.pallas_call(
        _s5_out_kernel,
        grid=(steps,),
        in_specs=[pl.BlockSpec((tm, e), lambda i: (tile(i), 0)),
                  pl.BlockSpec((tm, e), lambda i: (tile(i), 0)),
                  pl.BlockSpec((tm, e), lambda i: (tile(i), 1)),
                  vec, pl.BlockSpec((e, e), lambda i: (0, 0)), vec,
                  pl.BlockSpec((e, d), lambda i: (0, 0)),
                  pl.BlockSpec((tm, d), lambda i: (tile(i), 0)),
                  pl.BlockSpec((1, 1, d), lambda i: (_mod_row(tile(i), tiles_per_b, ctx_tiles, n_batch), 0, 0))],
        out_specs=pl.BlockSpec((tm, d), lambda i: (i, 0)),
        out_shape=jax.ShapeDtypeStruct((rows, d), F32),
        compiler_params=_cparams("arbitrary"),
        name="s5_out",
    )(y, uz, uz, d_skip.reshape(1, e).astype(F32), w_glu.astype(BF16), b_glu.reshape(1, e).astype(F32),
      w_out.astype(BF16), res2, gate)


def _s5_layer(stream, norm_g, scale, shift, gate, n_ctx, tm, tile_args, w_in, lam_re, lam_im, log_dt, b_re, b_im,
              c_re, c_im, d_skip, w_glu, b_glu, w_out, last):
    bsz, t_all, d = stream.shape
    m = bsz * t_all
    e = w_in.shape[1] // 2
    uz = _norm_mm(stream.reshape(m, d), norm_g, scale, shift, w_in, tm, *tile_args)
    k_big, k_inj, k_out, a_re, a_im = _s5_matrices(lam_re, lam_im, log_dt, b_re, b_im, c_re, c_im)
    y = _s5_core(uz, k_big, k_inj, k_out, a_re, a_im, bsz, n_ctx, e)
    return _s5_out(y, uz, d_skip, w_glu, b_glu, w_out, stream.reshape(m, d), gate, tm,
                   *tile_args, latent_only=last).reshape(bsz, -1, d)


def kernel(x, c, ctx, c_ctx, norm_g, w_mod, b_mod, rwkv_mu, rwkv_w_rkvg, rwkv_w0, rwkv_w1, rwkv_w2, rwkv_a0, rwkv_a1, rwkv_a2, rwkv_k_k, rwkv_k_a, rwkv_r_k, rwkv_ln_w, rwkv_ln_b, rwkv_w_out, na_w_in, na_q_g, na_k_g, na_rpb, na_w_out, s5_w_in, s5_lam_re, s5_lam_im, s5_log_dt, s5_b_re, s5_b_im, s5_c_re, s5_c_im, s5_d, s5_w_glu, s5_b_glu, s5_w_out):
    bsz, n_lat, d = x.shape
    n_ctx = ctx.shape[1]
    t_all = n_ctx + n_lat
    depth = norm_g.shape[0]
    grid_w = 64
    tm = _token_tile(n_ctx, t_all)
    tile_args = (t_all // tm, n_ctx // tm, bsz)
    stream = jnp.concatenate([ctx, x], axis=1).astype(F32)
    rows = 8 * ((bsz + 1 + 7) // 8)
    cc = jnp.zeros((rows, d), F32).at[:bsz].set(c.astype(F32)).at[bsz].set(c_ctx.astype(F32))
    mod_all = _modulation(cc, w_mod, b_mod)
    for i in range(depth):
        kind, j = i % 3, i // 3
        mod = mod_all[i, :bsz + 1]
        shift, scale, gate = (mod[:, k * d:(k + 1) * d].reshape(bsz + 1, 1, d) for k in range(3))
        last = i == depth - 1
        if kind == 0:
            stream = _rwkv_layer(stream, norm_g[i], scale, shift, gate, n_ctx, rwkv_mu[j], rwkv_w_rkvg[j], rwkv_w0[j],
                                 rwkv_w1[j], rwkv_w2[j], rwkv_a0[j], rwkv_a1[j], rwkv_a2[j], rwkv_k_k[j],
                                 rwkv_k_a[j], rwkv_r_k[j], rwkv_ln_w[j], rwkv_ln_b[j], rwkv_w_out[j], last)
        elif kind == 1:
            stream = _na_layer(stream, norm_g[i], scale, shift, gate, n_ctx, tm, tile_args, grid_w, na_w_in[j],
                               na_q_g[j], na_k_g[j], na_rpb[j], na_w_out[j], last)
        else:
            stream = _s5_layer(stream, norm_g[i], scale, shift, gate, n_ctx, tm, tile_args, s5_w_in[j],
                               s5_lam_re[j], s5_lam_im[j], s5_log_dt[j], s5_b_re[j], s5_b_im[j], s5_c_re[j],
                               s5_c_im[j], s5_d[j], s5_w_glu[j], s5_b_glu[j], s5_w_out[j], last)
    return stream.astype(x.dtype)
```

```python
import functools
import math

import numpy as np
import jax
import jax.numpy as jnp
from jax import lax
from jax.experimental import pallas as pl
from jax.experimental.pallas import tpu as pltpu

F32 = jnp.float32
BF16 = jnp.bfloat16
NORM_EPS = 1e-6
RWKV_GN_EPS = 64e-5
HEAD_DIM = 64
RWKV_CHUNK = 64
NA_ROWS_PER_STEP = 4
S5_CHUNK = 16
S5_GROUP = 16
MASK_NEG = -1e30
VMEM_LIMIT = 48 * 1024 * 1024
HIGHEST = lax.Precision.HIGHEST


def _cparams(*sem):
    return pltpu.CompilerParams(dimension_semantics=sem, vmem_limit_bytes=VMEM_LIMIT)


def _token_tile(n_ctx, n_all, largest=256):
    for t in (256, 128, 64):
        if t <= largest and n_ctx % t == 0 and n_all % t == 0:
            return t
    raise ValueError("context / sequence lengths must be multiples of 64")


def _sigmoid(x):
    return 0.5 * jnp.tanh(0.5 * x) + 0.5


def _mod_row(i, tiles_per_b, ctx_tiles, n_batch):
    return jnp.where(i % tiles_per_b < ctx_tiles, n_batch, i // tiles_per_b)


def _mod_kernel(c_ref, w_ref, b_ref, o_ref):
    c = c_ref[...]
    s = c * jax.nn.sigmoid(c)
    o_ref[0] = jnp.dot(s, w_ref[0], precision=HIGHEST, preferred_element_type=F32) + b_ref[0]


def _modulation(cc, w_mod, b_mod):
    rows, d = cc.shape
    depth, _, n = w_mod.shape
    tn = 512
    return pl.pallas_call(
        _mod_kernel,
        grid=(depth, n // tn),
        in_specs=[pl.BlockSpec((rows, d), lambda l, j: (0, 0)),
                  pl.BlockSpec((1, d, tn), lambda l, j: (l, 0, j)),
                  pl.BlockSpec((1, 1, tn), lambda l, j: (l, 0, j))],
        out_specs=pl.BlockSpec((1, rows, tn), lambda l, j: (l, 0, j)),
        out_shape=jax.ShapeDtypeStruct((depth, rows, n), F32),
        compiler_params=_cparams("arbitrary", "arbitrary"),
        name="modulation",
    )(cc, w_mod.astype(F32), b_mod.reshape(depth, 1, n).astype(F32))


def _norm_mm_kernel(x_ref, g_ref, sc_ref, sh_ref, w_ref, o_ref):
    x = x_ref[...]
    ms = jnp.mean(x * x, axis=-1, keepdims=True)
    y = x * lax.rsqrt(ms + NORM_EPS) * g_ref[...]
    h = (y * (1.0 + sc_ref[0]) + sh_ref[0]).astype(BF16)
    o_ref[...] = jnp.dot(h, w_ref[...], preferred_element_type=F32)


def _norm_mm(x2, g, scale, shift, w, tm, tiles_per_b, ctx_tiles, n_batch):
    m, d = x2.shape
    n = w.shape[1]
    row = lambda i: (_mod_row(i, tiles_per_b, ctx_tiles, n_batch), 0, 0)
    return pl.pallas_call(
        _norm_mm_kernel,
        grid=(m // tm,),
        in_specs=[pl.BlockSpec((tm, d), lambda i: (i, 0)),
                  pl.BlockSpec((1, d), lambda i: (0, 0)),
                  pl.BlockSpec((1, 1, d), row),
                  pl.BlockSpec((1, 1, d), row),
                  pl.BlockSpec((d, n), lambda i: (0, 0))],
        out_specs=pl.BlockSpec((tm, n), lambda i: (i, 0)),
        out_shape=jax.ShapeDtypeStruct((m, n), F32),
        compiler_params=_cparams("arbitrary"),
        name="norm_matmul",
    )(x2, g.reshape(1, d), scale, shift, w.astype(BF16))


def _mm_res_kernel(x_ref, w_ref, res_ref, gate_ref, o_ref):
    acc = jnp.dot(x_ref[...].astype(BF16), w_ref[...], preferred_element_type=F32)
    o_ref[...] = res_ref[...] + gate_ref[0] * acc


def _out_tiling(m, tm, tiles_per_b, ctx_tiles, n_batch, latent_only):
    if not latent_only:
        return m // tm, (lambda i: i), m
    lat = tiles_per_b - ctx_tiles
    return n_batch * lat, (lambda i: (i // lat) * tiles_per_b + ctx_tiles + i % lat), n_batch * lat * tm


def _mm_residual(x2, w, res2, gate, tm, tiles_per_b, ctx_tiles, n_batch, latent_only=False):
    m, k = x2.shape
    n = w.shape[1]
    steps, tile, rows = _out_tiling(m, tm, tiles_per_b, ctx_tiles, n_batch, latent_only)
    return pl.pallas_call(
        _mm_res_kernel,
        grid=(steps,),
        in_specs=[pl.BlockSpec((tm, k), lambda i: (tile(i), 0)),
                  pl.BlockSpec((k, n), lambda i: (0, 0)),
                  pl.BlockSpec((tm, n), lambda i: (tile(i), 0)),
                  pl.BlockSpec((1, 1, n), lambda i: (_mod_row(tile(i), tiles_per_b, ctx_tiles, n_batch), 0, 0))],
        out_specs=pl.BlockSpec((tm, n), lambda i: (i, 0)),
        out_shape=jax.ShapeDtypeStruct((rows, n), F32),
        compiler_params=_cparams("arbitrary"),
        name="matmul_residual",
    )(x2, w.astype(BF16), res2, gate)


def _dot_nt(a, b):
    return lax.dot_general(a, b, (((1,), (1,)), ((), ())), preferred_element_type=F32)


def _dot_tn(a, b):
    return lax.dot_general(a, b, (((0,), (0,)), ((), ())), preferred_element_type=F32)


def _dot(a, b):
    return jnp.dot(a, b, preferred_element_type=F32)


def _head_sum(x, ones_bd):
    lanes = ones_bd.shape[0]
    return jnp.concatenate(
        [jnp.dot(x[:, p * lanes:(p + 1) * lanes].astype(BF16), ones_bd, preferred_element_type=F32)
         for p in range(x.shape[1] // lanes)], axis=1)


def _same_head(lanes):
    return (lax.broadcasted_iota(jnp.int32, (lanes, lanes), 0) // HEAD_DIM
            == lax.broadcasted_iota(jnp.int32, (lanes, lanes), 1) // HEAD_DIM)


def _rwkv_proj_kernel(x_ref, xp_ref, xn_ref, g_ref, sc_ref, sh_ref, mu_ref, w_ref, w1_ref, w2_ref, a1_ref, a2_ref,
                      w0_ref, a0_ref, kk_ref, ka_ref, rk_ref,
                      rvk_out, g_out, bonus_out, dirs_out, *, tm, tiles_per_b, ctx_tiles):
    i = pl.program_id(0)
    j = i % tiles_per_b
    gain = g_ref[...]
    sc = 1.0 + sc_ref[0]
    sh = sh_ref[0]

    def norm(x):
        ms = jnp.mean(x * x, axis=-1, keepdims=True)
        return x * lax.rsqrt(ms + NORM_EPS) * gain * sc + sh

    h = norm(x_ref[...])
    has_prev = jnp.logical_and(j != 0, j != ctx_tiles)
    has_next = jnp.logical_and(j != ctx_tiles - 1, j != tiles_per_b - 1)
    h_prev = jnp.where(has_prev, norm(xp_ref[...])[7:8], 0.0)
    h_next = jnp.where(has_next, norm(xn_ref[...])[0:1], 0.0)
    row = lax.broadcasted_iota(jnp.int32, h.shape, 0)
    prev = jnp.where(row == 0, h_prev, pltpu.roll(h, 1, axis=0))
    nxt = jnp.where(row == tm - 1, h_next, pltpu.roll(h, tm - 1, axis=0))
    xx = 0.5 * (prev + nxt) - h
    mix = lambda n: (h + xx * mu_ref[n:n + 1, :]).astype(BF16)

    r = jnp.dot(mix(0), w_ref[0], preferred_element_type=F32)
    k = jnp.dot(mix(1), w_ref[1], preferred_element_type=F32)
    v = jnp.dot(mix(2), w_ref[2], preferred_element_type=F32)
    g_out[...] = jnp.dot(mix(3), w_ref[3], preferred_element_type=F32)
    dec = jnp.dot(jnp.tanh(jnp.dot(mix(4), w1_ref[...], preferred_element_type=F32)).astype(BF16), w2_ref[...],
                  preferred_element_type=F32)
    icl = jnp.dot(jnp.dot(mix(5), a1_ref[...], preferred_element_type=F32).astype(BF16), a2_ref[...],
                  preferred_element_type=F32)
    e = r.shape[1]
    ones_bd = _same_head(2 * HEAD_DIM).astype(BF16)
    kkf = k * kk_ref[...]
    kk = kkf * lax.rsqrt(jnp.maximum(_head_sum(kkf * kkf, ones_bd), 1e-24))
    rvk_out[:, 0:e] = r
    rvk_out[:, e:2 * e] = v
    rvk_out[:, 2 * e:3 * e] = kk
    bonus_out[...] = _head_sum(r * k * rk_ref[...], ones_bd) * v
    for s in range(2):
        dirs_out[s, :, 0:e] = -math.exp(-0.5) * _sigmoid(w0_ref[s:s + 1, :] + dec[:, s * e:(s + 1) * e])
        a = _sigmoid(a0_ref[s:s + 1, :] + icl[:, s * e:(s + 1) * e])
        dirs_out[s, :, e:2 * e] = k * (1.0 + (a - 1.0) * ka_ref[...])
        dirs_out[s, :, 2 * e:3 * e] = kk * a


def _rwkv_proj(stream2, norm_g, scale, shift, mu, w_rkvg, w0, w1, w2, a0, a1, a2, k_k, k_a, r_k, tm, tiles_per_b,
               ctx_tiles, n_batch):
    m, d = stream2.shape
    e = w_rkvg.shape[-1]
    lr = w1.shape[-1]
    nblk = m // 8
    zeros = jnp.zeros((lr, e), F32)
    cat = lambda w: jnp.concatenate([w[0], w[1]], axis=1).astype(BF16)
    bdiag = lambda w: jnp.concatenate([jnp.concatenate([w[0], zeros], axis=1),
                                       jnp.concatenate([zeros, w[1]], axis=1)], axis=0).astype(BF16)
    row = lambda i: (_mod_row(i, tiles_per_b, ctx_tiles, n_batch), 0, 0)
    full = lambda shape: pl.BlockSpec(shape, lambda i: (0,) * len(shape))
    tok = pl.BlockSpec((tm, e), lambda i: (i, 0))
    vec = lambda a: a.reshape(1, e).astype(F32)
    return pl.pallas_call(
        functools.partial(_rwkv_proj_kernel, tm=tm, tiles_per_b=tiles_per_b, ctx_tiles=ctx_tiles),
        grid=(m // tm,),
        in_specs=[pl.BlockSpec((tm, d), lambda i: (i, 0)),
                  pl.BlockSpec((8, d), lambda i: (jnp.maximum(i * (tm // 8) - 1, 0), 0)),
                  pl.BlockSpec((8, d), lambda i: (jnp.minimum((i + 1) * (tm // 8), nblk - 1), 0)),
                  full((1, d)), pl.BlockSpec((1, 1, d), row), pl.BlockSpec((1, 1, d), row),
                  full(mu.shape), full(w_rkvg.shape), full((d, 2 * lr)), full((2 * lr, 2 * e)),
                  full((d, 2 * lr)), full((2 * lr, 2 * e)), full((2, e)), full((2, e)),
                  full((1, e)), full((1, e)), full((1, e))],
        out_specs=[pl.BlockSpec((tm, 3 * e), lambda i: (i, 0)), tok, tok,
                   pl.BlockSpec((2, tm, 3 * e), lambda i: (0, i, 0))],
        out_shape=[jax.ShapeDtypeStruct((m, 3 * e), F32), jax.ShapeDtypeStruct((m, e), F32),
                   jax.ShapeDtypeStruct((m, e), F32), jax.ShapeDtypeStruct((2, m, 3 * e), F32)],
        compiler_params=_cparams("arbitrary"),
        name="rwkv_proj",
    )(stream2, stream2, stream2, norm_g.reshape(1, d), scale, shift, mu.astype(F32), w_rkvg.astype(BF16),
      cat(w1), bdiag(w2), cat(a1), bdiag(a2), w0.astype(F32), a0.astype(F32), vec(k_k), vec(k_a), vec(r_k))


def _cumsum_rows(x, reverse):
    c = x.shape[0]
    row = lax.broadcasted_iota(jnp.int32, x.shape, 0)
    shift = 1
    while shift < c:
        if reverse:
            x = x + jnp.where(row < c - shift, pltpu.roll(x, c - shift, axis=0), 0.0)
        else:
            x = x + jnp.where(row >= shift, pltpu.roll(x, shift, axis=0), 0.0)
        shift *= 2
    return x


def _rwkv_chunk_streams(streams):
    c = streams[0][0].shape[0]
    hd = HEAD_DIM
    lanes = 2 * hd
    row = lax.broadcasted_iota(jnp.int32, (c, 2 * c), 0)
    col = lax.broadcasted_iota(jnp.int32, (c, 2 * c), 1) % c
    lane_head = lax.broadcasted_iota(jnp.int32, (1, lanes), 1) // hd
    first = lane_head == 0
    same = _same_head(lanes)
    pick = lambda a, p: a[:, p * lanes:(p + 1) * lanes]
    zero = jnp.zeros((), BF16)

    units, incl2, strict2 = [], {}, {}
    lhs, k_h, b_h, k_p, b_p, vb, lp_tot, s_bd = {}, {}, {}, {}, {}, {}, {}, {}
    for si, (r, v, kk, lw, kd, bd, states, reverse) in enumerate(streams):
        incl2[si] = (col >= row) if reverse else (col <= row)
        strict2[si] = (col > row) if reverse else (col < row)
        lp = _cumsum_rows(lw, reverse)
        tot = jnp.sum(lw, axis=0, keepdims=True)
        lhs_f = jnp.concatenate([kk * jnp.exp(lp - lw), r * jnp.exp(lp)], axis=0).astype(BF16)
        e_ninc = jnp.exp(-lp)
        e_rem = jnp.exp(tot - lp)
        full = dict(lhs=lhs_f, k_h=(kd * e_ninc).astype(BF16), b_h=(bd * e_ninc).astype(BF16),
                    k_p=(kd * e_rem).astype(BF16), b_p=(bd * e_rem).astype(BF16), vb=v.astype(BF16), tot=tot)
        for p in range(len(states)):
            u_ = (si, p)
            units.append(u_)
            lhs[u_], k_h[u_], b_h[u_] = pick(full["lhs"], p), pick(full["k_h"], p), pick(full["b_h"], p)
            k_p[u_], b_p[u_], vb[u_] = pick(full["k_p"], p), pick(full["b_p"], p), pick(full["vb"], p)
            lp_tot[u_], s_bd[u_] = pick(full["tot"], p), states[p]
    bd = lambda x: jnp.concatenate([jnp.where(first, x, zero), jnp.where(first, zero, x)], axis=0)

    a_k = {u_: _dot_nt(lhs[u_], bd(k_h[u_])) for u_ in units}
    a_b = {u_: _dot_nt(lhs[u_], bd(b_h[u_])) for u_ in units}
    ls = {u_: _dot_nt(lhs[u_], s_bd[u_].astype(BF16)) for u_ in units}
    av = {u_: ls[u_] + _dot(jnp.concatenate([jnp.where(strict2[u_[0]], a_k[u_][:c], 0.0),
                                             jnp.where(incl2[u_[0]], a_k[u_][c:], 0.0)], axis=0).astype(BF16),
                            bd(vb[u_])) for u_ in units}
    z = {u_: av[u_][:c] for u_ in units}
    y0 = {u_: av[u_][c:] for u_ in units}

    d_inv = {u_: jnp.where(strict2[u_[0]], -a_b[u_][:c], 0.0) for u_ in units}
    pw = {u_: d_inv[u_].astype(BF16) for u_ in units}
    pw_f = {u_: _dot(pw[u_], bd(pw[u_])) for u_ in units}
    for _ in range(int(math.log2(c)) - 2):
        pw = {u_: pw_f[u_].astype(BF16) for u_ in units}
        sq = {u_: _dot(jnp.concatenate([pw[u_], d_inv[u_].astype(BF16)], axis=0), bd(pw[u_])) for u_ in units}
        d_inv = {u_: d_inv[u_] + pw_f[u_] + sq[u_][c:] for u_ in units}
        pw_f = {u_: sq[u_][:c] for u_ in units}
    pw = {u_: pw_f[u_].astype(BF16) for u_ in units}
    d_inv = {u_: d_inv[u_] + pw_f[u_] + _dot(d_inv[u_].astype(BF16), bd(pw[u_])) for u_ in units}

    u = {u_: z[u_] + _dot(d_inv[u_].astype(BF16), bd(z[u_].astype(BF16))) for u_ in units}
    y = {u_: y0[u_] - _dot(jnp.where(incl2[u_[0]], a_b[u_][c:], 0.0).astype(BF16), bd(u[u_].astype(BF16)))
         for u_ in units}
    s1 = {}
    for u_ in units:
        upd = _dot_tn(jnp.concatenate([vb[u_], -u[u_].astype(BF16)], axis=0),
                      jnp.concatenate([k_p[u_], b_p[u_]], axis=0))
        s1[u_] = s_bd[u_] * jnp.exp(lp_tot[u_]) + jnp.where(same, upd, 0.0)
    return [(jnp.concatenate([y[si, p] for p in range(len(st[6]))], axis=1),
             [s1[si, p] for p in range(len(st[6]))]) for si, st in enumerate(streams)]


def _rwkv_scan_kernel(rvkf_ref, dirf_ref, rvkb_ref, dirb_ref, yf_ref, yb_ref, s_ref):
    @pl.when(pl.program_id(1) == 0)
    def _():
        s_ref[...] = jnp.zeros_like(s_ref)

    npairs = s_ref.shape[0] // 2
    e = yf_ref.shape[2]
    third = lambda a: (a[:, 0:e], a[:, e:2 * e], a[:, 2 * e:3 * e])
    fwd = third(rvkf_ref[0]) + third(dirf_ref[0, 0]) + ([s_ref[p] for p in range(npairs)], False)
    bwd = third(rvkb_ref[0]) + third(dirb_ref[0, 0]) + ([s_ref[npairs + p] for p in range(npairs)], True)
    (y_f, s_f), (y_b, s_b) = _rwkv_chunk_streams([fwd, bwd])
    yf_ref[0] = y_f
    yb_ref[0] = y_b
    for p in range(npairs):
        s_ref[p] = s_f[p]
        s_ref[npairs + p] = s_b[p]


def _rwkv_scan(rvk, dirs, n_ctx):
    bsz, t, e3 = rvk.shape
    e = e3 // 3
    c = RWKV_CHUNK
    nc, nc_ctx = t // c, n_ctx // c
    rev = lambda ci: jnp.where(ci < nc_ctx, nc_ctx - 1 - ci, nc + nc_ctx - 1 - ci)
    return pl.pallas_call(
        _rwkv_scan_kernel,
        grid=(bsz, nc),
        in_specs=[pl.BlockSpec((1, c, e3), lambda b, ci: (b, ci, 0)),
                  pl.BlockSpec((1, 1, c, e3), lambda b, ci: (0, b, ci, 0)),
                  pl.BlockSpec((1, c, e3), lambda b, ci: (b, rev(ci), 0)),
                  pl.BlockSpec((1, 1, c, e3), lambda b, ci: (1, b, rev(ci), 0))],
        out_specs=[pl.BlockSpec((1, c, e), lambda b, ci: (b, ci, 0)),
                   pl.BlockSpec((1, c, e), lambda b, ci: (b, rev(ci), 0))],
        out_shape=[jax.ShapeDtypeStruct((bsz, t, e), F32)] * 2,
        scratch_shapes=[pltpu.VMEM((2 * e // (2 * HEAD_DIM), 2 * HEAD_DIM, 2 * HEAD_DIM), F32)],
        compiler_params=_cparams("arbitrary", "arbitrary"),
        name="rwkv_scan",
    )(rvk, dirs, rvk, dirs)


def _rwkv_out_kernel(yf_ref, yb_ref, bonus_ref, g_ref, lnw_ref, lnb_ref, w_ref, res_ref, gate_ref, o_ref):
    ones_bd = _same_head(2 * HEAD_DIM).astype(BF16)
    y = yf_ref[...] + yb_ref[...]
    mean = _head_sum(y, ones_bd) * (1.0 / HEAD_DIM)
    yc = y - mean
    var = _head_sum(yc * yc, ones_bd) * (1.0 / HEAD_DIM)
    yn = yc * lax.rsqrt(var + RWKV_GN_EPS) * lnw_ref[...] + lnb_ref[...]
    g = g_ref[...]
    o = ((yn + bonus_ref[...]) * (g * _sigmoid(g))).astype(BF16)
    o_ref[...] = res_ref[...] + gate_ref[0] * jnp.dot(o, w_ref[...], preferred_element_type=F32)


def _rwkv_out(y_f, y_b, bonus, g, ln_w, ln_b, w_out, res2, gate, tm, tiles_per_b, ctx_tiles, n_batch,
              latent_only=False):
    m, e = bonus.shape
    d = w_out.shape[1]
    steps, tile, rows = _out_tiling(m, tm, tiles_per_b, ctx_tiles, n_batch, latent_only)
    tok = pl.BlockSpec((tm, e), lambda i: (tile(i), 0))
    vec = pl.BlockSpec((1, e), lambda i: (0, 0))
    return pl.pallas_call(
        _rwkv_out_kernel,
        grid=(steps,),
        in_specs=[tok, tok, tok, tok, vec, vec,
                  pl.BlockSpec((e, d), lambda i: (0, 0)),
                  pl.BlockSpec((tm, d), lambda i: (tile(i), 0)),
                  pl.BlockSpec((1, 1, d), lambda i: (_mod_row(tile(i), tiles_per_b, ctx_tiles, n_batch), 0, 0))],
        out_specs=pl.BlockSpec((tm, d), lambda i: (i, 0)),
        out_shape=jax.ShapeDtypeStruct((rows, d), F32),
        compiler_params=_cparams("arbitrary"),
        name="rwkv_out",
    )(y_f, y_b, bonus, g, ln_w.reshape(1, e).astype(F32), ln_b.reshape(1, e).astype(F32), w_out.astype(BF16),
      res2, gate)


def _rwkv_layer(stream, norm_g, scale, shift, gate, n_ctx, mu, w_rkvg, w0, w1, w2, a0, a1, a2, k_k, k_a, r_k,
                ln_w, ln_b, w_out, last):
    bsz, t_all, d = stream.shape
    e = w_rkvg.shape[-1]
    m = bsz * t_all
    tm = _token_tile(n_ctx, t_all, 256)
    tile_args = (t_all // tm, n_ctx // tm, bsz)
    s2 = stream.reshape(m, d)
    rvk, g, bonus, dirs = _rwkv_proj(s2, norm_g, scale, shift, mu, w_rkvg, w0, w1, w2, a0, a1, a2,
                                     k_k, k_a, r_k, tm, *tile_args)
    y_f, y_b = _rwkv_scan(rvk.reshape(bsz, t_all, 3 * e), dirs.reshape(2, bsz, t_all, 3 * e), n_ctx)
    return _rwkv_out(y_f.reshape(m, e), y_b.reshape(m, e), bonus, g, ln_w, ln_b, w_out, s2, gate, tm,
                     *tile_args, latent_only=last).reshape(bsz, -1, d)


def _na_kernel(q_ref, k_ref, v_ref, z_ref, bias_ref, qg_ref, kg_ref, o_ref, kn_ref, vb_ref, *,
               n_ctx, grid_w, kh, khm, rows, scale, rq):
    step = pl.program_id(2)
    ctx_steps = n_ctx // (grid_w * rq)
    hd = HEAD_DIM
    lanes = 2 * hd
    same_head = _same_head(lanes).astype(BF16)
    lane_head = lax.broadcasted_iota(jnp.int32, (1, lanes), 1) // hd
    first = lane_head == 0

    def head_rms(x, g):
        ss = jnp.dot((x * x).astype(BF16), same_head, preferred_element_type=F32)
        return x * lax.rsqrt(ss * (1.0 / hd) + NORM_EPS) * g

    @pl.when(step == 0)
    def _():
        kn_ref[...] = head_rms(k_ref[0], kg_ref[...]).astype(BF16)
        vb_ref[...] = v_ref[0].astype(BF16)

    qn = head_rms(q_ref[0], qg_ref[...]) * scale
    z = z_ref[0]
    zgate = z * _sigmoid(z)
    k_ctx = kn_ref[0:n_ctx, :]
    v_ctx = vb_ref[0:n_ctx, :]
    subs = range(rq)
    q2 = {s: jnp.concatenate([jnp.where(lane_head == hh, qn[s * grid_w:(s + 1) * grid_w], 0.0).astype(BF16)
                              for hh in range(2)], axis=0) for s in subs}

    def attend(bands):
        s_c = {s: _dot_nt(q2[s], k_ctx) for s in subs}
        m = {s: jnp.max(s_c[s], axis=-1, keepdims=True) for s in subs}
        if bands is not None:
            s_n = {s: _dot_nt(q2[s], bands[s][0])
                      + jnp.concatenate([bands[s][2](hh) for hh in range(2)], axis=0) for s in subs}
            m = {s: jnp.maximum(m[s], jnp.max(s_n[s], axis=-1, keepdims=True)) for s in subs}
            p_n = {s: jnp.exp(s_n[s] - m[s]) for s in subs}
        p_c = {s: jnp.exp(s_c[s] - m[s]) for s in subs}
        den = {s: jnp.sum(p_c[s], axis=-1, keepdims=True) for s in subs}
        acc = {s: _dot(p_c[s].astype(BF16), v_ctx) for s in subs}
        if bands is not None:
            den = {s: den[s] + jnp.sum(p_n[s], axis=-1, keepdims=True) for s in subs}
            acc = {s: acc[s] + _dot(p_n[s].astype(BF16), bands[s][1]) for s in subs}
        o2 = {s: acc[s] / den[s] for s in subs}
        out = [jnp.where(first, o2[s][:grid_w], o2[s][grid_w:]) for s in subs]
        return jnp.concatenate(out, axis=0) * zgate

    @pl.when(step < ctx_steps)
    def _():
        o_ref[0] = attend(None)

    @pl.when(step >= ctx_steps)
    def _():
        bands = []
        for s in range(rq):
            i = (step - ctx_steps) * rq + s
            r0 = jnp.clip(i - kh // 2, 0, rows - kh)
            start = pl.multiple_of(n_ctx + r0 * grid_w, grid_w)
            base = khm - 1 - (i - r0)
            bias_of = functools.partial(
                lambda hh, base: jnp.concatenate([bias_ref[hh, base + 2 * q] for q in range(kh // 2)], axis=1),
                base=base)
            bands.append((kn_ref[pl.ds(start, kh * grid_w), :], vb_ref[pl.ds(start, kh * grid_w), :], bias_of))
        o_ref[0] = attend(bands)


def _na_bias_table(rpb, grid_w):
    kw = (rpb.shape[2] + 1) // 2
    j = np.arange(grid_w)[:, None]
    c = np.arange(grid_w)[None, :]
    c0 = np.clip(j - kw // 2, 0, grid_w - kw)
    valid = (c >= c0) & (c < c0 + kw)
    onehot = ((c - j + kw - 1)[None] == np.arange(2 * kw - 1)[:, None, None]) & valid[None]
    tiles = jnp.einsum("hab,bjc->hajc", rpb.astype(F32), jnp.asarray(onehot, F32), precision=HIGHEST)
    tiles = tiles + jnp.asarray(np.where(valid, 0.0, MASK_NEG), F32)
    return jnp.concatenate([tiles[:, :-1], tiles[:, 1:]], axis=-1)


def _na_attention(qkvz, rpb, q_g, k_g, n_ctx, grid_w, kh_max):
    bsz, t_all, e4 = qkvz.shape
    e = e4 // 4
    pairs = e // (2 * HEAD_DIM)
    rows = (t_all - n_ctx) // grid_w
    kh = min(kh_max, rows)
    assert kh % 2 == 0 and kh <= kh_max
    tbl = _na_bias_table(rpb, grid_w)
    ctx_tiles = n_ctx // grid_w
    rq = NA_ROWS_PER_STEP if (ctx_tiles % NA_ROWS_PER_STEP == 0 and rows % NA_ROWS_PER_STEP == 0) else 1
    lanes = 2 * HEAD_DIM
    g2 = lambda g: jnp.concatenate([g, g]).reshape(1, lanes).astype(F32)
    tok = lambda col0: pl.BlockSpec((1, rq * grid_w, lanes), lambda b, p, i: (b, i, col0 + p))
    seq = lambda col0: pl.BlockSpec((1, t_all, lanes), lambda b, p, i: (b, 0, col0 + p))
    return pl.pallas_call(
        functools.partial(_na_kernel, n_ctx=n_ctx, grid_w=grid_w, kh=kh, khm=kh_max, rows=rows,
                          scale=HEAD_DIM ** -0.5, rq=rq),
        grid=(bsz, pairs, t_all // (rq * grid_w)),
        in_specs=[tok(0), seq(pairs), seq(2 * pairs), tok(3 * pairs),
                  pl.BlockSpec((2,) + tbl.shape[1:], lambda b, p, i: (p, 0, 0, 0)),
                  pl.BlockSpec((1, lanes), lambda b, p, i: (0, 0)),
                  pl.BlockSpec((1, lanes), lambda b, p, i: (0, 0))],
        out_specs=tok(0),
        out_shape=jax.ShapeDtypeStruct((bsz, t_all, e), F32),
        scratch_shapes=[pltpu.VMEM((t_all, lanes), BF16), pltpu.VMEM((t_all, lanes), BF16)],
        compiler_params=_cparams("arbitrary", "arbitrary", "arbitrary"),
        name="na_attention",
    )(qkvz, qkvz, qkvz, qkvz, tbl, g2(q_g), g2(k_g))


def _na_layer(stream, norm_g, scale, shift, gate, n_ctx, tm, tile_args, grid_w, w_in, q_g, k_g, rpb, w_out, last):
    bsz, t_all, d = stream.shape
    m = bsz * t_all
    s2 = stream.reshape(m, d)
    qkvz = _norm_mm(s2, norm_g, scale, shift, w_in, tm, *tile_args).reshape(bsz, t_all, w_in.shape[1])
    kh_max = (rpb.shape[1] + 1) // 2
    o = _na_attention(qkvz, rpb, q_g, k_g, n_ctx, grid_w, kh_max)
    return _mm_residual(o.reshape(m, -1), w_out, s2, gate, tm, *tile_args, latent_only=last).reshape(bsz, -1, d)


def _s5_matrices(lam_re, lam_im, log_dt, b_re, b_im, c_re, c_im):
    nt = S5_CHUNK
    g, p, cg = b_re.shape
    tau = jnp.arange(nt + 1, dtype=F32)[:, None, None]
    i_idx = np.arange(nt)
    kers, sels, b_re_cols, b_im_cols, c_re_rows, c_im_rows, a_re, a_im = [], [], [], [], [], [], [], []
    for s in range(2):
        lr, li = lam_re[s].astype(F32), lam_im[s].astype(F32)
        step = jnp.exp(log_dt[s].astype(F32))[:, None]
        mag = jnp.exp(lr * step)
        ar, ai = mag * jnp.cos(li * step), mag * jnp.sin(li * step)
        den = lr * lr + li * li
        qr = ((ar - 1.0) * lr + ai * li) / den
        qi = (ai * lr - (ar - 1.0) * li) / den
        bbr = qr[..., None] * b_re - qi[..., None] * b_im
        bbi = qr[..., None] * b_im + qi[..., None] * b_re
        pmag = jnp.exp(lr * step * tau)
        pr, pi = pmag * jnp.cos(li * step * tau), pmag * jnp.sin(li * step * tau)
        clr = c_re[None] * pr[:, :, None, :] - c_im[None] * pi[:, :, None, :]
        cli = c_re[None] * pi[:, :, None, :] + c_im[None] * pr[:, :, None, :]
        ker = jnp.sum(clr[:nt, :, :, None, :] * bbr.transpose(0, 2, 1)[None, :, None]
                      - cli[:nt, :, :, None, :] * bbi.transpose(0, 2, 1)[None, :, None], axis=-1)
        kers.append(ker)
        lag = (i_idx[None, :] - i_idx[:, None]) if s == 0 else (i_idx[:, None] - i_idx[None, :])
        sels.append((lag[None] == i_idx[:, None, None]).astype(np.float32))
        inj = (nt - 1 - i_idx) if s == 0 else i_idx
        prj, pij = (a[inj].transpose(1, 0, 2)[:, :, None, :] for a in (pr, pi))
        bbr_t, bbi_t = (a.transpose(0, 2, 1)[:, None] for a in (bbr, bbi))
        b_re_cols.append((prj * bbr_t - pij * bbi_t).reshape(g, nt * cg, p))
        b_im_cols.append((prj * bbi_t + pij * bbr_t).reshape(g, nt * cg, p))
        out = (i_idx + 1) if s == 0 else (nt - i_idx)
        pro, pio = (a[out].transpose(1, 2, 0)[:, :, :, None] for a in (pr, pi))
        cre_t, cim_t = (a.transpose(0, 2, 1)[:, :, None, :] for a in (c_re, c_im))
        c_re_rows.append((cre_t * pro - cim_t * pio).reshape(g, p, nt * cg))
        c_im_rows.append((-(cre_t * pio + cim_t * pro)).reshape(g, p, nt * cg))
        a_re.append(pr[nt])
        a_im.append(pi[nt])
    kbig = jnp.einsum("tji,tgoc->gjcio", jnp.asarray(np.concatenate(sels, axis=0)), jnp.concatenate(kers, axis=0),
                      precision=HIGHEST).reshape(g, nt * cg, nt * cg)
    w_inj = jnp.concatenate(b_re_cols + b_im_cols, axis=2)
    w_out = jnp.concatenate(c_re_rows + c_im_rows, axis=1)
    coef = lambda a: jnp.concatenate(a, axis=-1)
    return kbig, w_inj, w_out, coef(a_re), coef(a_im)


def _s5_core_kernel(u_ref, wk_ref, winj_ref, wout_ref, are_ref, aim_ref, y_ref,
                    x_ref, yi_ref, bre_ref, bim_ref, xa_re, xa_im, xb_re, xb_im, *, nck, nck_ctx):
    nt, cg = S5_CHUNK, S5_GROUP
    gpb = u_ref.shape[1] // cg
    feat = nt * cg
    half = feat // 2
    pst = bre_ref.shape[2]
    lane_blk = lax.broadcasted_iota(jnp.int32, (1, u_ref.shape[1]), 1) // cg

    def regroup(pieces, shift_of, key_of):
        acc = None
        for n, src in enumerate(pieces):
            shift = shift_of(n) % (gpb * cg)
            rolled = pltpu.roll(src, shift, axis=1) if shift else src
            acc = rolled if acc is None else jnp.where(lane_blk == key_of(n), rolled, acc)
        return acc

    for hf in range(nt // gpb):
        toks = [u_ref[pl.ds(gpb * hf + jj, nck, stride=nt), :].astype(BF16) for jj in range(gpb)]
        for g8 in range(gpb):
            x_ref[:, g8 * feat + hf * half:g8 * feat + (hf + 1) * half] = regroup(
                toks, lambda jj: cg * (jj - g8), lambda jj: jj)

    for g8 in range(gpb):
        xg = x_ref[:, g8 * feat:(g8 + 1) * feat]
        yi_ref[:, g8 * feat:(g8 + 1) * feat] = jnp.dot(xg, wk_ref[g8], preferred_element_type=F32)
        inj = jnp.dot(xg, winj_ref[g8], preferred_element_type=F32)
        bre_ref[:, g8, :] = inj[:, :pst]
        bim_ref[:, g8, :] = inj[:, pst:]

    dir0 = lax.broadcasted_iota(jnp.int32, (1, pst), 1) < pst // 2
    a_re = are_ref[...]
    a_im = aim_ref[...]

    def body(k, carry):
        xr, xi = carry
        rk = jnp.where(k < nck_ctx, nck_ctx - 1 - k, nck + nck_ctx - 1 - k)
        xa_re[k] = xr
        xa_im[k] = xi
        xb_re[rk] = xr
        xb_im[rk] = xi
        b_r = jnp.where(dir0, bre_ref[k], bre_ref[rk])
        b_i = jnp.where(dir0, bim_ref[k], bim_ref[rk])
        return a_re * xr - a_im * xi + b_r, a_re * xi + a_im * xr + b_i

    zero = jnp.zeros(a_re.shape, F32)
    lax.fori_loop(0, nck, body, (zero, zero))

    for g8 in range(gpb):
        state = jnp.concatenate([jnp.where(dir0, xa_re[:, g8, :], xb_re[:, g8, :]),
                                 jnp.where(dir0, xa_im[:, g8, :], xb_im[:, g8, :])], axis=1).astype(BF16)
        yi_ref[:, g8 * feat:(g8 + 1) * feat] += jnp.dot(state, wout_ref[g8], preferred_element_type=F32)

    for hf in range(nt // gpb):
        grp = [yi_ref[:, g8 * feat + hf * half:g8 * feat + (hf + 1) * half] for g8 in range(gpb)]
        for ii in range(gpb):
            y_ref[pl.ds(gpb * hf + ii, nck, stride=nt), :] = regroup(grp, lambda g8: cg * (g8 - ii), lambda g8: g8)


def _s5_core(uz, w_k, w_inj, w_out, a_re, a_im, bsz, n_ctx, e):
    m = uz.shape[0]
    t_all = m // bsz
    nt, cg = S5_CHUNK, S5_GROUP
    lanes = 128
    gpb = lanes // cg
    nck, nck_ctx = t_all // nt, n_ctx // nt
    feat = nt * cg
    pst = w_out.shape[1] // 2
    assert w_k.shape[1:] == (feat, feat) and w_inj.shape[1:] == (feat, 2 * pst) and w_out.shape[1:] == (2 * pst, feat)
    return pl.pallas_call(
        functools.partial(_s5_core_kernel, nck=nck, nck_ctx=nck_ctx),
        grid=(bsz, e // lanes),
        in_specs=[pl.BlockSpec((t_all, lanes), lambda b, q: (b, q)),
                  pl.BlockSpec((gpb,) + w_k.shape[1:], lambda b, q: (q, 0, 0)),
                  pl.BlockSpec((gpb,) + w_inj.shape[1:], lambda b, q: (q, 0, 0)),
                  pl.BlockSpec((gpb,) + w_out.shape[1:], lambda b, q: (q, 0, 0)),
                  pl.BlockSpec((gpb, pst), lambda b, q: (q, 0)),
                  pl.BlockSpec((gpb, pst), lambda b, q: (q, 0))],
        out_specs=pl.BlockSpec((t_all, lanes), lambda b, q: (b, q)),
        out_shape=jax.ShapeDtypeStruct((m, e), F32),
        scratch_shapes=[pltpu.VMEM((nck, gpb * feat), BF16), pltpu.VMEM((nck, gpb * feat), F32)]
                       + [pltpu.VMEM((nck, gpb, pst), F32)] * 6,
        compiler_params=_cparams("arbitrary", "arbitrary"),
        name="s5_core",
    )(uz, w_k.astype(BF16), w_inj.astype(BF16), w_out.astype(BF16), a_re, a_im)


def _s5_out_kernel(y_ref, u_ref, z_ref, d_ref, wg_ref, bg_ref, w_ref, res_ref, gate_ref, o_ref):
    y = jax.nn.gelu(y_ref[...] + d_ref[...] * u_ref[...])
    y = y * _sigmoid(jnp.dot(y.astype(BF16), wg_ref[...], preferred_element_type=F32) + bg_ref[...])
    z = z_ref[...]
    o = (y * (z * _sigmoid(z))).astype(BF16)
    o_ref[...] = res_ref[...] + gate_ref[0] * jnp.dot(o, w_ref[...], preferred_element_type=F32)


def _s5_out(y, uz, d_skip, w_glu, b_glu, w_out, res2, gate, tm, tiles_per_b, ctx_tiles, n_batch, latent_only=False):
    m, e = y.shape
    d = w_out.shape[1]
    steps, tile, rows = _out_tiling(m, tm, tiles_per_b, ctx_tiles, n_batch, latent_only)
    vec = pl.BlockSpec((1, e), lambda i: (0, 0))
    return pl.pallas_call(
        _s5_out_kernel,
        grid=(steps,),
        in_specs=[pl.BlockSpec((tm, e), lambda i: (tile(i), 0)),
                  pl.BlockSpec((tm, e), lambda i: (tile(i), 0)),
                  pl.BlockSpec((tm, e), lambda i: (tile(i), 1)),
                  vec, pl.BlockSpec((e, e), lambda i: (0, 0)), vec,
                  pl.BlockSpec((e, d), lambda i: (0, 0)),
                  pl.BlockSpec((tm, d), lambda i: (tile(i), 0)),
                  pl.BlockSpec((1, 1, d), lambda i: (_mod_row(tile(i), tiles_per_b, ctx_tiles, n_batch), 0, 0))],
        out_specs=pl.BlockSpec((tm, d), lambda i: (i, 0)),
        out_shape=jax.ShapeDtypeStruct((rows, d), F32),
        compiler_params=_cparams("arbitrary"),
        name="s5_out",
    )(y, uz, uz, d_skip.reshape(1, e).astype(F32), w_glu.astype(BF16), b_glu.reshape(1, e).astype(F32),
      w_out.astype(BF16), res2, gate)


def _s5_layer(stream, norm_g, scale, shift, gate, n_ctx, tm, tile_args, w_in, lam_re, lam_im, log_dt, b_re, b_im,
              c_re, c_im, d_skip, w_glu, b_glu, w_out, last):
    bsz, t_all, d = stream.shape
    m = bsz * t_all
    e = w_in.shape[1] // 2
    uz = _norm_mm(stream.reshape(m, d), norm_g, scale, shift, w_in, tm, *tile_args)
    k_big, k_inj, k_out, a_re, a_im = _s5_matrices(lam_re, lam_im, log_dt, b_re, b_im, c_re, c_im)
    y = _s5_core(uz, k_big, k_inj, k_out, a_re, a_im, bsz, n_ctx, e)
    return _s5_out(y, uz, d_skip, w_glu, b_glu, w_out, stream.reshape(m, d), gate, tm,
                   *tile_args, latent_only=last).reshape(bsz, -1, d)


def kernel(x, c, ctx, c_ctx, norm_g, w_mod, b_mod, rwkv_mu, rwkv_w_rkvg, rwkv_w0, rwkv_w1, rwkv_w2, rwkv_a0, rwkv_a1, rwkv_a2, rwkv_k_k, rwkv_k_a, rwkv_r_k, rwkv_ln_w, rwkv_ln_b, rwkv_w_out, na_w_in, na_q_g, na_k_g, na_rpb, na_w_out, s5_w_in, s5_lam_re, s5_lam_im, s5_log_dt, s5_b_re, s5_b_im, s5_c_re, s5_c_im, s5_d, s5_w_glu, s5_b_glu, s5_w_out):
    bsz, n_lat, d = x.shape
    n_ctx = ctx.shape[1]
    t_all = n_ctx + n_lat
    depth = norm_g.shape[0]
    grid_w = 64
    tm = _token_tile(n_ctx, t_all)
    tile_args = (t_all // tm, n_ctx // tm, bsz)
    stream = jnp.concatenate([ctx, x], axis=1).astype(F32)
    rows = 8 * ((bsz + 1 + 7) // 8)
    cc = jnp.zeros((rows, d), F32).at[:bsz].set(c.astype(F32)).at[bsz].set(c_ctx.astype(F32))
    mod_all = _modulation(cc, w_mod, b_mod)
    for i in range(depth):
        kind, j = i % 3, i // 3
        mod = mod_all[i, :bsz + 1]
        shift, scale, gate = (mod[:, k * d:(k + 1) * d].reshape(bsz + 1, 1, d) for k in range(3))
        last = i == depth - 1
        if kind == 0:
            stream = _rwkv_layer(stream, norm_g[i], scale, shift, gate, n_ctx, rwkv_mu[j], rwkv_w_rkvg[j], rwkv_w0[j],
                                 rwkv_w1[j], rwkv_w2[j], rwkv_a0[j], rwkv_a1[j], rwkv_a2[j], rwkv_k_k[j],
                                 rwkv_k_a[j], rwkv_r_k[j], rwkv_ln_w[j], rwkv_ln_b[j], rwkv_w_out[j], last)
        elif kind == 1:
            stream = _na_layer(stream, norm_g[i], scale, shift, gate, n_ctx, tm, tile_args, grid_w, na_w_in[j],
                               na_q_g[j], na_k_g[j], na_rpb[j], na_w_out[j], last)
        else:
            stream = _s5_layer(stream, norm_g[i], scale, shift, gate, n_ctx, tm, tile_args, s5_w_in[j],
                               s5_lam_re[j], s5_lam_im[j], s5_log_dt[j], s5_b_re[j], s5_b_im[j], s5_c_re[j],
                               s5_c_im[j], s5_d[j], s5_w_glu[j], s5_b_glu[j], s5_w_out[j], last)
    return stream.astype(x.dtype)
```
